```python
import math
import jax, jax.numpy as jnp
from jax import lax
import numpy as np

D_MODEL = 2048
BATCH = 4
SEQ = 2048
DEPTH = 2

GRID_W = 64
CTX_LEN = 256
BLOCK = 128
WINDOW = 128
ROPE_THETA = 10000.0
EPS = 1e-6
NEG_INF = -1e30

A_HEADS = 4
A_DK = 64
A_DV = 2 * A_DK
B_HEADS = 8
B_KV_HEADS = 2
B_GROUP = B_HEADS // B_KV_HEADS
B_DH = 128
C_HEADS = 4
C_Q_RANK = 512
C_KV_RANK = 256
C_NOPE = 128
C_ROPE = 64
C_DQK = C_NOPE + C_ROPE
C_DV = 128

SPLIT_SIZES = (A_HEADS * 2 * A_DK, A_HEADS * 2 * A_DK, A_HEADS * A_DV,
               B_HEADS * B_DH, B_KV_HEADS * B_DH, B_KV_HEADS * B_DH,
               C_Q_RANK, C_KV_RANK, C_ROPE)
D_IN = sum(SPLIT_SIZES)
D_MIX = A_HEADS * A_DV + B_HEADS * B_DH + C_HEADS * C_DV

N_EXPERTS = 32
N_GROUPS = 4
EXPERTS_PER_GROUP = N_EXPERTS // N_GROUPS
TOP_K = 2
GROUP_SCORE_K = 2
D_EXPERT = 512

kernel_name = 'hybrid_diffusion_block'


def rmsnorm(x, g):
    xf = x.astype(jnp.float32)
    y = xf * lax.rsqrt(jnp.mean(xf * xf, axis=-1, keepdims=True) + EPS)
    return (y * g.astype(jnp.float32)).astype(x.dtype)


def modulate(h, shift, scale):
    return h * (1 + scale) + shift


def ada_modulation(cond, w, bias):
    m = jax.nn.silu(cond) @ w + bias
    return jnp.split(m, 6, axis=-1)


def axial_rope(rows, dim):
    t = jnp.arange(rows * GRID_W)
    r = (t // GRID_W).astype(jnp.float32)
    col = (t % GRID_W).astype(jnp.float32)
    nf = dim // 4
    inv = ROPE_THETA ** (-jnp.arange(nf, dtype=jnp.float32) / nf)
    ang = jnp.concatenate([r[:, None] * inv, col[:, None] * inv], axis=-1)
    return jnp.cos(ang), jnp.sin(ang)


def apply_rope(x, cos, sin):
    shape = (cos.shape[0],) + (1,) * (x.ndim - 3) + (cos.shape[-1],)
    cos, sin = cos.reshape(shape), sin.reshape(shape)
    xf = x.astype(jnp.float32)
    x1, x2 = jnp.split(xf, 2, axis=-1)
    return jnp.concatenate([x1 * cos - x2 * sin, x1 * sin + x2 * cos], axis=-1).astype(x.dtype)


def split_cols(z):
    parts, off = [], 0
    for size in SPLIT_SIZES:
        parts.append(z[..., off:off + size])
        off += size
    return parts


def mixer_inputs(h, w_in, a_qn, a_kn, b_qn, b_kn, c_qa_norm, c_kva_norm, c_wuq, c_wukv,
                 c_qn, c_kn, rope):
    bsz, n = h.shape[:2]
    aq, ak, av, bq, bk, bv, cq, ckv, ckr = split_cols(h @ w_in)
    aq = rmsnorm(aq.reshape(bsz, n, A_HEADS, 2, A_DK), a_qn)
    ak = rmsnorm(ak.reshape(bsz, n, A_HEADS, 2, A_DK), a_kn)
    av = av.reshape(bsz, n, A_HEADS, A_DV)
    bq = rmsnorm(bq.reshape(bsz, n, B_HEADS, B_DH), b_qn)
    bk = rmsnorm(bk.reshape(bsz, n, B_KV_HEADS, B_DH), b_kn)
    bv = bv.reshape(bsz, n, B_KV_HEADS, B_DH)
    q = (rmsnorm(cq, c_qa_norm) @ c_wuq).reshape(bsz, n, C_HEADS, C_DQK)
    kv = (rmsnorm(ckv, c_kva_norm) @ c_wukv).reshape(bsz, n, C_HEADS, C_NOPE + C_DV)
    q_nope = rmsnorm(q[..., :C_NOPE], c_qn[:C_NOPE])
    q_rope = rmsnorm(q[..., C_NOPE:], c_qn[C_NOPE:])
    k_nope = rmsnorm(kv[..., :C_NOPE], c_kn[:C_NOPE])
    cv = kv[..., C_NOPE:]
    k_rope = rmsnorm(ckr, c_kn[C_NOPE:])
    if rope is not None:
        (cos_a, sin_a), (cos_b, sin_b), (cos_c, sin_c) = rope
        aq, ak = apply_rope(aq, cos_a, sin_a), apply_rope(ak, cos_a, sin_a)
        bq, bk = apply_rope(bq, cos_b, sin_b), apply_rope(bk, cos_b, sin_b)
        q_rope = apply_rope(q_rope, cos_c, sin_c)
        k_rope = apply_rope(k_rope, cos_c, sin_c)
    cq_full = jnp.concatenate([q_nope, q_rope], axis=-1)
    ck_full = jnp.concatenate(
        [k_nope, jnp.broadcast_to(k_rope[:, :, None, :], (bsz, n, C_HEADS, C_ROPE))], axis=-1)
    return aq, ak, av, bq, bk, bv, cq_full, ck_full, cv


def sweep_blocks(fn, q):
    bsz, n = q.shape[:2]
    nb = n // BLOCK
    qb = jnp.moveaxis(q.reshape((bsz, nb, BLOCK) + q.shape[2:]), 1, 0)
    ob = lax.map(fn, qb)
    return jnp.moveaxis(ob, 0, 1).reshape((bsz, n) + ob.shape[3:])


def diff_lambda(lv, layer_idx):
    lam_init = 0.8 - 0.6 * math.exp(-0.3 * layer_idx)
    lv = lv.astype(jnp.float32)
    lam = jnp.exp(jnp.sum(lv[0] * lv[1])) - jnp.exp(jnp.sum(lv[2] * lv[3])) + lam_init
    return lam, lam_init


def diff_attention(q, k, v, lam):
    s = jnp.einsum('bqhmd,bkhmd->bhmqk', q, k).astype(jnp.float32) * (A_DK ** -0.5)
    p = jax.nn.softmax(s, axis=-1)
    p = p[:, :, 0] - lam * p[:, :, 1]
    return jnp.einsum('bhqk,bkhd->bqhd', p.astype(v.dtype), v)


def mha(q, k, v, scale):
    s = jnp.einsum('bqhd,bkhd->bhqk', q, k).astype(jnp.float32) * scale
    p = jax.nn.softmax(s, axis=-1)
    return jnp.einsum('bhqk,bkhd->bqhd', p.astype(v.dtype), v)


def swa_latent(q, k, v, k_ctx, v_ctx, sink):
    bsz, n = q.shape[:2]
    nb = n // BLOCK
    qb = q.reshape(bsz, nb, BLOCK, B_KV_HEADS, B_GROUP, B_DH)

    def band(t):
        tp = jnp.pad(t, ((0, 0), (BLOCK, BLOCK), (0, 0), (0, 0)))
        tp = tp.reshape(bsz, nb + 2, BLOCK, B_KV_HEADS, B_DH)
        return jnp.concatenate([tp[:, :nb], tp[:, 1:nb + 1], tp[:, 2:]], axis=2)

    kb, vb = band(k), band(v)
    qpos = jnp.arange(nb)[:, None] * BLOCK + jnp.arange(BLOCK)[None, :]
    kpos = (jnp.arange(nb)[:, None] - 1) * BLOCK + jnp.arange(3 * BLOCK)[None, :]
    valid = ((jnp.abs(qpos[:, :, None] - kpos[:, None, :]) <= WINDOW)
             & (kpos[:, None, :] >= 0) & (kpos[:, None, :] < n))
    scale = B_DH ** -0.5
    s_loc = jnp.einsum('bnqhgd,bnkhd->bnhgqk', qb, kb).astype(jnp.float32) * scale
    s_loc = jnp.where(valid[None, :, None, None], s_loc, NEG_INF)
    s_ctx = jnp.einsum('bnqhgd,bchd->bnhgqc', qb, k_ctx).astype(jnp.float32) * scale
    s_sink = jnp.broadcast_to(
        sink.astype(jnp.float32).reshape(1, 1, B_KV_HEADS, B_GROUP, 1, 1), s_loc.shape[:-1] + (1,))
    p = jax.nn.softmax(jnp.concatenate([s_loc, s_ctx, s_sink], axis=-1), axis=-1)
    n_loc = 3 * BLOCK
    p_loc = p[..., :n_loc].astype(v.dtype)
    p_ctx = p[..., n_loc:n_loc + k_ctx.shape[1]].astype(v.dtype)
    out = (jnp.einsum('bnhgqk,bnkhd->bnqhgd', p_loc, vb)
           + jnp.einsum('bnhgqc,bchd->bnqhgd', p_ctx, v_ctx))
    return out.reshape(bsz, n, B_HEADS, B_DH)


def swa_context(q, k, v, sink):
    bsz, n = q.shape[:2]
    qg = q.reshape(bsz, n, B_KV_HEADS, B_GROUP, B_DH)
    s = jnp.einsum('bqhgd,bkhd->bhgqk', qg, k).astype(jnp.float32) * (B_DH ** -0.5)
    s_sink = jnp.broadcast_to(
        sink.astype(jnp.float32).reshape(1, B_KV_HEADS, B_GROUP, 1, 1), s.shape[:-1] + (1,))
    p = jax.nn.softmax(jnp.concatenate([s, s_sink], axis=-1), axis=-1)[..., :-1]
    out = jnp.einsum('bhgqk,bkhd->bqhgd', p.astype(v.dtype), v)
    return out.reshape(bsz, n, B_HEADS, B_DH)


def moe(h, router_w, router_bias, w_gate, w_up, w_down):
    t = h.shape[0]
    scores = jax.nn.sigmoid(h.astype(jnp.float32) @ router_w.astype(jnp.float32))
    sel = scores + router_bias.astype(jnp.float32)
    gscore = lax.top_k(sel.reshape(t, N_GROUPS, EXPERTS_PER_GROUP), GROUP_SCORE_K)[0].sum(-1)
    gidx = jnp.argmax(gscore, axis=-1)
    in_group = (jnp.arange(N_EXPERTS) // EXPERTS_PER_GROUP)[None, :] == gidx[:, None]
    _, eidx = lax.top_k(jnp.where(in_group, sel, NEG_INF), TOP_K)
    w = jnp.take_along_axis(scores, eidx, axis=-1)
    w = w / jnp.sum(w, axis=-1, keepdims=True)
    gates = jnp.sum(jax.nn.one_hot(eidx, N_EXPERTS, dtype=jnp.float32) * w[..., None], axis=1)
    gates = gates.astype(h.dtype)
    y = jnp.zeros_like(h)
    for e in range(N_EXPERTS):
        a = jax.nn.silu(h @ w_gate[e]) * (h @ w_up[e])
        y = y + gates[:, e:e + 1] * (a @ w_down[e])
    return y


def setup_inputs(seed: int = 0) -> dict:
    key = jax.random.key(seed)
    keys = iter(jax.random.split(key, 40))

    def nrm(shape, scale):
        return jax.random.normal(next(keys), shape, jnp.float32) * scale

    def gain(shape):
        return 1.0 + nrm(shape, 0.05)

    D = D_MODEL
    return {
        'x': nrm((BATCH, SEQ, D), 1.0),
        'c': nrm((BATCH, D), 1.0),
        'ctx': nrm((BATCH, CTX_LEN, D), 1.0),
        'c_ctx': nrm((D,), 1.0),
        'ada_w': nrm((DEPTH, D, 6 * D), 0.5 * D ** -0.5),
        'ada_b': nrm((DEPTH, 6 * D), 0.01),
        'norm1_g': gain((DEPTH, D)),
        'norm2_g': gain((DEPTH, D)),
        'w_in': nrm((DEPTH, D, D_IN), D ** -0.5),
        'w_out': nrm((DEPTH, D_MIX, D), D_MIX ** -0.5),
        'a_qn': gain((DEPTH, A_DK)),
        'a_kn': gain((DEPTH, A_DK)),
        'a_lambda': nrm((DEPTH, 4, A_DK), 0.1),
        'a_subln': gain((DEPTH, A_DV)),
        'b_qn': gain((DEPTH, B_DH)),
        'b_kn': gain((DEPTH, B_DH)),
        'b_sink': nrm((DEPTH, B_HEADS), 1.0),
        'c_qa_norm': gain((DEPTH, C_Q_RANK)),
        'c_kva_norm': gain((DEPTH, C_KV_RANK)),
        'c_wuq': nrm((DEPTH, C_Q_RANK, C_HEADS * C_DQK), C_Q_RANK ** -0.5),
        'c_wukv': nrm((DEPTH, C_KV_RANK, C_HEADS * (C_NOPE + C_DV)), C_KV_RANK ** -0.5),
        'c_qn': gain((DEPTH, C_DQK)),
        'c_kn': gain((DEPTH, C_DQK)),
        'router_w': nrm((D, N_EXPERTS), D ** -0.5),
        'router_bias': nrm((N_EXPERTS,), 0.01),
        'moe_w_gate': nrm((DEPTH, N_EXPERTS, D, D_EXPERT), D ** -0.5),
        'moe_w_up': nrm((DEPTH, N_EXPERTS, D, D_EXPERT), D ** -0.5),
        'moe_w_down': nrm((DEPTH, N_EXPERTS, D_EXPERT, D), D_EXPERT ** -0.5),
    }


def reference(x, c, ctx, c_ctx, ada_w, ada_b, norm1_g, norm2_g, w_in, w_out,
              a_qn, a_kn, a_lambda, a_subln, b_qn, b_kn, b_sink,
              c_qa_norm, c_kva_norm, c_wuq, c_wukv, c_qn, c_kn,
              router_w, router_bias, moe_w_gate, moe_w_up, moe_w_down):
    bsz, n_lat, d = x.shape
    n_ctx = ctx.shape[1]
    rows = n_lat // GRID_W
    rope = (axial_rope(rows, A_DK), axial_rope(rows, B_DH), axial_rope(rows, C_ROPE))
    xc = ctx
    for l in range(DEPTH):
        last = l == DEPTH - 1
        sh1, sc1, g1, sh2, sc2, g2 = [m[:, None, :] for m in ada_modulation(c, ada_w[l], ada_b[l])]
        sh1c, sc1c, g1c, sh2c, sc2c, g2c = ada_modulation(c_ctx, ada_w[l], ada_b[l])
        lp = (w_in[l], a_qn[l], a_kn[l], b_qn[l], b_kn[l], c_qa_norm[l], c_kva_norm[l],
              c_wuq[l], c_wukv[l], c_qn[l], c_kn[l])

        h = modulate(rmsnorm(x, norm1_g[l]), sh1, sc1)
        hc = modulate(rmsnorm(xc, norm1_g[l]), sh1c, sc1c)
        aq, ak, av, bq, bk, bv, cq, ck, cv = mixer_inputs(h, *lp, rope=rope)
        aqc, akc, avc, bqc, bkc, bvc, cqc, ckc, cvc = mixer_inputs(hc, *lp, rope=None)
        lam, lam_init = diff_lambda(a_lambda[l], l)

        ak_all, av_all = jnp.concatenate([ak, akc], axis=1), jnp.concatenate([av, avc], axis=1)
        ck_all, cv_all = jnp.concatenate([ck, ckc], axis=1), jnp.concatenate([cv, cvc], axis=1)
        o_a = sweep_blocks(lambda qb: diff_attention(qb, ak_all, av_all, lam), aq)
        o_a = rmsnorm(o_a, a_subln[l]) * (1.0 - lam_init)
        o_b = swa_latent(bq, bk, bv, bkc, bvc, b_sink[l])
        o_c = sweep_blocks(lambda qb: mha(qb, ck_all, cv_all, C_DQK ** -0.5), cq)
        mix = jnp.concatenate([o_a.reshape(bsz, n_lat, -1), o_b.reshape(bsz, n_lat, -1),
                               o_c.reshape(bsz, n_lat, -1)], axis=-1)
        x = x + g1 * (mix @ w_out[l])

        if not last:
            oc_a = rmsnorm(diff_attention(aqc, akc, avc, lam), a_subln[l]) * (1.0 - lam_init)
            oc_b = swa_context(bqc, bkc, bvc, b_sink[l])
            oc_c = mha(cqc, ckc, cvc, C_DQK ** -0.5)
            mixc = jnp.concatenate([oc_a.reshape(bsz, n_ctx, -1), oc_b.reshape(bsz, n_ctx, -1),
                                    oc_c.reshape(bsz, n_ctx, -1)], axis=-1)
            xc = xc + g1c * (mixc @ w_out[l])

        h2 = modulate(rmsnorm(x, norm2_g[l]), sh2, sc2).reshape(bsz * n_lat, d)
        if not last:
            h2c = modulate(rmsnorm(xc, norm2_g[l]), sh2c, sc2c).reshape(bsz * n_ctx, d)
            y = moe(jnp.concatenate([h2, h2c], axis=0), router_w, router_bias,
                    moe_w_gate[l], moe_w_up[l], moe_w_down[l])
            y_lat = y[:bsz * n_lat].reshape(bsz, n_lat, d)
            xc = xc + g2c * y[bsz * n_lat:].reshape(bsz, n_ctx, d)
        else:
            y_lat = moe(h2, router_w, router_bias, moe_w_gate[l], moe_w_up[l],
                        moe_w_down[l]).reshape(bsz, n_lat, d)
        x = x + g2 * y_lat
    return x
```

```python
import functools
import math

import jax
import jax.numpy as jnp
from jax import lax
from jax.experimental import pallas as pl
from jax.experimental.pallas import tpu as pltpu

F32 = jnp.float32
BF16 = jnp.bfloat16

D_MODEL = 2048
GRID_W = 64
BLOCK = 128
WINDOW = 128
ROPE_THETA = 10000.0
EPS = 1e-6
NEG_INF = -1e30
A_HEADS, A_DK = 4, 64
A_DV = 2 * A_DK
B_HEADS, B_KV_HEADS, B_DH = 8, 2, 128
B_GROUP = B_HEADS // B_KV_HEADS
C_HEADS, C_Q_RANK, C_KV_RANK, C_NOPE, C_ROPE, C_DV = 4, 512, 256, 128, 64, 128
C_DQK = C_NOPE + C_ROPE
SPLIT_SIZES = (A_HEADS * 2 * A_DK, A_HEADS * 2 * A_DK, A_HEADS * A_DV,
               B_HEADS * B_DH, B_KV_HEADS * B_DH, B_KV_HEADS * B_DH,
               C_Q_RANK, C_KV_RANK, C_ROPE)
D_IN = sum(SPLIT_SIZES)
N_EXPERTS, N_GROUPS, TOP_K = 32, 4, 2
EXPERTS_PER_GROUP = N_EXPERTS // N_GROUPS
D_EXPERT = 512

LANES = 128
V7X_VMEM_LIMIT = 56 * 1024 * 1024

D_IN_PAD = ((D_IN + LANES - 1) // LANES) * LANES
C_HEAD_PAD = 2 * LANES
TM_PREP = 256
TQ_A = 256
TQ_C = 512
QB_B = 512
TM_E = 128
TM_C = 128
ADA_TN = 1024

_OFF = [0]
for _s in SPLIT_SIZES:
    _OFF.append(_OFF[-1] + _s)
O_AQ, O_AK, O_AV, O_BQ, O_BK, O_BV, O_CQ, O_CKV, O_CKR, _ = _OFF


def _cparams(sem):
    return pltpu.CompilerParams(dimension_semantics=sem, vmem_limit_bytes=V7X_VMEM_LIMIT)


def _silu(v):
    return v * (1.0 / (1.0 + jnp.exp(-v)))


def _ada_kernel(cond_ref, w_ref, b_ref, o_ref):
    s = _silu(cond_ref[...]).astype(BF16)
    o_ref[0] = jnp.dot(s, w_ref[0].astype(BF16), preferred_element_type=F32) + b_ref[0]


def _ada_modulation(cond8, ada_w, ada_b):
    depth, d, n = ada_w.shape
    return pl.pallas_call(
        _ada_kernel,
        out_shape=jax.ShapeDtypeStruct((depth, 8, n), F32),
        grid=(depth, n // ADA_TN),
        in_specs=[
            pl.BlockSpec((8, d), lambda l, j: (0, 0)),
            pl.BlockSpec((1, d, ADA_TN), lambda l, j: (l, 0, j)),
            pl.BlockSpec((1, 1, ADA_TN), lambda l, j: (l, 0, j)),
        ],
        out_specs=pl.BlockSpec((1, 8, ADA_TN), lambda l, j: (l, 0, j)),
        compiler_params=_cparams(("arbitrary", "arbitrary")),
        name="ada_modulation",
    )(cond8, ada_w, ada_b.reshape(depth, 1, n))


def _rope64(v, c, sa, sb):
    return v * c + pltpu.roll(v, 96, 1) * sa + pltpu.roll(v, 32, 1) * sb


def _rope128(v, c, s):
    return v * c + pltpu.roll(v, 64, 1) * s


def _norm_seg128(v, g):
    ms = jnp.sum(v * v, axis=-1, keepdims=True) * (1.0 / 128)
    return v * lax.rsqrt(ms + EPS) * g


def _norm_seg64x2(v, g, lo):
    sq = v * v
    s_lo = jnp.sum(jnp.where(lo, sq, 0.0), axis=-1, keepdims=True)
    s_hi = jnp.sum(jnp.where(lo, 0.0, sq), axis=-1, keepdims=True)
    ms = jnp.where(lo, s_lo, s_hi) * (1.0 / 64)
    return v * lax.rsqrt(ms + EPS) * g


def _norm_low64(v, g):
    ms = jnp.sum(v * v, axis=-1, keepdims=True) * (1.0 / 64)
    return v * lax.rsqrt(ms + EPS) * g


def _prep_kernel(xa_ref, xb_ref, sh_ref, sc_ref, g1_ref, w_ref, ca_ref, saa_ref, sab_ref, cb_ref, sb_ref,
                 gains_ref, gcq_ref, gckv_ref, wuq_ref, wukv_ref,
                 aq_ref, ak_ref, av_ref, bq_ref, bk_ref, bv_ref, cq_ref, ck_ref, cv_ref, *, n_a_tiles):
    x = jnp.where(pl.program_id(0) < n_a_tiles, xa_ref[...], xb_ref[...])
    ms = jnp.mean(x * x, axis=-1, keepdims=True)
    h = x * lax.rsqrt(ms + EPS) * g1_ref[...]
    h = h * (1.0 + sc_ref[0]) + sh_ref[0]
    z = jnp.dot(h.astype(BF16), w_ref[...], preferred_element_type=F32)

    lane = lax.broadcasted_iota(jnp.int32, (1, LANES), 1)
    lo = lane < 64
    ca, saa, sab = ca_ref[...], saa_ref[...], sab_ref[...]
    cb, sb = cb_ref[...], sb_ref[...]
    g_aq, g_ak, g_bq, g_bk = gains_ref[0:1], gains_ref[1:2], gains_ref[2:3], gains_ref[3:4]
    g_cqn, g_cqr, g_ckn, g_ckr = gains_ref[4:5], gains_ref[5:6], gains_ref[6:7], gains_ref[7:8]

    def blk(off, j):
        return z[:, off + j * LANES: off + (j + 1) * LANES]

    for j in range(A_HEADS):
        sl = slice(j * LANES, (j + 1) * LANES)
        aq_ref[:, sl] = _rope64(_norm_seg64x2(blk(O_AQ, j), g_aq, lo), ca, saa, sab).astype(BF16)
        ak_ref[:, sl] = _rope64(_norm_seg64x2(blk(O_AK, j), g_ak, lo), ca, saa, sab).astype(BF16)
        av_ref[:, sl] = blk(O_AV, j).astype(BF16)
    for j in range(B_HEADS):
        sl = slice(j * LANES, (j + 1) * LANES)
        bq_ref[:, sl] = _rope128(_norm_seg128(blk(O_BQ, j), g_bq), cb, sb).astype(BF16)
    for j in range(B_KV_HEADS):
        sl = slice(j * LANES, (j + 1) * LANES)
        bk_ref[:, sl] = _rope128(_norm_seg128(blk(O_BK, j), g_bk), cb, sb).astype(BF16)
        bv_ref[:, sl] = blk(O_BV, j).astype(BF16)
    cq = z[:, O_CQ:O_CQ + C_Q_RANK]
    cqn = cq * lax.rsqrt(jnp.mean(cq * cq, axis=-1, keepdims=True) + EPS) * gcq_ref[...]
    q = jnp.dot(cqn.astype(BF16), wuq_ref[...], preferred_element_type=F32)
    ckv = z[:, O_CKV:O_CKV + C_KV_RANK]
    ckvn = ckv * lax.rsqrt(jnp.mean(ckv * ckv, axis=-1, keepdims=True) + EPS) * gckv_ref[...]
    kv = jnp.dot(ckvn.astype(BF16), wukv_ref[...], preferred_element_type=F32)
    krope = _rope64(_norm_low64(z[:, O_CKR:O_CKR + LANES], g_ckr), ca, saa, sab).astype(BF16)
    for hh in range(C_HEADS):
        b0 = hh * C_HEAD_PAD
        cq_ref[:, b0:b0 + LANES] = _norm_seg128(q[:, b0:b0 + LANES], g_cqn).astype(BF16)
        cq_ref[:, b0 + LANES:b0 + 2 * LANES] = _rope64(
            _norm_low64(q[:, b0 + LANES:b0 + 2 * LANES], g_cqr), ca, saa, sab).astype(BF16)
        ck_ref[:, b0:b0 + LANES] = _norm_seg128(kv[:, b0:b0 + LANES], g_ckn).astype(BF16)
        ck_ref[:, b0 + LANES:b0 + 2 * LANES] = krope
        cv_ref[:, hh * LANES:(hh + 1) * LANES] = kv[:, b0 + LANES:b0 + 2 * LANES].astype(BF16)


def _split_rows(n_a_tiles):
    first = lambda i: (jnp.minimum(i, n_a_tiles - 1), 0)
    second = lambda i: (jnp.maximum(i - n_a_tiles, 0), 0)
    return first, second


def _prep(xa, xb, n_tiles, n_a_tiles, mod48, mod_row, g1, w_in_bf, tables, rope_blk, gains, gcq, gckv,
          wuq_bf, wukv_bf):
    d = xa.shape[1]
    tm = TM_PREP
    m = n_tiles * tm
    row = lambda i: (i, 0)
    const = lambda i: (0, 0)
    first, second = _split_rows(n_a_tiles)
    tab_spec = pl.BlockSpec((tm, LANES), lambda i: (rope_blk(i), 0))
    widths = (512, 512, 512, 1024, 256, 256, C_HEADS * C_HEAD_PAD, C_HEADS * C_HEAD_PAD, 512)
    return pl.pallas_call(
        functools.partial(_prep_kernel, n_a_tiles=n_a_tiles),
        out_shape=[jax.ShapeDtypeStruct((m, w), BF16) for w in widths],
        grid=(n_tiles,),
        in_specs=[
            pl.BlockSpec((tm, d), first),
            pl.BlockSpec((tm, d), second),
            pl.BlockSpec((1, 1, d), lambda i: (mod_row(i) * 6 + 0, 0, 0)),
            pl.BlockSpec((1, 1, d), lambda i: (mod_row(i) * 6 + 1, 0, 0)),
            pl.BlockSpec((1, d), const),
            pl.BlockSpec((d, D_IN_PAD), const, pipeline_mode=pl.Buffered(1)),
            tab_spec, tab_spec, tab_spec, tab_spec, tab_spec,
            pl.BlockSpec((8, LANES), const),
            pl.BlockSpec((1, C_Q_RANK), const),
            pl.BlockSpec((1, C_KV_RANK), const),
            pl.BlockSpec((C_Q_RANK, C_HEADS * C_HEAD_PAD), const, pipeline_mode=pl.Buffered(1)),
            pl.BlockSpec((C_KV_RANK, C_HEADS * C_HEAD_PAD), const, pipeline_mode=pl.Buffered(1)),
        ],
        out_specs=[pl.BlockSpec((tm, w), row) for w in widths],
        compiler_params=_cparams(("arbitrary",)),
        name="prep",
    )(xa, xb, mod48, mod48, g1, w_in_bf, *tables, gains, gcq, gckv, wuq_bf, wukv_bf)


def _dot_nt(a, b):
    return lax.dot_general(a, b, (((1,), (1,)), ((), ())), preferred_element_type=F32)


def _attn_a_kernel(*refs, n_src, lam_init):
    lv_ref, gsub_ref, q_ref = refs[0], refs[1], refs[2]
    k_refs = refs[3:3 + n_src]
    v_refs = refs[3 + n_src:3 + 2 * n_src]
    o_ref = refs[3 + 2 * n_src]
    tq = q_ref.shape[0]
    lv = lv_ref[...]
    lam = (jnp.exp(jnp.sum(lv[0:1] * lv[1:2], axis=-1, keepdims=True))
           - jnp.exp(jnp.sum(lv[2:3] * lv[3:4], axis=-1, keepdims=True)) + lam_init)
    q = q_ref[...]
    lo = lax.broadcasted_iota(jnp.int32, (1, LANES), 1) < 64
    zero = jnp.zeros_like(q)
    qq = jnp.concatenate([jnp.where(lo, q, zero), jnp.where(lo, zero, q)], axis=0)
    s = [_dot_nt(qq, k[...]) for k in k_refs]
    m = functools.reduce(jnp.maximum, [jnp.max(si, axis=-1, keepdims=True) for si in s])
    e = [jnp.exp(si - m) for si in s]
    l = functools.reduce(jnp.add, [jnp.sum(ei, axis=-1, keepdims=True) for ei in e])
    o2 = functools.reduce(jnp.add, [jnp.dot(ei.astype(BF16), v[...], preferred_element_type=F32)
                                    for ei, v in zip(e, v_refs)])
    o2 = o2 * (1.0 / l)
    o = o2[:tq] - lam * o2[tq:]
    ms = jnp.mean(o * o, axis=-1, keepdims=True)
    o_ref[...] = (o * lax.rsqrt(ms + EPS) * gsub_ref[...] * (1.0 - lam_init)).astype(BF16)


def _attn_a(lv, gsub, aq, ak, av, q_row0, q_rows, srcs, n_batch, layer_idx):
    tq = min(TQ_A, q_rows)
    nq = q_rows // tq
    qb0 = q_row0 // tq
    lam_init = 0.8 - 0.6 * math.exp(-0.3 * layer_idx)
    in_specs = [
        pl.BlockSpec((4, A_DK), lambda b, h, i: (0, 0)),
        pl.BlockSpec((1, A_DV), lambda b, h, i: (0, 0)),
        pl.BlockSpec((tq, LANES), lambda b, h, i: (qb0 + b * nq + i, h)),
    ]
    kv_specs = [pl.BlockSpec((rows, LANES), lambda b, h, i, blk0=row0 // rows: (blk0 + b, h))
                for row0, rows in srcs]
    return pl.pallas_call(
        functools.partial(_attn_a_kernel, n_src=len(srcs), lam_init=lam_init),
        out_shape=jax.ShapeDtypeStruct((n_batch * q_rows, A_HEADS * A_DV), BF16),
        grid=(n_batch, A_HEADS, nq),
        in_specs=in_specs + kv_specs + kv_specs,
        out_specs=pl.BlockSpec((tq, LANES), lambda b, h, i: (b * nq + i, h)),
        compiler_params=_cparams(("arbitrary", "arbitrary", "arbitrary")),
        name="attn_a",
    )(lv, gsub, aq, *([ak] * len(srcs)), *([av] * len(srcs)))


def _attn_c_kernel(*refs, n_src):
    q_ref = refs[0]
    k_refs = refs[1:1 + n_src]
    v_refs = refs[1 + n_src:1 + 2 * n_src]
    o_ref = refs[1 + 2 * n_src]
    q = q_ref[...]
    s = [_dot_nt(q, k[...]) for k in k_refs]
    m = functools.reduce(jnp.maximum, [jnp.max(si, axis=-1, keepdims=True) for si in s])
    e = [jnp.exp(si - m) for si in s]
    l = functools.reduce(jnp.add, [jnp.sum(ei, axis=-1, keepdims=True) for ei in e])
    o = functools.reduce(jnp.add, [jnp.dot(ei.astype(BF16), v[...], preferred_element_type=F32)
                                   for ei, v in zip(e, v_refs)])
    o_ref[...] = (o * (1.0 / l)).astype(BF16)


def _attn_c(cq, ck, cv, q_row0, q_rows, srcs, n_batch):
    tq = min(TQ_C, q_rows)
    nq = q_rows // tq
    qb0 = q_row0 // tq
    in_specs = [pl.BlockSpec((tq, C_HEAD_PAD), lambda b, h, i: (qb0 + b * nq + i, h))]
    k_specs = [pl.BlockSpec((rows, C_HEAD_PAD), lambda b, h, i, blk0=row0 // rows: (blk0 + b, h))
               for row0, rows in srcs]
    v_specs = [pl.BlockSpec((rows, C_DV), lambda b, h, i, blk0=row0 // rows: (blk0 + b, h))
               for row0, rows in srcs]
    return pl.pallas_call(
        functools.partial(_attn_c_kernel, n_src=len(srcs)),
        out_shape=jax.ShapeDtypeStruct((n_batch * q_rows, C_HEADS * C_DV), BF16),
        grid=(n_batch, C_HEADS, nq),
        in_specs=in_specs + k_specs + v_specs,
        out_specs=pl.BlockSpec((tq, C_DV), lambda b, h, i: (b * nq + i, h)),
        compiler_params=_cparams(("arbitrary", "arbitrary", "arbitrary")),
        name="attn_c",
    )(cq, *([ck] * len(srcs)), *([cv] * len(srcs)))


def _stack_heads(q):
    return jnp.concatenate([q[:, g * LANES:(g + 1) * LANES] for g in range(B_GROUP)], axis=0)


def _sink_rows(sink_ref, kvh, rows):
    return jnp.concatenate(
        [jnp.full((rows, 1), sink_ref[kvh * B_GROUP + g], F32) for g in range(B_GROUP)], axis=0)


def _attn_b_kernel(sink_ref, q_ref, kp_ref, km_ref, kn_ref, vp_ref, vm_ref, vn_ref, kc_ref, vc_ref, o_ref):
    kvh = pl.program_id(1)
    qb = pl.program_id(2)
    nqb = pl.num_programs(2)
    n_blk = QB_B // BLOCK
    kband = jnp.concatenate([kp_ref[...], km_ref[...], kn_ref[...]], axis=0)
    vband = jnp.concatenate([vp_ref[...], vm_ref[...], vn_ref[...]], axis=0)
    kc, vc = kc_ref[...], vc_ref[...]
    sink = _sink_rows(sink_ref, kvh, BLOCK)
    r = lax.broadcasted_iota(jnp.int32, (B_GROUP * BLOCK, 3 * BLOCK), 0) % BLOCK
    c = lax.broadcasted_iota(jnp.int32, (B_GROUP * BLOCK, 3 * BLOCK), 1)
    cr = c - r
    band_ok = (cr >= 0) & (cr <= BLOCK + WINDOW)
    for j in range(n_blk):
        q4 = _stack_heads(q_ref[j * BLOCK:(j + 1) * BLOCK, :])
        s_loc = _dot_nt(q4, kband[j * BLOCK:(j + 3) * BLOCK])
        valid = band_ok
        if j == 0:
            valid = valid & (c >= jnp.where(qb > 0, 0, BLOCK))
        if j == n_blk - 1:
            valid = valid & (c < jnp.where(qb < nqb - 1, 3 * BLOCK, 2 * BLOCK))
        s_loc = jnp.where(valid, s_loc, NEG_INF)
        s_ctx = _dot_nt(q4, kc)
        m = jnp.maximum(jnp.maximum(jnp.max(s_loc, axis=-1, keepdims=True),
                                    jnp.max(s_ctx, axis=-1, keepdims=True)), sink)
        e_loc, e_ctx = jnp.exp(s_loc - m), jnp.exp(s_ctx - m)
        l = (jnp.sum(e_loc, axis=-1, keepdims=True) + jnp.sum(e_ctx, axis=-1, keepdims=True)
             + jnp.exp(sink - m))
        o = (jnp.dot(e_loc.astype(BF16), vband[j * BLOCK:(j + 3) * BLOCK], preferred_element_type=F32)
             + jnp.dot(e_ctx.astype(BF16), vc, preferred_element_type=F32)) * (1.0 / l)
        for g in range(B_GROUP):
            o_ref[j * BLOCK:(j + 1) * BLOCK, g * LANES:(g + 1) * LANES] = (
                o[g * BLOCK:(g + 1) * BLOCK].astype(BF16))


def _attn_b(sink, bq, bk, bv, n_batch, n_lat, n_ctx):
    nqb = n_lat // QB_B
    per = QB_B // BLOCK
    blocks_per_batch = n_lat // BLOCK
    ctx_blk0 = n_batch * n_lat // n_ctx
    gw = B_GROUP * B_DH
    prev = lambda b, h, i: (b * blocks_per_batch + jnp.maximum(i * per - 1, 0), h)
    main = lambda b, h, i: (b * nqb + i, h)
    nxt = lambda b, h, i: (b * blocks_per_batch + jnp.minimum(i * per + per, blocks_per_batch - 1), h)
    ctx = lambda b, h, i: (ctx_blk0 + b, h)
    return pl.pallas_call(
        _attn_b_kernel,
        out_shape=jax.ShapeDtypeStruct((n_batch * n_lat, B_HEADS * B_DH), BF16),
        grid=(n_batch, B_KV_HEADS, nqb),
        in_specs=[
            pl.BlockSpec(memory_space=pltpu.SMEM),
            pl.BlockSpec((QB_B, gw), main),
            pl.BlockSpec((BLOCK, B_DH), prev), pl.BlockSpec((QB_B, B_DH), main), pl.BlockSpec((BLOCK, B_DH), nxt),
            pl.BlockSpec((BLOCK, B_DH), prev), pl.BlockSpec((QB_B, B_DH), main), pl.BlockSpec((BLOCK, B_DH), nxt),
            pl.BlockSpec((n_ctx, B_DH), ctx), pl.BlockSpec((n_ctx, B_DH), ctx),
        ],
        out_specs=pl.BlockSpec((QB_B, gw), main),
        compiler_params=_cparams(("arbitrary", "arbitrary", "arbitrary")),
        name="attn_b",
    )(sink, bq, bk, bk, bk, bv, bv, bv, bk, bv)


def _attn_b_ctx_kernel(sink_ref, q_ref, k_ref, v_ref, o_ref):
    kvh = pl.program_id(1)
    rows = q_ref.shape[0]
    q4 = _stack_heads(q_ref[...])
    sink = _sink_rows(sink_ref, kvh, rows)
    s = _dot_nt(q4, k_ref[...])
    m = jnp.maximum(jnp.max(s, axis=-1, keepdims=True), sink)
    e = jnp.exp(s - m)
    l = jnp.sum(e, axis=-1, keepdims=True) + jnp.exp(sink - m)
    o = jnp.dot(e.astype(BF16), v_ref[...], preferred_element_type=F32) * (1.0 / l)
    for g in range(B_GROUP):
        o_ref[:, g * LANES:(g + 1) * LANES] = o[g * rows:(g + 1) * rows].astype(BF16)


def _attn_b_ctx(sink, bq, bk, bv, n_batch, n_lat, n_ctx):
    gw = B_GROUP * B_DH
    ctx_blk0 = n_batch * n_lat // n_ctx
    return pl.pallas_call(
        _attn_b_ctx_kernel,
        out_shape=jax.ShapeDtypeStruct((n_batch * n_ctx, B_HEADS * B_DH), BF16),
        grid=(n_batch, B_KV_HEADS),
        in_specs=[
            pl.BlockSpec(memory_space=pltpu.SMEM),
            pl.BlockSpec((n_ctx, gw), lambda b, h: (ctx_blk0 + b, h)),
            pl.BlockSpec((n_ctx, B_DH), lambda b, h: (ctx_blk0 + b, h)),
            pl.BlockSpec((n_ctx, B_DH), lambda b, h: (ctx_blk0 + b, h)),
        ],
        out_specs=pl.BlockSpec((n_ctx, gw), lambda b, h: (b, h)),
        compiler_params=_cparams(("arbitrary", "arbitrary")),
        name="attn_b_ctx",
    )(sink, bq, bk, bv)


def _route(h2, rw_ref, rb_ref):
    logits = jnp.dot(h2, rw_ref[...], preferred_element_type=F32, precision=lax.Precision.HIGHEST)
    scores = 1.0 / (1.0 + jnp.exp(-logits))
    sel = scores + rb_ref[...]
    lane_i = lax.broadcasted_iota(jnp.int32, sel.shape, 1)
    lane = lane_i.astype(F32)
    big = float(N_EXPERTS)

    def top2(mask):
        v = jnp.where(mask, sel, -jnp.inf)
        m1 = jnp.max(v, axis=-1, keepdims=True)
        i1 = jnp.min(jnp.where(v == m1, lane, big), axis=-1, keepdims=True)
        v2 = jnp.where(lane == i1, -jnp.inf, v)
        m2 = jnp.max(v2, axis=-1, keepdims=True)
        i2 = jnp.min(jnp.where(v2 == m2, lane, big), axis=-1, keepdims=True)
        return m1, i1, m2, i2

    best = None
    for g in range(N_GROUPS):
        m1, i1, m2, i2 = top2((lane_i >= g * EXPERTS_PER_GROUP) & (lane_i < (g + 1) * EXPERTS_PER_GROUP))
        gs = m1 + m2
        if best is None:
            best = (gs, i1, i2)
        else:
            take = gs > best[0]
            best = (jnp.where(take, gs, best[0]), jnp.where(take, i1, best[1]), jnp.where(take, i2, best[2]))
    _, e1, e2 = best
    w1 = jnp.sum(jnp.where(lane == e1, scores, 0.0), axis=-1, keepdims=True)
    w2 = jnp.sum(jnp.where(lane == e2, scores, 0.0), axis=-1, keepdims=True)
    tot = w1 + w2
    return e1, e2, w1 / tot, w2 / tot


def _outproj_kernel(oa1_ref, ob1_ref, oc1_ref, x1in_ref, oa2_ref, ob2_ref, oc2_ref, x2in_ref,
                    g1_ref, sh2_ref, sc2_ref, n2_ref, w_ref, rw_ref, rb_ref,
                    x1_ref, h2_ref, idx_ref, wt_ref, *, n_a_tiles):
    first = pl.program_id(0) < n_a_tiles
    oa = jnp.where(first, oa1_ref[...], oa2_ref[...])
    ob = jnp.where(first, ob1_ref[...], ob2_ref[...])
    oc = jnp.where(first, oc1_ref[...], oc2_ref[...])
    x = jnp.where(first, x1in_ref[...], x2in_ref[...])
    na, nb = oa.shape[1], ob.shape[1]
    y = (jnp.dot(oa, w_ref[0:na, :], preferred_element_type=F32)
         + jnp.dot(ob, w_ref[na:na + nb, :], preferred_element_type=F32)
         + jnp.dot(oc, w_ref[na + nb:, :], preferred_element_type=F32))
    x1 = x + g1_ref[0] * y
    x1_ref[...] = x1
    ms = jnp.mean(x1 * x1, axis=-1, keepdims=True)
    h2 = x1 * lax.rsqrt(ms + EPS) * n2_ref[...]
    h2 = h2 * (1.0 + sc2_ref[0]) + sh2_ref[0]
    h2_ref[...] = h2
    e1, e2, w1, w2 = _route(h2, rw_ref, rb_ref)
    lane = lax.broadcasted_iota(jnp.int32, idx_ref.shape, 1)
    idx_ref[...] = jnp.where(lane == 0, e1, jnp.where(lane == 1, e2, 0.0)).astype(jnp.int32)
    wt_ref[...] = jnp.where(lane == 0, w1, jnp.where(lane == 1, w2, 0.0))


def _outproj(set_a, set_b, n_tiles, n_a_tiles, mod48, mod_row, n2, w_out_bf, router_w, router_b):
    d = set_a[3].shape[1]
    tm = TM_PREP
    m = n_tiles * tm
    row = lambda i: (i, 0)
    const = lambda i: (0, 0)
    first, second = _split_rows(n_a_tiles)
    mod_spec = lambda j: pl.BlockSpec((1, 1, d), lambda i: (mod_row(i) * 6 + j, 0, 0))
    in_specs = (
        [pl.BlockSpec((tm, a.shape[1]), first) for a in set_a]
        + [pl.BlockSpec((tm, a.shape[1]), second) for a in set_b]
        + [mod_spec(2), mod_spec(3), mod_spec(4),
           pl.BlockSpec((1, d), const),
           pl.BlockSpec((d, d), const, pipeline_mode=pl.Buffered(1)),
           pl.BlockSpec((d, N_EXPERTS), const),
           pl.BlockSpec((1, N_EXPERTS), const)])
    return pl.pallas_call(
        functools.partial(_outproj_kernel, n_a_tiles=n_a_tiles),
        out_shape=[jax.ShapeDtypeStruct((m, d), F32), jax.ShapeDtypeStruct((m, d), F32),
                   jax.ShapeDtypeStruct((m, LANES), jnp.int32), jax.ShapeDtypeStruct((m, LANES), F32)],
        grid=(n_tiles,),
        in_specs=in_specs,
        out_specs=[pl.BlockSpec((tm, d), row), pl.BlockSpec((tm, d), row),
                   pl.BlockSpec((tm, LANES), row), pl.BlockSpec((tm, LANES), row)],
        compiler_params=_cparams(("arbitrary",)),
        name="outproj",
    )(*set_a, *set_b, mod48, mod48, mod48, n2, w_out_bf, router_w, router_b)


def _experts_kernel(te_ref, src_ref, h2_hbm, gate_ref, wg_ref, wu_ref, wd_ref, o_ref,
                    xbuf, sem, wg_bf, wu_bf, wd_bf):
    i = pl.program_id(0)
    nt = pl.num_programs(0)
    slot = i % 2

    def row_copy(tile, r, s):
        tok = src_ref[tile * TM_E + r]
        return pltpu.make_async_copy(h2_hbm.at[pl.ds(tok, 1), :], xbuf.at[s, pl.ds(r, 1), :], sem.at[s])

    def issue(tile, s):
        def body(r, carry):
            row_copy(tile, r, s).start()
            return carry
        lax.fori_loop(0, TM_E, body, 0)

    @pl.when(i == 0)
    def _():
        issue(0, 0)

    @pl.when(i + 1 < nt)
    def _():
        issue(i + 1, 1 - slot)

    changed = jnp.logical_or(i == 0, te_ref[i] != te_ref[jnp.maximum(i - 1, 0)])

    @pl.when(changed)
    def _():
        wg_bf[...] = wg_ref[0].astype(BF16)
        wu_bf[...] = wu_ref[0].astype(BF16)
        wd_bf[...] = wd_ref[0].astype(BF16)

    pltpu.make_async_copy(h2_hbm.at[pl.ds(0, TM_E), :], xbuf.at[slot], sem.at[slot]).wait()
    x = xbuf[slot].astype(BF16)
    g = jnp.dot(x, wg_bf[...], preferred_element_type=F32)
    u = jnp.dot(x, wu_bf[...], preferred_element_type=F32)
    a = (_silu(g) * u).astype(BF16)
    y = jnp.dot(a, wd_bf[...], preferred_element_type=F32)
    o_ref[...] = y * gate_ref[...]


def _experts(tile_expert, src, h2_all, gate_sorted, w_gate, w_up, w_down):
    n_rows = src.shape[0]
    nt = n_rows // TM_E
    d, de = w_gate.shape[1], w_gate.shape[2]
    grid_spec = pltpu.PrefetchScalarGridSpec(
        num_scalar_prefetch=2,
        grid=(nt,),
        in_specs=[
            pl.BlockSpec(memory_space=pl.ANY),
            pl.BlockSpec((TM_E, 1), lambda i, te, sr: (i, 0)),
            pl.BlockSpec((1, d, de), lambda i, te, sr: (te[i], 0, 0)),
            pl.BlockSpec((1, d, de), lambda i, te, sr: (te[i], 0, 0)),
            pl.BlockSpec((1, de, d), lambda i, te, sr: (te[i], 0, 0)),
        ],
        out_specs=pl.BlockSpec((TM_E, d), lambda i, te, sr: (i, 0)),
        scratch_shapes=[
            pltpu.VMEM((2, TM_E, d), F32),
            pltpu.SemaphoreType.DMA((2,)),
            pltpu.VMEM((d, de), BF16), pltpu.VMEM((d, de), BF16), pltpu.VMEM((de, d), BF16),
        ],
    )
    return pl.pallas_call(
        _experts_kernel,
        out_shape=jax.ShapeDtypeStruct((n_rows, d), F32),
        grid_spec=grid_spec,
        compiler_params=_cparams(("arbitrary",)),
        name="experts",
    )(tile_expert, src, h2_all, gate_sorted, w_gate, w_up, w_down)


def _combine_kernel(pos_ref, o_hbm, x1_ref, g2_ref, x2_ref, buf, sem):
    i = pl.program_id(0)
    nt = pl.num_programs(0)
    slot = i % 2

    def issue(tile, s):
        def body(r, carry):
            t = tile * TM_C + r
            for k in range(TOP_K):
                pltpu.make_async_copy(o_hbm.at[pl.ds(pos_ref[TOP_K * t + k], 1), :],
                                      buf.at[s, k, pl.ds(r, 1), :], sem.at[s]).start()
            return carry
        lax.fori_loop(0, TM_C, body, 0)

    @pl.when(i == 0)
    def _():
        issue(0, 0)

    @pl.when(i + 1 < nt)
    def _():
        issue(i + 1, 1 - slot)

    for k in range(TOP_K):
        pltpu.make_async_copy(o_hbm.at[pl.ds(0, TM_C), :], buf.at[slot, k], sem.at[slot]).wait()
    x2_ref[...] = x1_ref[...] + g2_ref[0] * (buf[slot, 0] + buf[slot, 1])


def _combine(pos, o_sorted, x1, mod48, mod_row):
    m, d = x1.shape
    grid_spec = pltpu.PrefetchScalarGridSpec(
        num_scalar_prefetch=1,
        grid=(m // TM_C,),
        in_specs=[
            pl.BlockSpec(memory_space=pl.ANY),
            pl.BlockSpec((TM_C, d), lambda i, p: (i, 0)),
            pl.BlockSpec((1, 1, d), lambda i, p: (mod_row(i) * 6 + 5, 0, 0)),
        ],
        out_specs=pl.BlockSpec((TM_C, d), lambda i, p: (i, 0)),
        scratch_shapes=[pltpu.VMEM((2, TOP_K, TM_C, d), F32), pltpu.SemaphoreType.DMA((2,))],
    )
    return pl.pallas_call(
        _combine_kernel,
        out_shape=jax.ShapeDtypeStruct((m, d), F32),
        grid_spec=grid_spec,
        compiler_params=_cparams(("arbitrary",)),
        name="combine",
    )(pos, o_sorted, x1, mod48)


def _rope_tables(n_lat, tm):
    t = jnp.arange(n_lat)
    r = (t // GRID_W).astype(F32)
    col = (t % GRID_W).astype(F32)

    def cos_sin(dim):
        nf = dim // 4
        inv = ROPE_THETA ** (-jnp.arange(nf, dtype=F32) / nf)
        ang = jnp.concatenate([r[:, None] * inv, col[:, None] * inv], axis=-1)
        return jnp.cos(ang), jnp.sin(ang)

    c64, s64 = cos_sin(A_DK)
    c128, s128 = cos_sin(B_DH)
    z32 = jnp.zeros_like(s64)
    tabs = [
        jnp.concatenate([c64, c64, c64, c64], axis=-1),
        jnp.concatenate([-s64, z32, -s64, z32], axis=-1),
        jnp.concatenate([z32, s64, z32, s64], axis=-1),
        jnp.concatenate([c128, c128], axis=-1),
        jnp.concatenate([-s128, s128], axis=-1),
    ]
    ident = [jnp.ones((tm, LANES), F32), jnp.zeros((tm, LANES), F32), jnp.zeros((tm, LANES), F32),
             jnp.ones((tm, LANES), F32), jnp.zeros((tm, LANES), F32)]
    return [jnp.concatenate([a, b], axis=0) for a, b in zip(tabs, ident)]


def _layer_params(l, w_in, w_out, a_qn, a_kn, b_qn, b_kn, c_qa_norm, c_kva_norm, c_wuq, c_wukv, c_qn, c_kn):
    w_in_bf = jnp.pad(w_in[l], ((0, 0), (0, D_IN_PAD - D_IN))).astype(BF16)
    z64 = jnp.zeros((C_ROPE,), F32)
    gains = jnp.stack([
        jnp.tile(a_qn[l], 2) * (A_DK ** -0.5),
        jnp.tile(a_kn[l], 2),
        b_qn[l] * (B_DH ** -0.5),
        b_kn[l],
        c_qn[l][:C_NOPE] * (C_DQK ** -0.5),
        jnp.concatenate([c_qn[l][C_NOPE:] * (C_DQK ** -0.5), z64]),
        c_kn[l][:C_NOPE],
        jnp.concatenate([c_kn[l][C_NOPE:], z64]),
    ])
    wq = c_wuq[l].reshape(C_Q_RANK, C_HEADS, C_DQK)
    wq = jnp.pad(wq, ((0, 0), (0, 0), (0, C_HEAD_PAD - C_DQK))).reshape(C_Q_RANK, C_HEADS * C_HEAD_PAD)
    return dict(w_in=w_in_bf, gains=gains, gcq=c_qa_norm[l][None], gckv=c_kva_norm[l][None],
                wuq=wq.astype(BF16), wukv=c_wukv[l].astype(BF16), w_out=w_out[l].astype(BF16))


def _sorted_rows(idx, wts):
    t = idx.shape[0]
    n_pairs = t * TOP_K
    n_rows = ((n_pairs + N_EXPERTS * (TM_E - 1)) // TM_E) * TM_E
    nt = n_rows // TM_E
    flat_e = idx.reshape(-1)
    onehot = (flat_e[:, None] == jnp.arange(N_EXPERTS)[None, :]).astype(jnp.int32)
    csum = jnp.cumsum(onehot, axis=0)
    rank = jnp.take_along_axis(csum, flat_e[:, None], axis=1)[:, 0] - 1
    counts = csum[-1]
    padded = ((counts + TM_E - 1) // TM_E) * TM_E
    ends = jnp.cumsum(padded)
    pos = (ends - padded)[flat_e] + rank
    src = jnp.zeros((n_rows,), jnp.int32).at[pos].set(jnp.arange(n_pairs, dtype=jnp.int32) // TOP_K)
    gate = jnp.zeros((n_rows,), F32).at[pos].set(wts.reshape(-1))
    tile_start = jnp.arange(nt, dtype=jnp.int32) * TM_E
    tile_expert = jnp.sum((ends[None, :] <= tile_start[:, None]).astype(jnp.int32), axis=1)
    last_used = jnp.sum((ends <= ends[-1] - 1).astype(jnp.int32))
    tile_expert = jnp.minimum(tile_expert, last_used)
    return pos.astype(jnp.int32), src, gate[:, None], tile_expert


def kernel(x, c, ctx, c_ctx, ada_w, ada_b, norm1_g, norm2_g, w_in, w_out, a_qn, a_kn, a_lambda, a_subln,
           b_qn, b_kn, b_sink, c_qa_norm, c_kva_norm, c_wuq, c_wukv, c_qn, c_kn,
           router_w, router_bias, moe_w_gate, moe_w_up, moe_w_down):
    bsz, n_lat, d = x.shape
    n_ctx = ctx.shape[1]
    depth = ada_w.shape[0]
    t_lat, t_ctx = bsz * n_lat, bsz * n_ctx
    tm = TM_PREP
    lat_tiles = n_lat // tm
    n_lat_tiles, n_ctx_tiles = t_lat // tm, t_ctx // tm
    n_all_tiles = n_lat_tiles + n_ctx_tiles

    cond8 = jnp.concatenate([c, c_ctx[None], jnp.zeros((8 - bsz - 1, d), F32)], axis=0)
    mod = _ada_modulation(cond8, ada_w, ada_b)
    tables = _rope_tables(n_lat, tm)
    rb = router_bias[None]

    mod_row = lambda i: jnp.minimum(i // lat_tiles, bsz)
    mod_row_c = lambda i: jnp.minimum(i // (n_lat // TM_C), bsz)
    rope_blk = lambda i: jnp.where(i < n_lat_tiles, i % lat_tiles, lat_tiles)

    xa, xb, n_a_tiles = x.reshape(t_lat, d), ctx.reshape(t_ctx, d), n_lat_tiles
    lat_src, ctx_src = (0, n_lat), (t_lat, n_ctx)
    for l in range(depth):
        last = l == depth - 1
        p = _layer_params(l, w_in, w_out, a_qn, a_kn, b_qn, b_kn, c_qa_norm, c_kva_norm,
                          c_wuq, c_wukv, c_qn, c_kn)
        mod48 = mod[l].reshape(8 * 6, 1, d)
        aq, ak, av, bq, bk, bv, cq, ck, cv = _prep(
            xa, xb, n_all_tiles, n_a_tiles, mod48, mod_row, norm1_g[l][None], p["w_in"], tables, rope_blk,
            p["gains"], p["gcq"], p["gckv"], p["wuq"], p["wukv"])

        lv, gsub, sink = a_lambda[l], a_subln[l][None], b_sink[l]
        o_a = _attn_a(lv, gsub, aq, ak, av, 0, n_lat, [lat_src, ctx_src], bsz, l)
        o_b = _attn_b(sink, bq, bk, bv, bsz, n_lat, n_ctx)
        o_c = _attn_c(cq, ck, cv, 0, n_lat, [lat_src, ctx_src], bsz)
        set_a = (o_a, o_b, o_c, xa)
        if not last:
            oc_a = _attn_a(lv, gsub, aq, ak, av, t_lat, n_ctx, [ctx_src], bsz, l)
            oc_b = _attn_b_ctx(sink, bq, bk, bv, bsz, n_lat, n_ctx)
            oc_c = _attn_c(cq, ck, cv, t_lat, n_ctx, [ctx_src], bsz)
            set_b, n_tok_tiles = (oc_a, oc_b, oc_c, xb), n_all_tiles
        else:
            set_b, n_tok_tiles = set_a, n_lat_tiles
        x1, h2, idx, wts = _outproj(set_a, set_b, n_tok_tiles, min(n_a_tiles, n_tok_tiles), mod48, mod_row,
                                    norm2_g[l][None], p["w_out"], router_w, rb)

        pos, src, gate, tile_expert = _sorted_rows(idx[:, :TOP_K], wts[:, :TOP_K])
        o_sorted = _experts(tile_expert, src, h2, gate, moe_w_gate[l], moe_w_up[l], moe_w_down[l])
        xa = _combine(pos, o_sorted, x1, mod48, mod_row_c)
        xb, n_a_tiles = xa, n_all_tiles
    return xa.reshape(bsz, n_lat, d)
```

```python
import functools
import math

import jax
import jax.numpy as jnp
from jax import lax
from jax.experimental import pallas as pl
from jax.experimental.pallas import tpu as pltpu

F32 = jnp.float32
BF16 = jnp.bfloat16

D_MODEL = 2048
GRID_W = 64
BLOCK = 128
WINDOW = 128
ROPE_THETA = 10000.0
EPS = 1e-6
NEG_INF = -1e30
A_HEADS, A_DK = 4, 64
A_DV = 2 * A_DK
B_HEADS, B_KV_HEADS, B_DH = 8, 2, 128
B_GROUP = B_HEADS // B_KV_HEADS
C_HEADS, C_Q_RANK, C_KV_RANK, C_NOPE, C_ROPE, C_DV = 4, 512, 256, 128, 64, 128
C_DQK = C_NOPE + C_ROPE
SPLIT_SIZES = (A_HEADS * 2 * A_DK, A_HEADS * 2 * A_DK, A_HEADS * A_DV,
               B_HEADS * B_DH, B_KV_HEADS * B_DH, B_KV_HEADS * B_DH,
               C_Q_RANK, C_KV_RANK, C_ROPE)
D_IN = sum(SPLIT_SIZES)
N_EXPERTS, N_GROUPS, TOP_K = 32, 4, 2
EXPERTS_PER_GROUP = N_EXPERTS // N_GROUPS
D_EXPERT = 512

LANES = 128
V7X_VMEM_LIMIT = 56 * 1024 * 1024

D_IN_PAD = ((D_IN + LANES - 1) // LANES) * LANES
C_HEAD_PAD = 2 * LANES
TM_PREP = 256
TQ_A = 256
TQ_C = 512
QB_B = 512
TM_E = 128
TM_C = 128
ADA_TN = 1024

_OFF = [0]
for _s in SPLIT_SIZES:
    _OFF.append(_OFF[-1] + _s)
O_AQ, O_AK, O_AV, O_BQ, O_BK, O_BV, O_CQ, O_CKV, O_CKR, _ = _OFF


def _cparams(sem):
    return pltpu.CompilerParams(dimension_semantics=sem, vmem_limit_bytes=V7X_VMEM_LIMIT)


def _silu(v):
    return v * (1.0 / (1.0 + jnp.exp(-v)))


def _ada_kernel(cond_ref, w_ref, b_ref, o_ref):
    s = _silu(cond_ref[...]).astype(BF16)
    o_ref[0] = jnp.dot(s, w_ref[0].astype(BF16), preferred_element_type=F32) + b_ref[0]


def _ada_modulation(cond8, ada_w, ada_b):
    depth, d, n = ada_w.shape
    return pl.pallas_call(
        _ada_kernel,
        out_shape=jax.ShapeDtypeStruct((depth, 8, n), F32),
        grid=(depth, n // ADA_TN),
        in_specs=[
            pl.BlockSpec((8, d), lambda l, j: (0, 0)),
            pl.BlockSpec((1, d, ADA_TN), lambda l, j: (l, 0, j)),
            pl.BlockSpec((1, 1, ADA_TN), lambda l, j: (l, 0, j)),
        ],
        out_specs=pl.BlockSpec((1, 8, ADA_TN), lambda l, j: (l, 0, j)),
        compiler_params=_cparams(("arbitrary", "arbitrary")),
        name="ada_modulation",
    )(cond8, ada_w, ada_b.reshape(depth, 1, n))


def _rope64(v, c, sa, sb):
    return v * c + pltpu.roll(v, 96, 1) * sa + pltpu.roll(v, 32, 1) * sb


def _rope128(v, c, s):
    return v * c + pltpu.roll(v, 64, 1) * s


def _norm_seg128(v, g):
    ms = jnp.sum(v * v, axis=-1, keepdims=True) * (1.0 / 128)
    return v * lax.rsqrt(ms + EPS) * g


def _norm_seg64x2(v, g, lo):
    sq = v * v
    s_lo = jnp.sum(jnp.where(lo, sq, 0.0), axis=-1, keepdims=True)
    s_hi = jnp.sum(jnp.where(lo, 0.0, sq), axis=-1, keepdims=True)
    ms = jnp.where(lo, s_lo, s_hi) * (1.0 / 64)
    return v * lax.rsqrt(ms + EPS) * g


def _norm_low64(v, g):
    ms = jnp.sum(v * v, axis=-1, keepdims=True) * (1.0 / 64)
    return v * lax.rsqrt(ms + EPS) * g


def _prep_kernel(xa_ref, xb_ref, sh_ref, sc_ref, g1_ref, w_ref, ca_ref, saa_ref, sab_ref, cb_ref, sb_ref,
                 gains_ref, gcq_ref, gckv_ref, wuq_ref, wukv_ref,
                 aq_ref, ak_ref, av_ref, bq_ref, bk_ref, bv_ref, cq_ref, ck_ref, cv_ref, *, n_a_tiles):
    x = jnp.where(pl.program_id(0) < n_a_tiles, xa_ref[...], xb_ref[...])
    ms = jnp.mean(x * x, axis=-1, keepdims=True)
    h = x * lax.rsqrt(ms + EPS) * g1_ref[...]
    h = h * (1.0 + sc_ref[0]) + sh_ref[0]
    z = jnp.dot(h.astype(BF16), w_ref[...], preferred_element_type=F32)

    lane = lax.broadcasted_iota(jnp.int32, (1, LANES), 1)
    lo = lane < 64
    ca, saa, sab = ca_ref[...], saa_ref[...], sab_ref[...]
    cb, sb = cb_ref[...], sb_ref[...]
    g_aq, g_ak, g_bq, g_bk = gains_ref[0:1], gains_ref[1:2], gains_ref[2:3], gains_ref[3:4]
    g_cqn, g_cqr, g_ckn, g_ckr = gains_ref[4:5], gains_ref[5:6], gains_ref[6:7], gains_ref[7:8]

    def blk(off, j):
        return z[:, off + j * LANES: off + (j + 1) * LANES]

    for j in range(A_HEADS):
        sl = slice(j * LANES, (j + 1) * LANES)
        aq_ref[:, sl] = _rope64(_norm_seg64x2(blk(O_AQ, j), g_aq, lo), ca, saa, sab).astype(BF16)
        ak_ref[:, sl] = _rope64(_norm_seg64x2(blk(O_AK, j), g_ak, lo), ca, saa, sab).astype(BF16)
        av_ref[:, sl] = blk(O_AV, j).astype(BF16)
    for j in range(B_HEADS):
        sl = slice(j * LANES, (j + 1) * LANES)
        bq_ref[:, sl] = _rope128(_norm_seg128(blk(O_BQ, j), g_bq), cb, sb).astype(BF16)
    for j in range(B_KV_HEADS):
        sl = slice(j * LANES, (j + 1) * LANES)
        bk_ref[:, sl] = _rope128(_norm_seg128(blk(O_BK, j), g_bk), cb, sb).astype(BF16)
        bv_ref[:, sl] = blk(O_BV, j).astype(BF16)
    cq = z[:, O_CQ:O_CQ + C_Q_RANK]
    cqn = cq * lax.rsqrt(jnp.mean(cq * cq, axis=-1, keepdims=True) + EPS) * gcq_ref[...]
    q = jnp.dot(cqn.astype(BF16), wuq_ref[...], preferred_element_type=F32)
    ckv = z[:, O_CKV:O_CKV + C_KV_RANK]
    ckvn = ckv * lax.rsqrt(jnp.mean(ckv * ckv, axis=-1, keepdims=True) + EPS) * gckv_ref[...]
    kv = jnp.dot(ckvn.astype(BF16), wukv_ref[...], preferred_element_type=F32)
    krope = _rope64(_norm_low64(z[:, O_CKR:O_CKR + LANES], g_ckr), ca, saa, sab).astype(BF16)
    for hh in range(C_HEADS):
        b0 = hh * C_HEAD_PAD
        cq_ref[:, b0:b0 + LANES] = _norm_seg128(q[:, b0:b0 + LANES], g_cqn).astype(BF16)
        cq_ref[:, b0 + LANES:b0 + 2 * LANES] = _rope64(
            _norm_low64(q[:, b0 + LANES:b0 + 2 * LANES], g_cqr), ca, saa, sab).astype(BF16)
        ck_ref[:, b0:b0 + LANES] = _norm_seg128(kv[:, b0:b0 + LANES], g_ckn).astype(BF16)
        ck_ref[:, b0 + LANES:b0 + 2 * LANES] = krope
        cv_ref[:, hh * LANES:(hh + 1) * LANES] = kv[:, b0 + LANES:b0 + 2 * LANES].astype(BF16)


def _split_rows(n_a_tiles):
    first = lambda i: (jnp.minimum(i, n_a_tiles - 1), 0)
    second = lambda i: (jnp.maximum(i - n_a_tiles, 0), 0)
    return first, second


def _prep(xa, xb, n_tiles, n_a_tiles, mod48, mod_row, g1, w_in_bf, tables, rope_blk, gains, gcq, gckv,
          wuq_bf, wukv_bf):
    d = xa.shape[1]
    tm = TM_PREP
    m = n_tiles * tm
    row = lambda i: (i, 0)
    const = lambda i: (0, 0)
    first, second = _split_rows(n_a_tiles)
    tab_spec = pl.BlockSpec((tm, LANES), lambda i: (rope_blk(i), 0))
    widths = (512, 512, 512, 1024, 256, 256, C_HEADS * C_HEAD_PAD, C_HEADS * C_HEAD_PAD, 512)
    return pl.pallas_call(
        functools.partial(_prep_kernel, n_a_tiles=n_a_tiles),
        out_shape=[jax.ShapeDtypeStruct((m, w), BF16) for w in widths],
        grid=(n_tiles,),
        in_specs=[
            pl.BlockSpec((tm, d), first),
            pl.BlockSpec((tm, d), second),
            pl.BlockSpec((1, 1, d), lambda i: (mod_row(i) * 6 + 0, 0, 0)),
            pl.BlockSpec((1, 1, d), lambda i: (mod_row(i) * 6 + 1, 0, 0)),
            pl.BlockSpec((1, d), const),
            pl.BlockSpec((d, D_IN_PAD), const, pipeline_mode=pl.Buffered(1)),
            tab_spec, tab_spec, tab_spec, tab_spec, tab_spec,
            pl.BlockSpec((8, LANES), const),
            pl.BlockSpec((1, C_Q_RANK), const),
            pl.BlockSpec((1, C_KV_RANK), const),
            pl.BlockSpec((C_Q_RANK, C_HEADS * C_HEAD_PAD), const, pipeline_mode=pl.Buffered(1)),
            pl.BlockSpec((C_KV_RANK, C_HEADS * C_HEAD_PAD), const, pipeline_mode=pl.Buffered(1)),
        ],
        out_specs=[pl.BlockSpec((tm, w), row) for w in widths],
        compiler_params=_cparams(("arbitrary",)),
        name="prep",
    )(xa, xb, mod48, mod48, g1, w_in_bf, *tables, gains, gcq, gckv, wuq_bf, wukv_bf)


def _dot_nt(a, b):
    return lax.dot_general(a, b, (((1,), (1,)), ((), ())), preferred_element_type=F32)


def _attn_a_kernel(*refs, n_src, lam_init):
    lv_ref, gsub_ref, q_ref = refs[0], refs[1], refs[2]
    k_refs = refs[3:3 + n_src]
    v_refs = refs[3 + n_src:3 + 2 * n_src]
    o_ref = refs[3 + 2 * n_src]
    tq = q_ref.shape[0]
    lv = lv_ref[...]
    lam = (jnp.exp(jnp.sum(lv[0:1] * lv[1:2], axis=-1, keepdims=True))
           - jnp.exp(jnp.sum(lv[2:3] * lv[3:4], axis=-1, keepdims=True)) + lam_init)
    q = q_ref[...]
    lo = lax.broadcasted_iota(jnp.int32, (1, LANES), 1) < 64
    zero = jnp.zeros_like(q)
    qq = jnp.concatenate([jnp.where(lo, q, zero), jnp.where(lo, zero, q)], axis=0)
    s = [_dot_nt(qq, k[...]) for k in k_refs]
    m = functools.reduce(jnp.maximum, [jnp.max(si, axis=-1, keepdims=True) for si in s])
    e = [jnp.exp(si - m) for si in s]
    l = functools.reduce(jnp.add, [jnp.sum(ei, axis=-1, keepdims=True) for ei in e])
    o2 = functools.reduce(jnp.add, [jnp.dot(ei.astype(BF16), v[...], preferred_element_type=F32)
                                    for ei, v in zip(e, v_refs)])
    o2 = o2 * (1.0 / l)
    o = o2[:tq] - lam * o2[tq:]
    ms = jnp.mean(o * o, axis=-1, keepdims=True)
    o_ref[...] = (o * lax.rsqrt(ms + EPS) * gsub_ref[...] * (1.0 - lam_init)).astype(BF16)


def _attn_a(lv, gsub, aq, ak, av, q_row0, q_rows, srcs, n_batch, layer_idx):
    tq = min(TQ_A, q_rows)
    nq = q_rows // tq
    qb0 = q_row0 // tq
    lam_init = 0.8 - 0.6 * math.exp(-0.3 * layer_idx)
    in_specs = [
        pl.BlockSpec((4, A_DK), lambda b, h, i: (0, 0)),
        pl.BlockSpec((1, A_DV), lambda b, h, i: (0, 0)),
        pl.BlockSpec((tq, LANES), lambda b, h, i: (qb0 + b * nq + i, h)),
    ]
    kv_specs = [pl.BlockSpec((rows, LANES), lambda b, h, i, blk0=row0 // rows: (blk0 + b, h))
                for row0, rows in srcs]
    return pl.pallas_call(
        functools.partial(_attn_a_kernel, n_src=len(srcs), lam_init=lam_init),
        out_shape=jax.ShapeDtypeStruct((n_batch * q_rows, A_HEADS * A_DV), BF16),
        grid=(n_batch, A_HEADS, nq),
        in_specs=in_specs + kv_specs + kv_specs,
        out_specs=pl.BlockSpec((tq, LANES), lambda b, h, i: (b * nq + i, h)),
        compiler_params=_cparams(("arbitrary", "arbitrary", "arbitrary")),
        name="attn_a",
    )(lv, gsub, aq, *([ak] * len(srcs)), *([av] * len(srcs)))


def _attn_c_kernel(*refs, n_src):
    q_ref = refs[0]
    k_refs = refs[1:1 + n_src]
    v_refs = refs[1 + n_src:1 + 2 * n_src]
    o_ref = refs[1 + 2 * n_src]
    q = q_ref[...]
    s = [_dot_nt(q, k[...]) for k in k_refs]
    m = functools.reduce(jnp.maximum, [jnp.max(si, axis=-1, keepdims=True) for si in s])
    e = [jnp.exp(si - m) for si in s]
    l = functools.reduce(jnp.add, [jnp.sum(ei, axis=-1, keepdims=True) for ei in e])
    o = functools.reduce(jnp.add, [jnp.dot(ei.astype(BF16), v[...], preferred_element_type=F32)
                                   for ei, v in zip(e, v_refs)])
    o_ref[...] = (o * (1.0 / l)).astype(BF16)


def _attn_c(cq, ck, cv, q_row0, q_rows, srcs, n_batch):
    tq = min(TQ_C, q_rows)
    nq = q_rows // tq
    qb0 = q_row0 // tq
    in_specs = [pl.BlockSpec((tq, C_HEAD_PAD), lambda b, h, i: (qb0 + b * nq + i, h))]
    k_specs = [pl.BlockSpec((rows, C_HEAD_PAD), lambda b, h, i, blk0=row0 // rows: (blk0 + b, h))
               for row0, rows in srcs]
    v_specs = [pl.BlockSpec((rows, C_DV), lambda b, h, i, blk0=row0 // rows: (blk0 + b, h))
               for row0, rows in srcs]
    return pl.pallas_call(
        functools.partial(_attn_c_kernel, n_src=len(srcs)),
        out_shape=jax.ShapeDtypeStruct((n_batch * q_rows, C_HEADS * C_DV), BF16),
        grid=(n_batch, C_HEADS, nq),
        in_specs=in_specs + k_specs + v_specs,
        out_specs=pl.BlockSpec((tq, C_DV), lambda b, h, i: (b * nq + i, h)),
        compiler_params=_cparams(("arbitrary", "arbitrary", "arbitrary")),
        name="attn_c",
    )(cq, *([ck] * len(srcs)), *([cv] * len(srcs)))


def _stack_heads(q):
    return jnp.concatenate([q[:, g * LANES:(g + 1) * LANES] for g in range(B_GROUP)], axis=0)


def _sink_rows(sink_ref, kvh, rows):
    return jnp.concatenate(
        [jnp.full((rows, 1), sink_ref[kvh * B_GROUP + g], F32) for g in range(B_GROUP)], axis=0)


def _attn_b_kernel(sink_ref, q_ref, kp_ref, km_ref, kn_ref, vp_ref, vm_ref, vn_ref, kc_ref, vc_ref, o_ref):
    kvh = pl.program_id(1)
    qb = pl.program_id(2)
    nqb = pl.num_programs(2)
    n_blk = QB_B // BLOCK
    kband = jnp.concatenate([kp_ref[...], km_ref[...], kn_ref[...]], axis=0)
    vband = jnp.concatenate([vp_ref[...], vm_ref[...], vn_ref[...]], axis=0)
    kc, vc = kc_ref[...], vc_ref[...]
    sink = _sink_rows(sink_ref, kvh, BLOCK)
    r = lax.broadcasted_iota(jnp.int32, (B_GROUP * BLOCK, 3 * BLOCK), 0) % BLOCK
    c = lax.broadcasted_iota(jnp.int32, (B_GROUP * BLOCK, 3 * BLOCK), 1)
    cr = c - r
    band_ok = (cr >= 0) & (cr <= BLOCK + WINDOW)
    for j in range(n_blk):
        q4 = _stack_heads(q_ref[j * BLOCK:(j + 1) * BLOCK, :])
        s_loc = _dot_nt(q4, kband[j * BLOCK:(j + 3) * BLOCK])
        valid = band_ok
        if j == 0:
            valid = valid & (c >= jnp.where(qb > 0, 0, BLOCK))
        if j == n_blk - 1:
            valid = valid & (c < jnp.where(qb < nqb - 1, 3 * BLOCK, 2 * BLOCK))
        s_loc = jnp.where(valid, s_loc, NEG_INF)
        s_ctx = _dot_nt(q4, kc)
        m = jnp.maximum(jnp.maximum(jnp.max(s_loc, axis=-1, keepdims=True),
                                    jnp.max(s_ctx, axis=-1, keepdims=True)), sink)
        e_loc, e_ctx = jnp.exp(s_loc - m), jnp.exp(s_ctx - m)
        l = (jnp.sum(e_loc, axis=-1, keepdims=True) + jnp.sum(e_ctx, axis=-1, keepdims=True)
             + jnp.exp(sink - m))
        o = (jnp.dot(e_loc.astype(BF16), vband[j * BLOCK:(j + 3) * BLOCK], preferred_element_type=F32)
             + jnp.dot(e_ctx.astype(BF16), vc, preferred_element_type=F32)) * (1.0 / l)
        for g in range(B_GROUP):
            o_ref[j * BLOCK:(j + 1) * BLOCK, g * LANES:(g + 1) * LANES] = (
                o[g * BLOCK:(g + 1) * BLOCK].astype(BF16))


def _attn_b(sink, bq, bk, bv, n_batch, n_lat, n_ctx):
    nqb = n_lat // QB_B
    per = QB_B // BLOCK
    blocks_per_batch = n_lat // BLOCK
    ctx_blk0 = n_batch * n_lat // n_ctx
    gw = B_GROUP * B_DH
    prev = lambda b, h, i: (b * blocks_per_batch + jnp.maximum(i * per - 1, 0), h)
    main = lambda b, h, i: (b * nqb + i, h)
    nxt = lambda b, h, i: (b * blocks_per_batch + jnp.minimum(i * per + per, blocks_per_batch - 1), h)
    ctx = lambda b, h, i: (ctx_blk0 + b, h)
    return pl.pallas_call(
        _attn_b_kernel,
        out_shape=jax.ShapeDtypeStruct((n_batch * n_lat, B_HEADS * B_DH), BF16),
        grid=(n_batch, B_KV_HEADS, nqb),
        in_specs=[
            pl.BlockSpec(memory_space=pltpu.SMEM),
            pl.BlockSpec((QB_B, gw), main),
            pl.BlockSpec((BLOCK, B_DH), prev), pl.BlockSpec((QB_B, B_DH), main), pl.BlockSpec((BLOCK, B_DH), nxt),
            pl.BlockSpec((BLOCK, B_DH), prev), pl.BlockSpec((QB_B, B_DH), main), pl.BlockSpec((BLOCK, B_DH), nxt),
            pl.BlockSpec((n_ctx, B_DH), ctx), pl.BlockSpec((n_ctx, B_DH), ctx),
        ],
        out_specs=pl.BlockSpec((QB_B, gw), main),
        compiler_params=_cparams(("arbitrary", "arbitrary", "arbitrary")),
        name="attn_b",
    )(sink, bq, bk, bk, bk, bv, bv, bv, bk, bv)


def _attn_b_ctx_kernel(sink_ref, q_ref, k_ref, v_ref, o_ref):
    kvh = pl.program_id(1)
    rows = q_ref.shape[0]
    q4 = _stack_heads(q_ref[...])
    sink = _sink_rows(sink_ref, kvh, rows)
    s = _dot_nt(q4, k_ref[...])
    m = jnp.maximum(jnp.max(s, axis=-1, keepdims=True), sink)
    e = jnp.exp(s - m)
    l = jnp.sum(e, axis=-1, keepdims=True) + jnp.exp(sink - m)
    o = jnp.dot(e.astype(BF16), v_ref[...], preferred_element_type=F32) * (1.0 / l)
    for g in range(B_GROUP):
        o_ref[:, g * LANES:(g + 1) * LANES] = o[g * rows:(g + 1) * rows].astype(BF16)


def _attn_b_ctx(sink, bq, bk, bv, n_batch, n_lat, n_ctx):
    gw = B_GROUP * B_DH
    ctx_blk0 = n_batch * n_lat // n_ctx
    return pl.pallas_call(
        _attn_b_ctx_kernel,
        out_shape=jax.ShapeDtypeStruct((n_batch * n_ctx, B_HEADS * B_DH), BF16),
        grid=(n_batch, B_KV_HEADS),
        in_specs=[
            pl.BlockSpec(memory_space=pltpu.SMEM),
            pl.BlockSpec((n_ctx, gw), lambda b, h: (ctx_blk0 + b, h)),
            pl.BlockSpec((n_ctx, B_DH), lambda b, h: (ctx_blk0 + b, h)),
            pl.BlockSpec((n_ctx, B_DH), lambda b, h: (ctx_blk0 + b, h)),
        ],
        out_specs=pl.BlockSpec((n_ctx, gw), lambda b, h: (b, h)),
        compiler_params=_cparams(("arbitrary", "arbitrary")),
        name="attn_b_ctx",
    )(sink, bq, bk, bv)


def _route(h2, rw_ref, rb_ref):
    logits = jnp.dot(h2, rw_ref[...], preferred_element_type=F32, precision=lax.Precision.HIGHEST)
    scores = 1.0 / (1.0 + jnp.exp(-logits))
    sel = scores + rb_ref[...]
    lane_i = lax.broadcasted_iota(jnp.int32, sel.shape, 1)
    lane = lane_i.astype(F32)
    big = float(N_EXPERTS)

    def top2(mask):
        v = jnp.where(mask, sel, -jnp.inf)
        m1 = jnp.max(v, axis=-1, keepdims=True)
        i1 = jnp.min(jnp.where(v == m1, lane, big), axis=-1, keepdims=True)
        v2 = jnp.where(lane == i1, -jnp.inf, v)
        m2 = jnp.max(v2, axis=-1, keepdims=True)
        i2 = jnp.min(jnp.where(v2 == m2, lane, big), axis=-1, keepdims=True)
        return m1, i1, m2, i2

    best = None
    for g in range(N_GROUPS):
        m1, i1, m2, i2 = top2((lane_i >= g * EXPERTS_PER_GROUP) & (lane_i < (g + 1) * EXPERTS_PER_GROUP))
        gs = m1 + m2
        if best is None:
            best = (gs, i1, i2)
        else:
            take = gs > best[0]
            best = (jnp.where(take, gs, best[0]), jnp.where(take, i1, best[1]), jnp.where(take, i2, best[2]))
    _, e1, e2 = best
    w1 = jnp.sum(jnp.where(lane == e1, scores, 0.0), axis=-1, keepdims=True)
    w2 = jnp.sum(jnp.where(lane == e2, scores, 0.0), axis=-1, keepdims=True)
    tot = w1 + w2
    return e1, e2, w1 / tot, w2 / tot


def _outproj_kernel(oa1_ref, ob1_ref, oc1_ref, x1in_ref, oa2_ref, ob2_ref, oc2_ref, x2in_ref,
                    g1_ref, sh2_ref, sc2_ref, n2_ref, w_ref, rw_ref, rb_ref,
                    x1_ref, h2_ref, idx_ref, wt_ref, *, n_a_tiles):
    first = pl.program_id(0) < n_a_tiles
    oa = jnp.where(first, oa1_ref[...], oa2_ref[...])
    ob = jnp.where(first, ob1_ref[...], ob2_ref[...])
    oc = jnp.where(first, oc1_ref[...], oc2_ref[...])
    x = jnp.where(first, x1in_ref[...], x2in_ref[...])
    na, nb = oa.shape[1], ob.shape[1]
    y = (jnp.dot(oa, w_ref[0:na, :], preferred_element_type=F32)
         + jnp.dot(ob, w_ref[na:na + nb, :], preferred_element_type=F32)
         + jnp.dot(oc, w_ref[na + nb:, :], preferred_element_type=F32))
    x1 = x + g1_ref[0] * y
    x1_ref[...] = x1
    ms = jnp.mean(x1 * x1, axis=-1, keepdims=True)
    h2 = x1 * lax.rsqrt(ms + EPS) * n2_ref[...]
    h2 = h2 * (1.0 + sc2_ref[0]) + sh2_ref[0]
    h2_ref[...] = h2
    e1, e2, w1, w2 = _route(h2, rw_ref, rb_ref)
    lane = lax.broadcasted_iota(jnp.int32, idx_ref.shape, 1)
    idx_ref[...] = jnp.where(lane == 0, e1, jnp.where(lane == 1, e2, 0.0)).astype(jnp.int32)
    wt_ref[...] = jnp.where(lane == 0, w1, jnp.where(lane == 1, w2, 0.0))


def _outproj(set_a, set_b, n_tiles, n_a_tiles, mod48, mod_row, n2, w_out_bf, router_w, router_b):
    d = set_a[3].shape[1]
    tm = TM_PREP
    m = n_tiles * tm
    row = lambda i: (i, 0)
    const = lambda i: (0, 0)
    first, second = _split_rows(n_a_tiles)
    mod_spec = lambda j: pl.BlockSpec((1, 1, d), lambda i: (mod_row(i) * 6 + j, 0, 0))
    in_specs = (
        [pl.BlockSpec((tm, a.shape[1]), first) for a in set_a]
        + [pl.BlockSpec((tm, a.shape[1]), second) for a in set_b]
        + [mod_spec(2), mod_spec(3), mod_spec(4),
           pl.BlockSpec((1, d), const),
           pl.BlockSpec((d, d), const, pipeline_mode=pl.Buffered(1)),
           pl.BlockSpec((d, N_EXPERTS), const),
           pl.BlockSpec((1, N_EXPERTS), const)])
    return pl.pallas_call(
        functools.partial(_outproj_kernel, n_a_tiles=n_a_tiles),
        out_shape=[jax.ShapeDtypeStruct((m, d), F32), jax.ShapeDtypeStruct((m, d), F32),
                   jax.ShapeDtypeStruct((m, LANES), jnp.int32), jax.ShapeDtypeStruct((m, LANES), F32)],
        grid=(n_tiles,),
        in_specs=in_specs,
        out_specs=[pl.BlockSpec((tm, d), row), pl.BlockSpec((tm, d), row),
                   pl.BlockSpec((tm, LANES), row), pl.BlockSpec((tm, LANES), row)],
        compiler_params=_cparams(("arbitrary",)),
        name="outproj",
    )(*set_a, *set_b, mod48, mod48, mod48, n2, w_out_bf, router_w, router_b)


def _experts_kernel(te_ref, nv_ref, src_ref, h2_hbm, wg_ref, wu_ref, wd_ref, o_ref,
                    xbuf, sem, wg_bf, wu_bf, wd_bf):
    i = pl.program_id(0)
    n_valid = nv_ref[0]
    slot = i % 2

    def issue(tile, s):
        base = tile * TM_E
        for r in range(TM_E):
            pltpu.make_async_copy(h2_hbm.at[pl.ds(src_ref[base + r], 1), :],
                                  xbuf.at[s, pl.ds(r, 1), :], sem.at[s]).start()

    @pl.when(i == 0)
    def _():
        issue(0, 0)

    changed = jnp.logical_or(i == 0, te_ref[i] != te_ref[jnp.maximum(i - 1, 0)])

    @pl.when(jnp.logical_and(changed, i < n_valid))
    def _():
        wg_bf[...] = wg_ref[0, 0].astype(BF16)
        wu_bf[...] = wu_ref[0, 0].astype(BF16)
        wd_bf[...] = wd_ref[0, 0].astype(BF16)

    def wait_slot(s):
        pltpu.make_async_copy(h2_hbm.at[pl.ds(0, TM_E), :], xbuf.at[s], sem.at[s]).wait()

    @pl.when(i < n_valid)
    def _():
        wait_slot(slot)
        x = xbuf[slot].astype(BF16)
        issue(i + 1, 1 - slot)
        g = jnp.dot(x, wg_bf[...], preferred_element_type=F32)
        u = jnp.dot(x, wu_bf[...], preferred_element_type=F32)
        a = (_silu(g) * u).astype(BF16)
        o_ref[...] = jnp.dot(a, wd_bf[...], preferred_element_type=F32)

    @pl.when(i == n_valid - 1)
    def _():
        wait_slot(1 - slot)

    @pl.when(i >= n_valid)
    def _():
        o_ref[...] = jnp.zeros_like(o_ref)


def _experts(layer, tile_expert, n_valid, src, h2_all, w_gate, w_up, w_down):
    n_rows = src.shape[0] - TM_E
    nt = n_rows // TM_E
    d, de = w_gate.shape[2], w_gate.shape[3]
    w_map = lambda i, te, nv, sr: (layer, te[i], 0, 0)
    grid_spec = pltpu.PrefetchScalarGridSpec(
        num_scalar_prefetch=3,
        grid=(nt,),
        in_specs=[
            pl.BlockSpec(memory_space=pl.ANY),
            pl.BlockSpec((1, 1, d, de), w_map),
            pl.BlockSpec((1, 1, d, de), w_map),
            pl.BlockSpec((1, 1, de, d), w_map),
        ],
        out_specs=pl.BlockSpec((TM_E, d), lambda i, te, nv, sr: (i, 0)),
        scratch_shapes=[
            pltpu.VMEM((2, TM_E, d), F32),
            pltpu.SemaphoreType.DMA((2,)),
            pltpu.VMEM((d, de), BF16), pltpu.VMEM((d, de), BF16), pltpu.VMEM((de, d), BF16),
        ],
    )
    return pl.pallas_call(
        _experts_kernel,
        out_shape=jax.ShapeDtypeStruct((n_rows, d), F32),
        grid_spec=grid_spec,
        compiler_params=_cparams(("arbitrary",)),
        name="experts",
    )(tile_expert, n_valid, src, h2_all, w_gate, w_up, w_down)


def _combine_kernel(pos_ref, o_hbm, x1_ref, wt_ref, g2_ref, x2_ref, buf, sem):
    i = pl.program_id(0)
    nt = pl.num_programs(0)
    slot = i % 2

    def issue(tile, s):
        base = tile * (TM_C * TOP_K)
        for r in range(TM_C):
            for k in range(TOP_K):
                pltpu.make_async_copy(o_hbm.at[pl.ds(pos_ref[base + TOP_K * r + k], 1), :],
                                      buf.at[s, k, pl.ds(r, 1), :], sem.at[s]).start()

    @pl.when(i == 0)
    def _():
        issue(0, 0)

    @pl.when(i + 1 < nt)
    def _():
        issue(i + 1, 1 - slot)

    for k in range(TOP_K):
        pltpu.make_async_copy(o_hbm.at[pl.ds(0, TM_C), :], buf.at[slot, k], sem.at[slot]).wait()
    wt = wt_ref[...]
    y = wt[:, 0:1] * buf[slot, 0] + wt[:, 1:2] * buf[slot, 1]
    x2_ref[...] = x1_ref[...] + g2_ref[0] * y


def _combine(pos, o_sorted, x1, wts, mod48, mod_row):
    m, d = x1.shape
    grid_spec = pltpu.PrefetchScalarGridSpec(
        num_scalar_prefetch=1,
        grid=(m // TM_C,),
        in_specs=[
            pl.BlockSpec(memory_space=pl.ANY),
            pl.BlockSpec((TM_C, d), lambda i, p: (i, 0)),
            pl.BlockSpec((TM_C, LANES), lambda i, p: (i, 0)),
            pl.BlockSpec((1, 1, d), lambda i, p: (mod_row(i) * 6 + 5, 0, 0)),
        ],
        out_specs=pl.BlockSpec((TM_C, d), lambda i, p: (i, 0)),
        scratch_shapes=[pltpu.VMEM((2, TOP_K, TM_C, d), F32), pltpu.SemaphoreType.DMA((2,))],
    )
    return pl.pallas_call(
        _combine_kernel,
        out_shape=jax.ShapeDtypeStruct((m, d), F32),
        grid_spec=grid_spec,
        compiler_params=_cparams(("arbitrary",)),
        name="combine",
    )(pos, o_sorted, x1, wts, mod48)


def _rope_tables(n_lat, tm):
    t = jnp.arange(n_lat)
    r = (t // GRID_W).astype(F32)
    col = (t % GRID_W).astype(F32)

    def cos_sin(dim):
        nf = dim // 4
        inv = ROPE_THETA ** (-jnp.arange(nf, dtype=F32) / nf)
        ang = jnp.concatenate([r[:, None] * inv, col[:, None] * inv], axis=-1)
        return jnp.cos(ang), jnp.sin(ang)

    c64, s64 = cos_sin(A_DK)
    c128, s128 = cos_sin(B_DH)
    z32 = jnp.zeros_like(s64)
    tabs = [
        jnp.concatenate([c64, c64, c64, c64], axis=-1),
        jnp.concatenate([-s64, z32, -s64, z32], axis=-1),
        jnp.concatenate([z32, s64, z32, s64], axis=-1),
        jnp.concatenate([c128, c128], axis=-1),
        jnp.concatenate([-s128, s128], axis=-1),
    ]
    ident = [jnp.ones((tm, LANES), F32), jnp.zeros((tm, LANES), F32), jnp.zeros((tm, LANES), F32),
             jnp.ones((tm, LANES), F32), jnp.zeros((tm, LANES), F32)]
    return [jnp.concatenate([a, b], axis=0) for a, b in zip(tabs, ident)]


def _layer_params(l, w_in, w_out, a_qn, a_kn, b_qn, b_kn, c_qa_norm, c_kva_norm, c_wuq, c_wukv, c_qn, c_kn):
    w_in_bf = jnp.pad(w_in[l], ((0, 0), (0, D_IN_PAD - D_IN))).astype(BF16)
    z64 = jnp.zeros((C_ROPE,), F32)
    gains = jnp.stack([
        jnp.tile(a_qn[l], 2) * (A_DK ** -0.5),
        jnp.tile(a_kn[l], 2),
        b_qn[l] * (B_DH ** -0.5),
        b_kn[l],
        c_qn[l][:C_NOPE] * (C_DQK ** -0.5),
        jnp.concatenate([c_qn[l][C_NOPE:] * (C_DQK ** -0.5), z64]),
        c_kn[l][:C_NOPE],
        jnp.concatenate([c_kn[l][C_NOPE:], z64]),
    ])
    wq = c_wuq[l].reshape(C_Q_RANK, C_HEADS, C_DQK)
    wq = jnp.pad(wq, ((0, 0), (0, 0), (0, C_HEAD_PAD - C_DQK))).reshape(C_Q_RANK, C_HEADS * C_HEAD_PAD)
    return dict(w_in=w_in_bf, gains=gains, gcq=c_qa_norm[l][None], gckv=c_kva_norm[l][None],
                wuq=wq.astype(BF16), wukv=c_wukv[l].astype(BF16), w_out=w_out[l].astype(BF16))


def _sorted_rows(idx):
    t = idx.shape[0]
    n_pairs = t * TOP_K
    n_rows = ((n_pairs + N_EXPERTS * (TM_E - 1)) // TM_E) * TM_E
    nt = n_rows // TM_E
    flat_e = idx.reshape(-1)
    onehot = (flat_e[:, None] == jnp.arange(N_EXPERTS)[None, :]).astype(jnp.int32)
    csum = jnp.cumsum(onehot, axis=0)
    rank = jnp.take_along_axis(csum, flat_e[:, None], axis=1)[:, 0] - 1
    counts = csum[-1]
    padded = ((counts + TM_E - 1) // TM_E) * TM_E
    ends = jnp.cumsum(padded)
    pos = (ends - padded)[flat_e] + rank
    src = jnp.zeros((n_rows + TM_E,), jnp.int32).at[pos].set(
        jnp.arange(n_pairs, dtype=jnp.int32) // TOP_K, unique_indices=True)
    tile_start = jnp.arange(nt, dtype=jnp.int32) * TM_E
    tile_expert = jnp.sum((ends[None, :] <= tile_start[:, None]).astype(jnp.int32), axis=1)
    last_used = jnp.sum((ends <= ends[-1] - 1).astype(jnp.int32))
    tile_expert = jnp.minimum(tile_expert, last_used)
    n_valid = (ends[-1:] // TM_E).astype(jnp.int32)
    return pos.astype(jnp.int32), src, tile_expert.astype(jnp.int32), n_valid


def kernel(x, c, ctx, c_ctx, ada_w, ada_b, norm1_g, norm2_g, w_in, w_out, a_qn, a_kn, a_lambda, a_subln,
           b_qn, b_kn, b_sink, c_qa_norm, c_kva_norm, c_wuq, c_wukv, c_qn, c_kn,
           router_w, router_bias, moe_w_gate, moe_w_up, moe_w_down):
    bsz, n_lat, d = x.shape
    n_ctx = ctx.shape[1]
    depth = ada_w.shape[0]
    t_lat, t_ctx = bsz * n_lat, bsz * n_ctx
    tm = TM_PREP
    lat_tiles = n_lat // tm
    n_lat_tiles, n_ctx_tiles = t_lat // tm, t_ctx // tm
    n_all_tiles = n_lat_tiles + n_ctx_tiles

    cond8 = jnp.concatenate([c, c_ctx[None], jnp.zeros((8 - bsz - 1, d), F32)], axis=0)
    mod = _ada_modulation(cond8, ada_w, ada_b)
    tables = _rope_tables(n_lat, tm)
    rb = router_bias[None]

    mod_row = lambda i: jnp.minimum(i // lat_tiles, bsz)
    mod_row_c = lambda i: jnp.minimum(i // (n_lat // TM_C), bsz)
    rope_blk = lambda i: jnp.where(i < n_lat_tiles, i % lat_tiles, lat_tiles)

    xa, xb, n_a_tiles = x.reshape(t_lat, d), ctx.reshape(t_ctx, d), n_lat_tiles
    lat_src, ctx_src = (0, n_lat), (t_lat, n_ctx)
    for l in range(depth):
        last = l == depth - 1
        p = _layer_params(l, w_in, w_out, a_qn, a_kn, b_qn, b_kn, c_qa_norm, c_kva_norm,
                          c_wuq, c_wukv, c_qn, c_kn)
        mod48 = mod[l].reshape(8 * 6, 1, d)
        aq, ak, av, bq, bk, bv, cq, ck, cv = _prep(
            xa, xb, n_all_tiles, n_a_tiles, mod48, mod_row, norm1_g[l][None], p["w_in"], tables, rope_blk,
            p["gains"], p["gcq"], p["gckv"], p["wuq"], p["wukv"])

        lv, gsub, sink = a_lambda[l], a_subln[l][None], b_sink[l]
        o_a = _attn_a(lv, gsub, aq, ak, av, 0, n_lat, [lat_src, ctx_src], bsz, l)
        o_b = _attn_b(sink, bq, bk, bv, bsz, n_lat, n_ctx)
        o_c = _attn_c(cq, ck, cv, 0, n_lat, [lat_src, ctx_src], bsz)
        set_a = (o_a, o_b, o_c, xa)
        if not last:
            oc_a = _attn_a(lv, gsub, aq, ak, av, t_lat, n_ctx, [ctx_src], bsz, l)
            oc_b = _attn_b_ctx(sink, bq, bk, bv, bsz, n_lat, n_ctx)
            oc_c = _attn_c(cq, ck, cv, t_lat, n_ctx, [ctx_src], bsz)
            set_b, n_tok_tiles = (oc_a, oc_b, oc_c, xb), n_all_tiles
        else:
            set_b, n_tok_tiles = set_a, n_lat_tiles
        x1, h2, idx, wts = _outproj(set_a, set_b, n_tok_tiles, min(n_a_tiles, n_tok_tiles), mod48, mod_row,
                                    norm2_g[l][None], p["w_out"], router_w, rb)

        pos, src, tile_expert, n_valid = _sorted_rows(idx[:, :TOP_K])
        o_sorted = _experts(l, tile_expert, n_valid, src, h2, moe_w_gate, moe_w_up, moe_w_down)
        xa = _combine(pos, o_sorted, x1, wts, mod48, mod_row_c)
        xb, n_a_tiles = xa, n_all_tiles
    return xa.reshape(bsz, n_lat, d)
```

```python
import functools
import math

import jax
import jax.numpy as jnp
from jax import lax
from jax.experimental import pallas as pl
from jax.experimental.pallas import tpu as pltpu

F32 = jnp.float32
BF16 = jnp.bfloat16

D_MODEL = 2048
GRID_W = 64
BLOCK = 128
WINDOW = 128
ROPE_THETA = 10000.0
EPS = 1e-6
NEG_INF = -1e30
A_HEADS, A_DK = 4, 64
A_DV = 2 * A_DK
B_HEADS, B_KV_HEADS, B_DH = 8, 2, 128
B_GROUP = B_HEADS // B_KV_HEADS
C_HEADS, C_Q_RANK, C_KV_RANK, C_NOPE, C_ROPE, C_DV = 4, 512, 256, 128, 64, 128
C_DQK = C_NOPE + C_ROPE
SPLIT_SIZES = (A_HEADS * 2 * A_DK, A_HEADS * 2 * A_DK, A_HEADS * A_DV,
               B_HEADS * B_DH, B_KV_HEADS * B_DH, B_KV_HEADS * B_DH,
               C_Q_RANK, C_KV_RANK, C_ROPE)
D_IN = sum(SPLIT_SIZES)
N_EXPERTS, N_GROUPS, TOP_K = 32, 4, 2
EXPERTS_PER_GROUP = N_EXPERTS // N_GROUPS
D_EXPERT = 512

LANES = 128
V7X_VMEM_LIMIT = 56 * 1024 * 1024

D_IN_PAD = ((D_IN + LANES - 1) // LANES) * LANES
C_HEAD_PAD = 2 * LANES
TM_PREP = 256
TQ_A = 256
TQ_C = 512
QB_B = 512
TM_E = 128
X_SLOTS = 3
TM_C = 128
ADA_TN = 1024

_OFF = [0]
for _s in SPLIT_SIZES:
    _OFF.append(_OFF[-1] + _s)
O_AQ, O_AK, O_AV, O_BQ, O_BK, O_BV, O_CQ, O_CKV, O_CKR, _ = _OFF


def _cparams(sem):
    return pltpu.CompilerParams(dimension_semantics=sem, vmem_limit_bytes=V7X_VMEM_LIMIT)


def _silu(v):
    return v * (1.0 / (1.0 + jnp.exp(-v)))


def _ada_kernel(cond_ref, w_ref, b_ref, o_ref):
    s = _silu(cond_ref[...]).astype(BF16)
    o_ref[0] = jnp.dot(s, w_ref[0].astype(BF16), preferred_element_type=F32) + b_ref[0]


def _ada_modulation(cond8, ada_w, ada_b):
    depth, d, n = ada_w.shape
    return pl.pallas_call(
        _ada_kernel,
        out_shape=jax.ShapeDtypeStruct((depth, 8, n), F32),
        grid=(depth, n // ADA_TN),
        in_specs=[
            pl.BlockSpec((8, d), lambda l, j: (0, 0)),
            pl.BlockSpec((1, d, ADA_TN), lambda l, j: (l, 0, j)),
            pl.BlockSpec((1, 1, ADA_TN), lambda l, j: (l, 0, j)),
        ],
        out_specs=pl.BlockSpec((1, 8, ADA_TN), lambda l, j: (l, 0, j)),
        compiler_params=_cparams(("arbitrary", "arbitrary")),
        name="ada_modulation",
    )(cond8, ada_w, ada_b.reshape(depth, 1, n))


def _rope64(v, c, sa, sb):
    return v * c + pltpu.roll(v, 96, 1) * sa + pltpu.roll(v, 32, 1) * sb


def _rope128(v, c, s):
    return v * c + pltpu.roll(v, 64, 1) * s


def _norm_seg128(v, g):
    ms = jnp.sum(v * v, axis=-1, keepdims=True) * (1.0 / 128)
    return v * lax.rsqrt(ms + EPS) * g


def _norm_seg64x2(v, g, lo):
    sq = v * v
    s_lo = jnp.sum(jnp.where(lo, sq, 0.0), axis=-1, keepdims=True)
    s_hi = jnp.sum(jnp.where(lo, 0.0, sq), axis=-1, keepdims=True)
    ms = jnp.where(lo, s_lo, s_hi) * (1.0 / 64)
    return v * lax.rsqrt(ms + EPS) * g


def _norm_low64(v, g):
    ms = jnp.sum(v * v, axis=-1, keepdims=True) * (1.0 / 64)
    return v * lax.rsqrt(ms + EPS) * g


def _prep_kernel(xa_ref, xb_ref, sh_ref, sc_ref, g1_ref, w_ref, ca_ref, saa_ref, sab_ref, cb_ref, sb_ref,
                 gains_ref, gcq_ref, gckv_ref, wuq_ref, wukv_ref,
                 aq_ref, ak_ref, av_ref, bq_ref, bk_ref, bv_ref, cq_ref, ck_ref, cv_ref, *, n_a_tiles):
    x = jnp.where(pl.program_id(0) < n_a_tiles, xa_ref[...], xb_ref[...])
    ms = jnp.mean(x * x, axis=-1, keepdims=True)
    h = x * lax.rsqrt(ms + EPS) * g1_ref[...]
    h = h * (1.0 + sc_ref[0]) + sh_ref[0]
    z = jnp.dot(h.astype(BF16), w_ref[...], preferred_element_type=F32)

    lane = lax.broadcasted_iota(jnp.int32, (1, LANES), 1)
    lo = lane < 64
    ca, saa, sab = ca_ref[...], saa_ref[...], sab_ref[...]
    cb, sb = cb_ref[...], sb_ref[...]
    g_aq, g_ak, g_bq, g_bk = gains_ref[0:1], gains_ref[1:2], gains_ref[2:3], gains_ref[3:4]
    g_cqn, g_cqr, g_ckn, g_ckr = gains_ref[4:5], gains_ref[5:6], gains_ref[6:7], gains_ref[7:8]

    def blk(off, j):
        return z[:, off + j * LANES: off + (j + 1) * LANES]

    for j in range(A_HEADS):
        sl = slice(j * LANES, (j + 1) * LANES)
        aq_ref[:, sl] = _rope64(_norm_seg64x2(blk(O_AQ, j), g_aq, lo), ca, saa, sab).astype(BF16)
        ak_ref[:, sl] = _rope64(_norm_seg64x2(blk(O_AK, j), g_ak, lo), ca, saa, sab).astype(BF16)
        av_ref[:, sl] = blk(O_AV, j).astype(BF16)
    for j in range(B_HEADS):
        sl = slice(j * LANES, (j + 1) * LANES)
        bq_ref[:, sl] = _rope128(_norm_seg128(blk(O_BQ, j), g_bq), cb, sb).astype(BF16)
    for j in range(B_KV_HEADS):
        sl = slice(j * LANES, (j + 1) * LANES)
        bk_ref[:, sl] = _rope128(_norm_seg128(blk(O_BK, j), g_bk), cb, sb).astype(BF16)
        bv_ref[:, sl] = blk(O_BV, j).astype(BF16)
    cq = z[:, O_CQ:O_CQ + C_Q_RANK]
    cqn = cq * lax.rsqrt(jnp.mean(cq * cq, axis=-1, keepdims=True) + EPS) * gcq_ref[...]
    q = jnp.dot(cqn.astype(BF16), wuq_ref[...], preferred_element_type=F32)
    ckv = z[:, O_CKV:O_CKV + C_KV_RANK]
    ckvn = ckv * lax.rsqrt(jnp.mean(ckv * ckv, axis=-1, keepdims=True) + EPS) * gckv_ref[...]
    kv = jnp.dot(ckvn.astype(BF16), wukv_ref[...], preferred_element_type=F32)
    krope = _rope64(_norm_low64(z[:, O_CKR:O_CKR + LANES], g_ckr), ca, saa, sab).astype(BF16)
    for hh in range(C_HEADS):
        b0 = hh * C_HEAD_PAD
        cq_ref[:, b0:b0 + LANES] = _norm_seg128(q[:, b0:b0 + LANES], g_cqn).astype(BF16)
        cq_ref[:, b0 + LANES:b0 + 2 * LANES] = _rope64(
            _norm_low64(q[:, b0 + LANES:b0 + 2 * LANES], g_cqr), ca, saa, sab).astype(BF16)
        ck_ref[:, b0:b0 + LANES] = _norm_seg128(kv[:, b0:b0 + LANES], g_ckn).astype(BF16)
        ck_ref[:, b0 + LANES:b0 + 2 * LANES] = krope
        cv_ref[:, hh * LANES:(hh + 1) * LANES] = kv[:, b0 + LANES:b0 + 2 * LANES].astype(BF16)


def _split_rows(n_a_tiles):
    first = lambda i: (jnp.minimum(i, n_a_tiles - 1), 0)
    second = lambda i: (jnp.maximum(i - n_a_tiles, 0), 0)
    return first, second


def _prep(xa, xb, n_tiles, n_a_tiles, mod48, mod_row, g1, w_in_bf, tables, rope_blk, gains, gcq, gckv,
          wuq_bf, wukv_bf):
    d = xa.shape[1]
    tm = TM_PREP
    m = n_tiles * tm
    row = lambda i: (i, 0)
    const = lambda i: (0, 0)
    first, second = _split_rows(n_a_tiles)
    tab_spec = pl.BlockSpec((tm, LANES), lambda i: (rope_blk(i), 0))
    widths = (512, 512, 512, 1024, 256, 256, C_HEADS * C_HEAD_PAD, C_HEADS * C_HEAD_PAD, 512)
    return pl.pallas_call(
        functools.partial(_prep_kernel, n_a_tiles=n_a_tiles),
        out_shape=[jax.ShapeDtypeStruct((m, w), BF16) for w in widths],
        grid=(n_tiles,),
        in_specs=[
            pl.BlockSpec((tm, d), first),
            pl.BlockSpec((tm, d), second),
            pl.BlockSpec((1, 1, d), lambda i: (mod_row(i) * 6 + 0, 0, 0)),
            pl.BlockSpec((1, 1, d), lambda i: (mod_row(i) * 6 + 1, 0, 0)),
            pl.BlockSpec((1, d), const),
            pl.BlockSpec((d, D_IN_PAD), const, pipeline_mode=pl.Buffered(1)),
            tab_spec, tab_spec, tab_spec, tab_spec, tab_spec,
            pl.BlockSpec((8, LANES), const),
            pl.BlockSpec((1, C_Q_RANK), const),
            pl.BlockSpec((1, C_KV_RANK), const),
            pl.BlockSpec((C_Q_RANK, C_HEADS * C_HEAD_PAD), const, pipeline_mode=pl.Buffered(1)),
            pl.BlockSpec((C_KV_RANK, C_HEADS * C_HEAD_PAD), const, pipeline_mode=pl.Buffered(1)),
        ],
        out_specs=[pl.BlockSpec((tm, w), row) for w in widths],
        compiler_params=_cparams(("arbitrary",)),
        name="prep",
    )(xa, xb, mod48, mod48, g1, w_in_bf, *tables, gains, gcq, gckv, wuq_bf, wukv_bf)


def _dot_nt(a, b):
    return lax.dot_general(a, b, (((1,), (1,)), ((), ())), preferred_element_type=F32)


def _attn_a_kernel(*refs, n_src, lam_init):
    lv_ref, gsub_ref, q_ref = refs[0], refs[1], refs[2]
    k_refs = refs[3:3 + n_src]
    v_refs = refs[3 + n_src:3 + 2 * n_src]
    o_ref = refs[3 + 2 * n_src]
    tq = q_ref.shape[0]
    lv = lv_ref[...]
    lam = (jnp.exp(jnp.sum(lv[0:1] * lv[1:2], axis=-1, keepdims=True))
           - jnp.exp(jnp.sum(lv[2:3] * lv[3:4], axis=-1, keepdims=True)) + lam_init)
    q = q_ref[...]
    lo = lax.broadcasted_iota(jnp.int32, (1, LANES), 1) < 64
    zero = jnp.zeros_like(q)
    qq = jnp.concatenate([jnp.where(lo, q, zero), jnp.where(lo, zero, q)], axis=0)
    s = [_dot_nt(qq, k[...]) for k in k_refs]
    m = functools.reduce(jnp.maximum, [jnp.max(si, axis=-1, keepdims=True) for si in s])
    e = [jnp.exp(si - m) for si in s]
    l = functools.reduce(jnp.add, [jnp.sum(ei, axis=-1, keepdims=True) for ei in e])
    o2 = functools.reduce(jnp.add, [jnp.dot(ei.astype(BF16), v[...], preferred_element_type=F32)
                                    for ei, v in zip(e, v_refs)])
    o2 = o2 * (1.0 / l)
    o = o2[:tq] - lam * o2[tq:]
    ms = jnp.mean(o * o, axis=-1, keepdims=True)
    o_ref[...] = (o * lax.rsqrt(ms + EPS) * gsub_ref[...] * (1.0 - lam_init)).astype(BF16)


def _attn_a(lv, gsub, aq, ak, av, q_row0, q_rows, srcs, n_batch, layer_idx):
    tq = min(TQ_A, q_rows)
    nq = q_rows // tq
    qb0 = q_row0 // tq
    lam_init = 0.8 - 0.6 * math.exp(-0.3 * layer_idx)
    in_specs = [
        pl.BlockSpec((4, A_DK), lambda b, h, i: (0, 0)),
        pl.BlockSpec((1, A_DV), lambda b, h, i: (0, 0)),
        pl.BlockSpec((tq, LANES), lambda b, h, i: (qb0 + b * nq + i, h)),
    ]
    kv_specs = [pl.BlockSpec((rows, LANES), lambda b, h, i, blk0=row0 // rows: (blk0 + b, h))
                for row0, rows in srcs]
    return pl.pallas_call(
        functools.partial(_attn_a_kernel, n_src=len(srcs), lam_init=lam_init),
        out_shape=jax.ShapeDtypeStruct((n_batch * q_rows, A_HEADS * A_DV), BF16),
        grid=(n_batch, A_HEADS, nq),
        in_specs=in_specs + kv_specs + kv_specs,
        out_specs=pl.BlockSpec((tq, LANES), lambda b, h, i: (b * nq + i, h)),
        compiler_params=_cparams(("arbitrary", "arbitrary", "arbitrary")),
        name="attn_a",
    )(lv, gsub, aq, *([ak] * len(srcs)), *([av] * len(srcs)))


def _attn_c_kernel(*refs, n_src):
    q_ref = refs[0]
    k_refs = refs[1:1 + n_src]
    v_refs = refs[1 + n_src:1 + 2 * n_src]
    o_ref = refs[1 + 2 * n_src]
    q = q_ref[...]
    s = [_dot_nt(q, k[...]) for k in k_refs]
    m = functools.reduce(jnp.maximum, [jnp.max(si, axis=-1, keepdims=True) for si in s])
    e = [jnp.exp(si - m) for si in s]
    l = functools.reduce(jnp.add, [jnp.sum(ei, axis=-1, keepdims=True) for ei in e])
    o = functools.reduce(jnp.add, [jnp.dot(ei.astype(BF16), v[...], preferred_element_type=F32)
                                   for ei, v in zip(e, v_refs)])
    o_ref[...] = (o * (1.0 / l)).astype(BF16)


def _attn_c(cq, ck, cv, q_row0, q_rows, srcs, n_batch):
    tq = min(TQ_C, q_rows)
    nq = q_rows // tq
    qb0 = q_row0 // tq
    in_specs = [pl.BlockSpec((tq, C_HEAD_PAD), lambda b, h, i: (qb0 + b * nq + i, h))]
    k_specs = [pl.BlockSpec((rows, C_HEAD_PAD), lambda b, h, i, blk0=row0 // rows: (blk0 + b, h))
               for row0, rows in srcs]
    v_specs = [pl.BlockSpec((rows, C_DV), lambda b, h, i, blk0=row0 // rows: (blk0 + b, h))
               for row0, rows in srcs]
    return pl.pallas_call(
        functools.partial(_attn_c_kernel, n_src=len(srcs)),
        out_shape=jax.ShapeDtypeStruct((n_batch * q_rows, C_HEADS * C_DV), BF16),
        grid=(n_batch, C_HEADS, nq),
        in_specs=in_specs + k_specs + v_specs,
        out_specs=pl.BlockSpec((tq, C_DV), lambda b, h, i: (b * nq + i, h)),
        compiler_params=_cparams(("arbitrary", "arbitrary", "arbitrary")),
        name="attn_c",
    )(cq, *([ck] * len(srcs)), *([cv] * len(srcs)))


def _stack_heads(q):
    return jnp.concatenate([q[:, g * LANES:(g + 1) * LANES] for g in range(B_GROUP)], axis=0)


def _sink_rows(sink_ref, kvh, rows):
    return jnp.concatenate(
        [jnp.full((rows, 1), sink_ref[kvh * B_GROUP + g], F32) for g in range(B_GROUP)], axis=0)


def _attn_b_kernel(sink_ref, q_ref, kp_ref, km_ref, kn_ref, vp_ref, vm_ref, vn_ref, kc_ref, vc_ref, o_ref):
    kvh = pl.program_id(1)
    qb = pl.program_id(2)
    nqb = pl.num_programs(2)
    n_blk = QB_B // BLOCK
    kband = jnp.concatenate([kp_ref[...], km_ref[...], kn_ref[...]], axis=0)
    vband = jnp.concatenate([vp_ref[...], vm_ref[...], vn_ref[...]], axis=0)
    kc, vc = kc_ref[...], vc_ref[...]
    sink = _sink_rows(sink_ref, kvh, BLOCK)
    r = lax.broadcasted_iota(jnp.int32, (B_GROUP * BLOCK, 3 * BLOCK), 0) % BLOCK
    c = lax.broadcasted_iota(jnp.int32, (B_GROUP * BLOCK, 3 * BLOCK), 1)
    cr = c - r
    band_ok = (cr >= 0) & (cr <= BLOCK + WINDOW)
    for j in range(n_blk):
        q4 = _stack_heads(q_ref[j * BLOCK:(j + 1) * BLOCK, :])
        s_loc = _dot_nt(q4, kband[j * BLOCK:(j + 3) * BLOCK])
        valid = band_ok
        if j == 0:
            valid = valid & (c >= jnp.where(qb > 0, 0, BLOCK))
        if j == n_blk - 1:
            valid = valid & (c < jnp.where(qb < nqb - 1, 3 * BLOCK, 2 * BLOCK))
        s_loc = jnp.where(valid, s_loc, NEG_INF)
        s_ctx = _dot_nt(q4, kc)
        m = jnp.maximum(jnp.maximum(jnp.max(s_loc, axis=-1, keepdims=True),
                                    jnp.max(s_ctx, axis=-1, keepdims=True)), sink)
        e_loc, e_ctx = jnp.exp(s_loc - m), jnp.exp(s_ctx - m)
        l = (jnp.sum(e_loc, axis=-1, keepdims=True) + jnp.sum(e_ctx, axis=-1, keepdims=True)
             + jnp.exp(sink - m))
        o = (jnp.dot(e_loc.astype(BF16), vband[j * BLOCK:(j + 3) * BLOCK], preferred_element_type=F32)
             + jnp.dot(e_ctx.astype(BF16), vc, preferred_element_type=F32)) * (1.0 / l)
        for g in range(B_GROUP):
            o_ref[j * BLOCK:(j + 1) * BLOCK, g * LANES:(g + 1) * LANES] = (
                o[g * BLOCK:(g + 1) * BLOCK].astype(BF16))


def _attn_b(sink, bq, bk, bv, n_batch, n_lat, n_ctx):
    nqb = n_lat // QB_B
    per = QB_B // BLOCK
    blocks_per_batch = n_lat // BLOCK
    ctx_blk0 = n_batch * n_lat // n_ctx
    gw = B_GROUP * B_DH
    prev = lambda b, h, i: (b * blocks_per_batch + jnp.maximum(i * per - 1, 0), h)
    main = lambda b, h, i: (b * nqb + i, h)
    nxt = lambda b, h, i: (b * blocks_per_batch + jnp.minimum(i * per + per, blocks_per_batch - 1), h)
    ctx = lambda b, h, i: (ctx_blk0 + b, h)
    return pl.pallas_call(
        _attn_b_kernel,
        out_shape=jax.ShapeDtypeStruct((n_batch * n_lat, B_HEADS * B_DH), BF16),
        grid=(n_batch, B_KV_HEADS, nqb),
        in_specs=[
            pl.BlockSpec(memory_space=pltpu.SMEM),
            pl.BlockSpec((QB_B, gw), main),
            pl.BlockSpec((BLOCK, B_DH), prev), pl.BlockSpec((QB_B, B_DH), main), pl.BlockSpec((BLOCK, B_DH), nxt),
            pl.BlockSpec((BLOCK, B_DH), prev), pl.BlockSpec((QB_B, B_DH), main), pl.BlockSpec((BLOCK, B_DH), nxt),
            pl.BlockSpec((n_ctx, B_DH), ctx), pl.BlockSpec((n_ctx, B_DH), ctx),
        ],
        out_specs=pl.BlockSpec((QB_B, gw), main),
        compiler_params=_cparams(("arbitrary", "arbitrary", "arbitrary")),
        name="attn_b",
    )(sink, bq, bk, bk, bk, bv, bv, bv, bk, bv)


def _attn_b_ctx_kernel(sink_ref, q_ref, k_ref, v_ref, o_ref):
    kvh = pl.program_id(1)
    rows = q_ref.shape[0]
    q4 = _stack_heads(q_ref[...])
    sink = _sink_rows(sink_ref, kvh, rows)
    s = _dot_nt(q4, k_ref[...])
    m = jnp.maximum(jnp.max(s, axis=-1, keepdims=True), sink)
    e = jnp.exp(s - m)
    l = jnp.sum(e, axis=-1, keepdims=True) + jnp.exp(sink - m)
    o = jnp.dot(e.astype(BF16), v_ref[...], preferred_element_type=F32) * (1.0 / l)
    for g in range(B_GROUP):
        o_ref[:, g * LANES:(g + 1) * LANES] = o[g * rows:(g + 1) * rows].astype(BF16)


def _attn_b_ctx(sink, bq, bk, bv, n_batch, n_lat, n_ctx):
    gw = B_GROUP * B_DH
    ctx_blk0 = n_batch * n_lat // n_ctx
    return pl.pallas_call(
        _attn_b_ctx_kernel,
        out_shape=jax.ShapeDtypeStruct((n_batch * n_ctx, B_HEADS * B_DH), BF16),
        grid=(n_batch, B_KV_HEADS),
        in_specs=[
            pl.BlockSpec(memory_space=pltpu.SMEM),
            pl.BlockSpec((n_ctx, gw), lambda b, h: (ctx_blk0 + b, h)),
            pl.BlockSpec((n_ctx, B_DH), lambda b, h: (ctx_blk0 + b, h)),
            pl.BlockSpec((n_ctx, B_DH), lambda b, h: (ctx_blk0 + b, h)),
        ],
        out_specs=pl.BlockSpec((n_ctx, gw), lambda b, h: (b, h)),
        compiler_params=_cparams(("arbitrary", "arbitrary")),
        name="attn_b_ctx",
    )(sink, bq, bk, bv)


def _route(h2, rw_ref, rb_ref):
    logits = jnp.dot(h2, rw_ref[...], preferred_element_type=F32, precision=lax.Precision.HIGHEST)
    scores = 1.0 / (1.0 + jnp.exp(-logits))
    sel = scores + rb_ref[...]
    lane_i = lax.broadcasted_iota(jnp.int32, sel.shape, 1)
    lane = lane_i.astype(F32)
    big = float(N_EXPERTS)

    def top2(mask):
        v = jnp.where(mask, sel, -jnp.inf)
        m1 = jnp.max(v, axis=-1, keepdims=True)
        i1 = jnp.min(jnp.where(v == m1, lane, big), axis=-1, keepdims=True)
        v2 = jnp.where(lane == i1, -jnp.inf, v)
        m2 = jnp.max(v2, axis=-1, keepdims=True)
        i2 = jnp.min(jnp.where(v2 == m2, lane, big), axis=-1, keepdims=True)
        return m1, i1, m2, i2

    best = None
    for g in range(N_GROUPS):
        m1, i1, m2, i2 = top2((lane_i >= g * EXPERTS_PER_GROUP) & (lane_i < (g + 1) * EXPERTS_PER_GROUP))
        gs = m1 + m2
        if best is None:
            best = (gs, i1, i2)
        else:
            take = gs > best[0]
            best = (jnp.where(take, gs, best[0]), jnp.where(take, i1, best[1]), jnp.where(take, i2, best[2]))
    _, e1, e2 = best
    w1 = jnp.sum(jnp.where(lane == e1, scores, 0.0), axis=-1, keepdims=True)
    w2 = jnp.sum(jnp.where(lane == e2, scores, 0.0), axis=-1, keepdims=True)
    tot = w1 + w2
    return e1, e2, w1 / tot, w2 / tot


def _outproj_kernel(oa1_ref, ob1_ref, oc1_ref, x1in_ref, oa2_ref, ob2_ref, oc2_ref, x2in_ref,
                    g1_ref, sh2_ref, sc2_ref, n2_ref, w_ref, rw_ref, rb_ref,
                    x1_ref, h2_ref, idx_ref, wt_ref, *, n_a_tiles):
    first = pl.program_id(0) < n_a_tiles
    oa = jnp.where(first, oa1_ref[...], oa2_ref[...])
    ob = jnp.where(first, ob1_ref[...], ob2_ref[...])
    oc = jnp.where(first, oc1_ref[...], oc2_ref[...])
    x = jnp.where(first, x1in_ref[...], x2in_ref[...])
    na, nb = oa.shape[1], ob.shape[1]
    y = (jnp.dot(oa, w_ref[0:na, :], preferred_element_type=F32)
         + jnp.dot(ob, w_ref[na:na + nb, :], preferred_element_type=F32)
         + jnp.dot(oc, w_ref[na + nb:, :], preferred_element_type=F32))
    x1 = x + g1_ref[0] * y
    x1_ref[...] = x1
    ms = jnp.mean(x1 * x1, axis=-1, keepdims=True)
    h2 = x1 * lax.rsqrt(ms + EPS) * n2_ref[...]
    h2 = h2 * (1.0 + sc2_ref[0]) + sh2_ref[0]
    h2_ref[...] = h2
    e1, e2, w1, w2 = _route(h2, rw_ref, rb_ref)
    lane = lax.broadcasted_iota(jnp.int32, idx_ref.shape, 1)
    idx_ref[...] = jnp.where(lane == 0, e1, jnp.where(lane == 1, e2, 0.0)).astype(jnp.int32)
    wt_ref[...] = jnp.where(lane == 0, w1, jnp.where(lane == 1, w2, 0.0))


def _outproj(set_a, set_b, n_tiles, n_a_tiles, mod48, mod_row, n2, w_out_bf, router_w, router_b):
    d = set_a[3].shape[1]
    tm = TM_PREP
    m = n_tiles * tm
    row = lambda i: (i, 0)
    const = lambda i: (0, 0)
    first, second = _split_rows(n_a_tiles)
    mod_spec = lambda j: pl.BlockSpec((1, 1, d), lambda i: (mod_row(i) * 6 + j, 0, 0))
    in_specs = (
        [pl.BlockSpec((tm, a.shape[1]), first) for a in set_a]
        + [pl.BlockSpec((tm, a.shape[1]), second) for a in set_b]
        + [mod_spec(2), mod_spec(3), mod_spec(4),
           pl.BlockSpec((1, d), const),
           pl.BlockSpec((d, d), const, pipeline_mode=pl.Buffered(1)),
           pl.BlockSpec((d, N_EXPERTS), const),
           pl.BlockSpec((1, N_EXPERTS), const)])
    return pl.pallas_call(
        functools.partial(_outproj_kernel, n_a_tiles=n_a_tiles),
        out_shape=[jax.ShapeDtypeStruct((m, d), F32), jax.ShapeDtypeStruct((m, d), F32),
                   jax.ShapeDtypeStruct((m, LANES), jnp.int32), jax.ShapeDtypeStruct((m, LANES), F32)],
        grid=(n_tiles,),
        in_specs=in_specs,
        out_specs=[pl.BlockSpec((tm, d), row), pl.BlockSpec((tm, d), row),
                   pl.BlockSpec((tm, LANES), row), pl.BlockSpec((tm, LANES), row)],
        compiler_params=_cparams(("arbitrary",)),
        name="outproj",
    )(*set_a, *set_b, mod48, mod48, mod48, n2, w_out_bf, router_w, router_b)


def _experts_kernel(te_ref, nxt_ref, ws_ref, nv_ref, src_ref, h2_hbm, wg_hbm, wu_hbm, wd_hbm, o_ref,
                    xbuf, xsem, wg_f, wu_f, wd_f, wsem, wg_bf, wu_bf, wd_bf, *, layer):
    i = pl.program_id(0)
    n_valid = nv_ref[0]
    slot = i % X_SLOTS

    def issue_rows(tile):
        s = tile % X_SLOTS
        base = tile * TM_E
        for r in range(TM_E):
            pltpu.make_async_copy(h2_hbm.at[pl.ds(src_ref[base + r], 1), :],
                                  xbuf.at[s, pl.ds(r, 1), :], xsem.at[s]).start()

    def wait_rows(s):
        pltpu.make_async_copy(h2_hbm.at[pl.ds(0, TM_E), :], xbuf.at[s], xsem.at[s]).wait()

    def weight_copies(e, s):
        return (pltpu.make_async_copy(wg_hbm.at[layer, e], wg_f.at[s], wsem.at[s]),
                pltpu.make_async_copy(wu_hbm.at[layer, e], wu_f.at[s], wsem.at[s]),
                pltpu.make_async_copy(wd_hbm.at[layer, e], wd_f.at[s], wsem.at[s]))

    @pl.when(i == 0)
    def _():
        for cp in weight_copies(te_ref[0], ws_ref[0]):
            cp.start()
        for t in range(X_SLOTS - 1):
            issue_rows(t)

    first_of_expert = jnp.logical_or(i == 0, te_ref[i] != te_ref[jnp.maximum(i - 1, 0)])

    @pl.when(jnp.logical_and(first_of_expert, i < n_valid))
    def _():
        ws = ws_ref[i]
        for cp in weight_copies(te_ref[i], ws):
            cp.wait()

        @pl.when(nxt_ref[i] >= 0)
        def _():
            for cp in weight_copies(nxt_ref[i], 1 - ws):
                cp.start()

        wg_bf[...] = wg_f[ws].astype(BF16)
        wu_bf[...] = wu_f[ws].astype(BF16)
        wd_bf[...] = wd_f[ws].astype(BF16)

    @pl.when(i < n_valid)
    def _():
        wait_rows(slot)
        x = xbuf[slot].astype(BF16)
        issue_rows(i + X_SLOTS - 1)
        g = jnp.dot(x, wg_bf[...], preferred_element_type=F32)
        u = jnp.dot(x, wu_bf[...], preferred_element_type=F32)
        a = (_silu(g) * u).astype(BF16)
        o_ref[...] = jnp.dot(a, wd_bf[...], preferred_element_type=F32)

    @pl.when(i == n_valid - 1)
    def _():
        for t in range(1, X_SLOTS):
            wait_rows((i + t) % X_SLOTS)

    @pl.when(i >= n_valid)
    def _():
        o_ref[...] = jnp.zeros_like(o_ref)


def _experts(layer, tile_expert, next_expert, w_slot, n_valid, src, h2_all, w_gate, w_up, w_down):
    n_rows = src.shape[0] - (X_SLOTS - 1) * TM_E
    nt = n_rows // TM_E
    d, de = w_gate.shape[2], w_gate.shape[3]
    grid_spec = pltpu.PrefetchScalarGridSpec(
        num_scalar_prefetch=5,
        grid=(nt,),
        in_specs=[pl.BlockSpec(memory_space=pl.ANY)] * 4,
        out_specs=pl.BlockSpec((TM_E, d), lambda i, *_: (i, 0)),
        scratch_shapes=[
            pltpu.VMEM((X_SLOTS, TM_E, d), F32),
            pltpu.SemaphoreType.DMA((X_SLOTS,)),
            pltpu.VMEM((2, d, de), F32), pltpu.VMEM((2, d, de), F32), pltpu.VMEM((2, de, d), F32),
            pltpu.SemaphoreType.DMA((2,)),
            pltpu.VMEM((d, de), BF16), pltpu.VMEM((d, de), BF16), pltpu.VMEM((de, d), BF16),
        ],
    )
    return pl.pallas_call(
        functools.partial(_experts_kernel, layer=layer),
        out_shape=jax.ShapeDtypeStruct((n_rows, d), F32),
        grid_spec=grid_spec,
        compiler_params=_cparams(("arbitrary",)),
        name="experts",
    )(tile_expert, next_expert, w_slot, n_valid, src, h2_all, w_gate, w_up, w_down)


def _combine_kernel(pos_ref, o_hbm, x1_ref, wt_ref, g2_ref, x2_ref, buf, sem):
    i = pl.program_id(0)
    nt = pl.num_programs(0)
    slot = i % 2

    def issue(tile, s):
        base = tile * (TM_C * TOP_K)
        for r in range(TM_C):
            for k in range(TOP_K):
                pltpu.make_async_copy(o_hbm.at[pl.ds(pos_ref[base + TOP_K * r + k], 1), :],
                                      buf.at[s, k, pl.ds(r, 1), :], sem.at[s]).start()

    @pl.when(i == 0)
    def _():
        issue(0, 0)

    @pl.when(i + 1 < nt)
    def _():
        issue(i + 1, 1 - slot)

    for k in range(TOP_K):
        pltpu.make_async_copy(o_hbm.at[pl.ds(0, TM_C), :], buf.at[slot, k], sem.at[slot]).wait()
    wt = wt_ref[...]
    y = wt[:, 0:1] * buf[slot, 0] + wt[:, 1:2] * buf[slot, 1]
    x2_ref[...] = x1_ref[...] + g2_ref[0] * y


def _combine(pos, o_sorted, x1, wts, mod48, mod_row):
    m, d = x1.shape
    grid_spec = pltpu.PrefetchScalarGridSpec(
        num_scalar_prefetch=1,
        grid=(m // TM_C,),
        in_specs=[
            pl.BlockSpec(memory_space=pl.ANY),
            pl.BlockSpec((TM_C, d), lambda i, p: (i, 0)),
            pl.BlockSpec((TM_C, LANES), lambda i, p: (i, 0)),
            pl.BlockSpec((1, 1, d), lambda i, p: (mod_row(i) * 6 + 5, 0, 0)),
        ],
        out_specs=pl.BlockSpec((TM_C, d), lambda i, p: (i, 0)),
        scratch_shapes=[pltpu.VMEM((2, TOP_K, TM_C, d), F32), pltpu.SemaphoreType.DMA((2,))],
    )
    return pl.pallas_call(
        _combine_kernel,
        out_shape=jax.ShapeDtypeStruct((m, d), F32),
        grid_spec=grid_spec,
        compiler_params=_cparams(("arbitrary",)),
        name="combine",
    )(pos, o_sorted, x1, wts, mod48)


def _rope_tables(n_lat, tm):
    t = jnp.arange(n_lat)
    r = (t // GRID_W).astype(F32)
    col = (t % GRID_W).astype(F32)

    def cos_sin(dim):
        nf = dim // 4
        inv = ROPE_THETA ** (-jnp.arange(nf, dtype=F32) / nf)
        ang = jnp.concatenate([r[:, None] * inv, col[:, None] * inv], axis=-1)
        return jnp.cos(ang), jnp.sin(ang)

    c64, s64 = cos_sin(A_DK)
    c128, s128 = cos_sin(B_DH)
    z32 = jnp.zeros_like(s64)
    tabs = [
        jnp.concatenate([c64, c64, c64, c64], axis=-1),
        jnp.concatenate([-s64, z32, -s64, z32], axis=-1),
        jnp.concatenate([z32, s64, z32, s64], axis=-1),
        jnp.concatenate([c128, c128], axis=-1),
        jnp.concatenate([-s128, s128], axis=-1),
    ]
    ident = [jnp.ones((tm, LANES), F32), jnp.zeros((tm, LANES), F32), jnp.zeros((tm, LANES), F32),
             jnp.ones((tm, LANES), F32), jnp.zeros((tm, LANES), F32)]
    return [jnp.concatenate([a, b], axis=0) for a, b in zip(tabs, ident)]


def _layer_params(l, w_in, w_out, a_qn, a_kn, b_qn, b_kn, c_qa_norm, c_kva_norm, c_wuq, c_wukv, c_qn, c_kn):
    w_in_bf = jnp.pad(w_in[l], ((0, 0), (0, D_IN_PAD - D_IN))).astype(BF16)
    z64 = jnp.zeros((C_ROPE,), F32)
    gains = jnp.stack([
        jnp.tile(a_qn[l], 2) * (A_DK ** -0.5),
        jnp.tile(a_kn[l], 2),
        b_qn[l] * (B_DH ** -0.5),
        b_kn[l],
        c_qn[l][:C_NOPE] * (C_DQK ** -0.5),
        jnp.concatenate([c_qn[l][C_NOPE:] * (C_DQK ** -0.5), z64]),
        c_kn[l][:C_NOPE],
        jnp.concatenate([c_kn[l][C_NOPE:], z64]),
    ])
    wq = c_wuq[l].reshape(C_Q_RANK, C_HEADS, C_DQK)
    wq = jnp.pad(wq, ((0, 0), (0, 0), (0, C_HEAD_PAD - C_DQK))).reshape(C_Q_RANK, C_HEADS * C_HEAD_PAD)
    return dict(w_in=w_in_bf, gains=gains, gcq=c_qa_norm[l][None], gckv=c_kva_norm[l][None],
                wuq=wq.astype(BF16), wukv=c_wukv[l].astype(BF16), w_out=w_out[l].astype(BF16))


def _sorted_rows(idx):
    t = idx.shape[0]
    n_pairs = t * TOP_K
    n_rows = ((n_pairs + N_EXPERTS * (TM_E - 1)) // TM_E) * TM_E
    nt = n_rows // TM_E
    flat_e = idx.reshape(-1)
    onehot = (flat_e[:, None] == jnp.arange(N_EXPERTS)[None, :]).astype(jnp.int32)
    csum = jnp.cumsum(onehot, axis=0)
    rank = jnp.take_along_axis(csum, flat_e[:, None], axis=1)[:, 0] - 1
    counts = csum[-1]
    padded = ((counts + TM_E - 1) // TM_E) * TM_E
    ends = jnp.cumsum(padded)
    pos = (ends - padded)[flat_e] + rank
    src = jnp.zeros((n_rows + (X_SLOTS - 1) * TM_E,), jnp.int32).at[pos].set(
        jnp.arange(n_pairs, dtype=jnp.int32) // TOP_K, unique_indices=True)
    tile_start = jnp.arange(nt, dtype=jnp.int32) * TM_E
    tile_expert = jnp.sum((ends[None, :] <= tile_start[:, None]).astype(jnp.int32), axis=1)
    last_used = jnp.sum((ends <= ends[-1] - 1).astype(jnp.int32))
    tile_expert = jnp.minimum(tile_expert, last_used)
    n_valid = (ends[-1:] // TM_E).astype(jnp.int32)
    used = counts > 0
    e_ids = jnp.arange(N_EXPERTS, dtype=jnp.int32)
    later = jnp.where(used[None, :] & (e_ids[None, :] > e_ids[:, None]), e_ids[None, :], N_EXPERTS)
    next_used = jnp.min(later, axis=1)
    next_used = jnp.where(next_used == N_EXPERTS, -1, next_used).astype(jnp.int32)
    parity = ((jnp.cumsum(used.astype(jnp.int32)) - 1) % 2).astype(jnp.int32)
    return (pos.astype(jnp.int32), src, tile_expert.astype(jnp.int32), next_used[tile_expert],
            parity[tile_expert], n_valid)


def kernel(x, c, ctx, c_ctx, ada_w, ada_b, norm1_g, norm2_g, w_in, w_out, a_qn, a_kn, a_lambda, a_subln,
           b_qn, b_kn, b_sink, c_qa_norm, c_kva_norm, c_wuq, c_wukv, c_qn, c_kn,
           router_w, router_bias, moe_w_gate, moe_w_up, moe_w_down):
    bsz, n_lat, d = x.shape
    n_ctx = ctx.shape[1]
    depth = ada_w.shape[0]
    t_lat, t_ctx = bsz * n_lat, bsz * n_ctx
    tm = TM_PREP
    lat_tiles = n_lat // tm
    n_lat_tiles, n_ctx_tiles = t_lat // tm, t_ctx // tm
    n_all_tiles = n_lat_tiles + n_ctx_tiles

    cond8 = jnp.concatenate([c, c_ctx[None], jnp.zeros((8 - bsz - 1, d), F32)], axis=0)
    mod = _ada_modulation(cond8, ada_w, ada_b)
    tables = _rope_tables(n_lat, tm)
    rb = router_bias[None]

    mod_row = lambda i: jnp.minimum(i // lat_tiles, bsz)
    mod_row_c = lambda i: jnp.minimum(i // (n_lat // TM_C), bsz)
    rope_blk = lambda i: jnp.where(i < n_lat_tiles, i % lat_tiles, lat_tiles)

    xa, xb, n_a_tiles = x.reshape(t_lat, d), ctx.reshape(t_ctx, d), n_lat_tiles
    lat_src, ctx_src = (0, n_lat), (t_lat, n_ctx)
    for l in range(depth):
        last = l == depth - 1
        p = _layer_params(l, w_in, w_out, a_qn, a_kn, b_qn, b_kn, c_qa_norm, c_kva_norm,
                          c_wuq, c_wukv, c_qn, c_kn)
        mod48 = mod[l].reshape(8 * 6, 1, d)
        aq, ak, av, bq, bk, bv, cq, ck, cv = _prep(
            xa, xb, n_all_tiles, n_a_tiles, mod48, mod_row, norm1_g[l][None], p["w_in"], tables, rope_blk,
            p["gains"], p["gcq"], p["gckv"], p["wuq"], p["wukv"])

        lv, gsub, sink = a_lambda[l], a_subln[l][None], b_sink[l]
        o_a = _attn_a(lv, gsub, aq, ak, av, 0, n_lat, [lat_src, ctx_src], bsz, l)
        o_b = _attn_b(sink, bq, bk, bv, bsz, n_lat, n_ctx)
        o_c = _attn_c(cq, ck, cv, 0, n_lat, [lat_src, ctx_src], bsz)
        set_a = (o_a, o_b, o_c, xa)
        if not last:
            oc_a = _attn_a(lv, gsub, aq, ak, av, t_lat, n_ctx, [ctx_src], bsz, l)
            oc_b = _attn_b_ctx(sink, bq, bk, bv, bsz, n_lat, n_ctx)
            oc_c = _attn_c(cq, ck, cv, t_lat, n_ctx, [ctx_src], bsz)
            set_b, n_tok_tiles = (oc_a, oc_b, oc_c, xb), n_all_tiles
        else:
            set_b, n_tok_tiles = set_a, n_lat_tiles
        x1, h2, idx, wts = _outproj(set_a, set_b, n_tok_tiles, min(n_a_tiles, n_tok_tiles), mod48, mod_row,
                                    norm2_g[l][None], p["w_out"], router_w, rb)

        pos, src, tile_expert, next_expert, w_slot, n_valid = _sorted_rows(idx[:, :TOP_K])
        o_sorted = _experts(l, tile_expert, next_expert, w_slot, n_valid, src, h2,
                            moe_w_gate, moe_w_up, moe_w_down)
        xa = _combine(pos, o_sorted, x1, wts, mod48, mod_row_c)
        xb, n_a_tiles = xa, n_all_tiles
    return xa.reshape(bsz, n_lat, d)
```

```python
import functools
import math

import jax
import jax.numpy as jnp
from jax import lax
from jax.experimental import pallas as pl
from jax.experimental.pallas import tpu as pltpu

F32 = jnp.float32
BF16 = jnp.bfloat16

D_MODEL = 2048
GRID_W = 64
BLOCK = 128
WINDOW = 128
ROPE_THETA = 10000.0
EPS = 1e-6
NEG_INF = -1e30
LOG2E = math.log2(math.e)
A_HEADS, A_DK = 4, 64
A_DV = 2 * A_DK
B_HEADS, B_KV_HEADS, B_DH = 8, 2, 128
B_GROUP = B_HEADS // B_KV_HEADS
C_HEADS, C_Q_RANK, C_KV_RANK, C_NOPE, C_ROPE, C_DV = 4, 512, 256, 128, 64, 128
C_DQK = C_NOPE + C_ROPE
SPLIT_SIZES = (A_HEADS * 2 * A_DK, A_HEADS * 2 * A_DK, A_HEADS * A_DV,
               B_HEADS * B_DH, B_KV_HEADS * B_DH, B_KV_HEADS * B_DH,
               C_Q_RANK, C_KV_RANK, C_ROPE)
D_IN = sum(SPLIT_SIZES)
N_EXPERTS, N_GROUPS, TOP_K = 32, 4, 2
EXPERTS_PER_GROUP = N_EXPERTS // N_GROUPS
D_EXPERT = 512

LANES = 128
V7X_VMEM_LIMIT = 56 * 1024 * 1024

D_IN_PAD = ((D_IN + LANES - 1) // LANES) * LANES
C_HEAD_PAD = 2 * LANES
TM_PREP = 256
TQ_A = 512
TQ_C = 1024
QB_B = 512
TM_E = 128
X_SLOTS = 3
ROW_CHUNKS = D_MODEL // LANES
X_PITCH = ROW_CHUNKS + 8
WEIGHT_DMA_PRIORITY = 1
TM_C = 128
ADA_TN = 1024
W_STAGE_ROWS = 128

_OFF = [0]
for _s in SPLIT_SIZES:
    _OFF.append(_OFF[-1] + _s)
O_AQ, O_AK, O_AV, O_BQ, O_BK, O_BV, O_CQ, O_CKV, O_CKR, _ = _OFF


def _cparams(sem):
    return pltpu.CompilerParams(dimension_semantics=sem, vmem_limit_bytes=V7X_VMEM_LIMIT)


def _silu(v):
    return v * (1.0 / (1.0 + jnp.exp(-v)))


def _ada_kernel(cond_ref, w_ref, b_ref, o_ref):
    s = _silu(cond_ref[...]).astype(BF16)
    o_ref[0] = jnp.dot(s, w_ref[0].astype(BF16), preferred_element_type=F32) + b_ref[0]


def _ada_modulation(cond8, ada_w, ada_b):
    depth, d, n = ada_w.shape
    return pl.pallas_call(
        _ada_kernel,
        out_shape=jax.ShapeDtypeStruct((depth, 8, n), F32),
        grid=(depth, n // ADA_TN),
        in_specs=[
            pl.BlockSpec((8, d), lambda l, j: (0, 0)),
            pl.BlockSpec((1, d, ADA_TN), lambda l, j: (l, 0, j)),
            pl.BlockSpec((1, 1, ADA_TN), lambda l, j: (l, 0, j)),
        ],
        out_specs=pl.BlockSpec((1, 8, ADA_TN), lambda l, j: (l, 0, j)),
        compiler_params=_cparams(("arbitrary", "arbitrary")),
        name="ada_modulation",
    )(cond8, ada_w, ada_b.reshape(depth, 1, n))


def _rope64(v, c, sa, sb):
    return v * c + pltpu.roll(v, 96, 1) * sa + pltpu.roll(v, 32, 1) * sb


def _rope128(v, c, s):
    return v * c + pltpu.roll(v, 64, 1) * s


def _norm_seg128(v, g):
    ms = jnp.sum(v * v, axis=-1, keepdims=True) * (1.0 / 128)
    return v * lax.rsqrt(ms + EPS) * g


def _norm_seg64x2(v, g, lo):
    sq = v * v
    s_lo = jnp.sum(jnp.where(lo, sq, 0.0), axis=-1, keepdims=True)
    s_hi = jnp.sum(jnp.where(lo, 0.0, sq), axis=-1, keepdims=True)
    ms = jnp.where(lo, s_lo, s_hi) * (1.0 / 64)
    return v * lax.rsqrt(ms + EPS) * g


def _norm_low64(v, g):
    ms = jnp.sum(v * v, axis=-1, keepdims=True) * (1.0 / 64)
    return v * lax.rsqrt(ms + EPS) * g


def _load_weight_bf16(w_hbm, layer, stage, sem, w_bf):
    k, n = w_hbm.shape[1], w_hbm.shape[2]
    ch = stage.shape[1]

    def chunk_copy(c):
        return pltpu.make_async_copy(w_hbm.at[layer, pl.ds(c * ch, ch), :], stage.at[c % 2], sem.at[c % 2])

    n_pad = w_bf.shape[1]
    if n_pad > n:
        edge = (n // LANES) * LANES
        w_bf[:, edge:n_pad] = jnp.zeros((k, n_pad - edge), BF16)
    chunk_copy(0).start()
    for c in range(k // ch):
        if c + 1 < k // ch:
            chunk_copy(c + 1).start()
        chunk_copy(c).wait()
        w_bf[c * ch:(c + 1) * ch, 0:n] = stage[c % 2].astype(BF16)


def _prep_kernel(xa_ref, xb_ref, sh_ref, sc_ref, g1_ref, w_hbm, ca_ref, saa_ref, sab_ref, cb_ref, sb_ref,
                 gains_ref, gcq_ref, gckv_ref, wuq_ref, wukv_ref,
                 aq_ref, ak_ref, av_ref, bq_ref, bk_ref, bv_ref, cq_ref, ck_ref, cv_ref,
                 w_bf, w_stage, w_sem, *, n_a_tiles, layer):
    @pl.when(pl.program_id(0) == 0)
    def _():
        _load_weight_bf16(w_hbm, layer, w_stage, w_sem, w_bf)

    x = jnp.where(pl.program_id(0) < n_a_tiles, xa_ref[...], xb_ref[...])
    ms = jnp.mean(x * x, axis=-1, keepdims=True)
    h = x * lax.rsqrt(ms + EPS) * g1_ref[...]
    h = h * (1.0 + sc_ref[0]) + sh_ref[0]
    z = jnp.dot(h.astype(BF16), w_bf[...], preferred_element_type=F32)

    lane = lax.broadcasted_iota(jnp.int32, (1, LANES), 1)
    lo = lane < 64
    ca, saa, sab = ca_ref[...], saa_ref[...], sab_ref[...]
    cb, sb = cb_ref[...], sb_ref[...]
    g_aq, g_ak, g_bq, g_bk = gains_ref[0:1], gains_ref[1:2], gains_ref[2:3], gains_ref[3:4]
    g_cqn, g_cqr, g_ckn, g_ckr = gains_ref[4:5], gains_ref[5:6], gains_ref[6:7], gains_ref[7:8]

    def blk(off, j):
        return z[:, off + j * LANES: off + (j + 1) * LANES]

    for j in range(A_HEADS):
        sl = slice(j * LANES, (j + 1) * LANES)
        aq_ref[:, sl] = _rope64(_norm_seg64x2(blk(O_AQ, j), g_aq, lo), ca, saa, sab).astype(BF16)
        ak_ref[:, sl] = _rope64(_norm_seg64x2(blk(O_AK, j), g_ak, lo), ca, saa, sab).astype(BF16)
        av_ref[:, sl] = blk(O_AV, j).astype(BF16)
    for j in range(B_HEADS):
        sl = slice(j * LANES, (j + 1) * LANES)
        bq_ref[:, sl] = _rope128(_norm_seg128(blk(O_BQ, j), g_bq), cb, sb).astype(BF16)
    for j in range(B_KV_HEADS):
        sl = slice(j * LANES, (j + 1) * LANES)
        bk_ref[:, sl] = _rope128(_norm_seg128(blk(O_BK, j), g_bk), cb, sb).astype(BF16)
        bv_ref[:, sl] = blk(O_BV, j).astype(BF16)
    cq = z[:, O_CQ:O_CQ + C_Q_RANK]
    cqn = cq * lax.rsqrt(jnp.mean(cq * cq, axis=-1, keepdims=True) + EPS) * gcq_ref[...]
    q = jnp.dot(cqn.astype(BF16), wuq_ref[...], preferred_element_type=F32)
    ckv = z[:, O_CKV:O_CKV + C_KV_RANK]
    ckvn = ckv * lax.rsqrt(jnp.mean(ckv * ckv, axis=-1, keepdims=True) + EPS) * gckv_ref[...]
    kv = jnp.dot(ckvn.astype(BF16), wukv_ref[...], preferred_element_type=F32)
    krope = _rope64(_norm_low64(z[:, O_CKR:O_CKR + LANES], g_ckr), ca, saa, sab).astype(BF16)
    for hh in range(C_HEADS):
        b0 = hh * C_HEAD_PAD
        cq_ref[:, b0:b0 + LANES] = _norm_seg128(q[:, b0:b0 + LANES], g_cqn).astype(BF16)
        cq_ref[:, b0 + LANES:b0 + 2 * LANES] = _rope64(
            _norm_low64(q[:, b0 + LANES:b0 + 2 * LANES], g_cqr), ca, saa, sab).astype(BF16)
        ck_ref[:, b0:b0 + LANES] = _norm_seg128(kv[:, b0:b0 + LANES], g_ckn).astype(BF16)
        ck_ref[:, b0 + LANES:b0 + 2 * LANES] = krope
        cv_ref[:, hh * LANES:(hh + 1) * LANES] = kv[:, b0 + LANES:b0 + 2 * LANES].astype(BF16)


def _split_rows(n_a_tiles):
    first = lambda i: (jnp.minimum(i, n_a_tiles - 1), 0)
    second = lambda i: (jnp.maximum(i - n_a_tiles, 0), 0)
    return first, second


def _prep(layer, xa, xb, n_tiles, n_a_tiles, mod48, mod_row, g1, w_in, tables, rope_blk, gains, gcq, gckv,
          wuq_bf, wukv_bf):
    d = xa.shape[1]
    tm = TM_PREP
    m = n_tiles * tm
    row = lambda i: (i, 0)
    const = lambda i: (0, 0)
    first, second = _split_rows(n_a_tiles)
    tab_spec = pl.BlockSpec((tm, LANES), lambda i: (rope_blk(i), 0))
    widths = (512, 512, 512, 1024, 256, 256, C_HEADS * C_HEAD_PAD, C_HEADS * C_HEAD_PAD, 512)
    return pl.pallas_call(
        functools.partial(_prep_kernel, n_a_tiles=n_a_tiles, layer=layer),
        out_shape=[jax.ShapeDtypeStruct((m, w), BF16) for w in widths],
        grid=(n_tiles,),
        in_specs=[
            pl.BlockSpec((tm, d), first),
            pl.BlockSpec((tm, d), second),
            pl.BlockSpec((1, 1, d), lambda i: (mod_row(i) * 6 + 0, 0, 0)),
            pl.BlockSpec((1, 1, d), lambda i: (mod_row(i) * 6 + 1, 0, 0)),
            pl.BlockSpec((1, d), const),
            pl.BlockSpec(memory_space=pl.ANY),
            tab_spec, tab_spec, tab_spec, tab_spec, tab_spec,
            pl.BlockSpec((8, LANES), const),
            pl.BlockSpec((1, C_Q_RANK), const),
            pl.BlockSpec((1, C_KV_RANK), const),
            pl.BlockSpec((C_Q_RANK, C_HEADS * C_HEAD_PAD), const, pipeline_mode=pl.Buffered(1)),
            pl.BlockSpec((C_KV_RANK, C_HEADS * C_HEAD_PAD), const, pipeline_mode=pl.Buffered(1)),
        ],
        out_specs=[pl.BlockSpec((tm, w), row) for w in widths],
        scratch_shapes=[pltpu.VMEM((d, D_IN_PAD), BF16),
                        pltpu.VMEM((2, W_STAGE_ROWS, w_in.shape[2]), F32),
                        pltpu.SemaphoreType.DMA((2,))],
        compiler_params=_cparams(("arbitrary",)),
        name="prep",
    )(xa, xb, mod48, mod48, g1, w_in, *tables, gains, gcq, gckv, wuq_bf, wukv_bf)


def _dot_nt(a, b):
    return lax.dot_general(a, b, (((1,), (1,)), ((), ())), preferred_element_type=F32)


def _dot_tn(a, b):
    return lax.dot_general(a, b, (((0,), (0,)), ((), ())), preferred_element_type=F32)


def _softmax_pv_t(q, k_refs, v_refs):
    s = [_dot_nt(k[...], q) for k in k_refs]
    m = functools.reduce(jnp.maximum, [jnp.max(si, axis=0, keepdims=True) for si in s])
    e = [jnp.exp2(si - m) for si in s]
    l = functools.reduce(jnp.add, [jnp.sum(ei, axis=0, keepdims=True) for ei in e])
    o = functools.reduce(jnp.add, [_dot_tn(v[...], ei.astype(BF16)) for ei, v in zip(e, v_refs)])
    return o * (1.0 / l)


def _attn_a_kernel(*refs, n_src, lam_init):
    lv_ref, gsub_ref, q_ref = refs[0], refs[1], refs[2]
    k_refs = refs[3:3 + n_src]
    v_refs = refs[3 + n_src:3 + 2 * n_src]
    o_ref = refs[3 + 2 * n_src]
    tq = q_ref.shape[0]
    lv = lv_ref[...]
    lam = (jnp.exp(jnp.sum(lv[0:1] * lv[1:2], axis=-1, keepdims=True))
           - jnp.exp(jnp.sum(lv[2:3] * lv[3:4], axis=-1, keepdims=True)) + lam_init)
    q = q_ref[...]
    lo = lax.broadcasted_iota(jnp.int32, (1, LANES), 1) < 64
    zero = jnp.zeros_like(q)
    qq = jnp.concatenate([jnp.where(lo, q, zero), jnp.where(lo, zero, q)], axis=0)
    o2 = _softmax_pv_t(qq, k_refs, v_refs)
    o = o2[:, :tq] - lam * o2[:, tq:]
    ms = jnp.mean(o * o, axis=0, keepdims=True)
    o = (o * lax.rsqrt(ms + EPS)).T
    o_ref[...] = (o * gsub_ref[...] * (1.0 - lam_init)).astype(BF16)


def _attn_a(lv, gsub, aq, ak, av, q_row0, q_rows, srcs, n_batch, layer_idx):
    tq = min(TQ_A, q_rows)
    nq = q_rows // tq
    qb0 = q_row0 // tq
    lam_init = 0.8 - 0.6 * math.exp(-0.3 * layer_idx)
    in_specs = [
        pl.BlockSpec((4, A_DK), lambda b, h, i: (0, 0)),
        pl.BlockSpec((1, A_DV), lambda b, h, i: (0, 0)),
        pl.BlockSpec((tq, LANES), lambda b, h, i: (qb0 + b * nq + i, h)),
    ]
    kv_specs = [pl.BlockSpec((rows, LANES), lambda b, h, i, blk0=row0 // rows: (blk0 + b, h))
                for row0, rows in srcs]
    return pl.pallas_call(
        functools.partial(_attn_a_kernel, n_src=len(srcs), lam_init=lam_init),
        out_shape=jax.ShapeDtypeStruct((n_batch * q_rows, A_HEADS * A_DV), BF16),
        grid=(n_batch, A_HEADS, nq),
        in_specs=in_specs + kv_specs + kv_specs,
        out_specs=pl.BlockSpec((tq, LANES), lambda b, h, i: (b * nq + i, h)),
        compiler_params=_cparams(("arbitrary", "arbitrary", "arbitrary")),
        name="attn_a",
    )(lv, gsub, aq, *([ak] * len(srcs)), *([av] * len(srcs)))


def _attn_c_kernel(*refs, n_src):
    q_ref = refs[0]
    k_refs = refs[1:1 + n_src]
    v_refs = refs[1 + n_src:1 + 2 * n_src]
    o_ref = refs[1 + 2 * n_src]
    o_ref[...] = _softmax_pv_t(q_ref[...], k_refs, v_refs).T.astype(BF16)


def _attn_c(cq, ck, cv, q_row0, q_rows, srcs, n_batch):
    tq = min(TQ_C, q_rows)
    nq = q_rows // tq
    qb0 = q_row0 // tq
    in_specs = [pl.BlockSpec((tq, C_HEAD_PAD), lambda b, h, i: (qb0 + b * nq + i, h))]
    k_specs = [pl.BlockSpec((rows, C_HEAD_PAD), lambda b, h, i, blk0=row0 // rows: (blk0 + b, h))
               for row0, rows in srcs]
    v_specs = [pl.BlockSpec((rows, C_DV), lambda b, h, i, blk0=row0 // rows: (blk0 + b, h))
               for row0, rows in srcs]
    return pl.pallas_call(
        functools.partial(_attn_c_kernel, n_src=len(srcs)),
        out_shape=jax.ShapeDtypeStruct((n_batch * q_rows, C_HEADS * C_DV), BF16),
        grid=(n_batch, C_HEADS, nq),
        in_specs=in_specs + k_specs + v_specs,
        out_specs=pl.BlockSpec((tq, C_DV), lambda b, h, i: (b * nq + i, h)),
        compiler_params=_cparams(("arbitrary", "arbitrary", "arbitrary")),
        name="attn_c",
    )(cq, *([ck] * len(srcs)), *([cv] * len(srcs)))


def _stack_heads(q):
    return jnp.concatenate([q[:, g * LANES:(g + 1) * LANES] for g in range(B_GROUP)], axis=0)


def _sink_row(sink_ref, kvh, cols):
    return jnp.concatenate(
        [jnp.full((1, cols), sink_ref[kvh * B_GROUP + g] * LOG2E, F32) for g in range(B_GROUP)], axis=1)


def _attn_b_kernel(sink_ref, q_ref, kp_ref, km_ref, kn_ref, vp_ref, vm_ref, vn_ref, kc_ref, vc_ref, o_ref):
    kvh = pl.program_id(1)
    qb = pl.program_id(2)
    nqb = pl.num_programs(2)
    n_blk = QB_B // BLOCK
    kband = jnp.concatenate([kp_ref[...], km_ref[...], kn_ref[...]], axis=0)
    vband = jnp.concatenate([vp_ref[...], vm_ref[...], vn_ref[...]], axis=0)
    kc, vc = kc_ref[...], vc_ref[...]
    sink = _sink_row(sink_ref, kvh, BLOCK)
    c = lax.broadcasted_iota(jnp.int32, (3 * BLOCK, B_GROUP * BLOCK), 0)
    r = lax.broadcasted_iota(jnp.int32, (3 * BLOCK, B_GROUP * BLOCK), 1) % BLOCK
    cr = c - r
    band_ok = (cr >= 0) & (cr <= BLOCK + WINDOW)
    for j in range(n_blk):
        q4 = _stack_heads(q_ref[j * BLOCK:(j + 1) * BLOCK, :])
        s_loc = _dot_nt(kband[j * BLOCK:(j + 3) * BLOCK], q4)
        valid = band_ok
        if j == 0:
            valid = valid & (c >= jnp.where(qb > 0, 0, BLOCK))
        if j == n_blk - 1:
            valid = valid & (c < jnp.where(qb < nqb - 1, 3 * BLOCK, 2 * BLOCK))
        s_loc = jnp.where(valid, s_loc, NEG_INF)
        s_ctx = _dot_nt(kc, q4)
        m = jnp.maximum(jnp.maximum(jnp.max(s_loc, axis=0, keepdims=True),
                                    jnp.max(s_ctx, axis=0, keepdims=True)), sink)
        e_loc, e_ctx = jnp.exp2(s_loc - m), jnp.exp2(s_ctx - m)
        l = (jnp.sum(e_loc, axis=0, keepdims=True) + jnp.sum(e_ctx, axis=0, keepdims=True)
             + jnp.exp2(sink - m))
        o = (_dot_tn(vband[j * BLOCK:(j + 3) * BLOCK], e_loc.astype(BF16))
             + _dot_tn(vc, e_ctx.astype(BF16))) * (1.0 / l)
        o = o.T
        for g in range(B_GROUP):
            o_ref[j * BLOCK:(j + 1) * BLOCK, g * LANES:(g + 1) * LANES] = (
                o[g * BLOCK:(g + 1) * BLOCK].astype(BF16))


def _attn_b(sink, bq, bk, bv, n_batch, n_lat, n_ctx):
    nqb = n_lat // QB_B
    per = QB_B // BLOCK
    blocks_per_batch = n_lat // BLOCK
    ctx_blk0 = n_batch * n_lat // n_ctx
    gw = B_GROUP * B_DH
    prev = lambda b, h, i: (b * blocks_per_batch + jnp.maximum(i * per - 1, 0), h)
    main = lambda b, h, i: (b * nqb + i, h)
    nxt = lambda b, h, i: (b * blocks_per_batch + jnp.minimum(i * per + per, blocks_per_batch - 1), h)
    ctx = lambda b, h, i: (ctx_blk0 + b, h)
    return pl.pallas_call(
        _attn_b_kernel,
        out_shape=jax.ShapeDtypeStruct((n_batch * n_lat, B_HEADS * B_DH), BF16),
        grid=(n_batch, B_KV_HEADS, nqb),
        in_specs=[
            pl.BlockSpec(memory_space=pltpu.SMEM),
            pl.BlockSpec((QB_B, gw), main),
            pl.BlockSpec((BLOCK, B_DH), prev), pl.BlockSpec((QB_B, B_DH), main), pl.BlockSpec((BLOCK, B_DH), nxt),
            pl.BlockSpec((BLOCK, B_DH), prev), pl.BlockSpec((QB_B, B_DH), main), pl.BlockSpec((BLOCK, B_DH), nxt),
            pl.BlockSpec((n_ctx, B_DH), ctx), pl.BlockSpec((n_ctx, B_DH), ctx),
        ],
        out_specs=pl.BlockSpec((QB_B, gw), main),
        compiler_params=_cparams(("arbitrary", "arbitrary", "arbitrary")),
        name="attn_b",
    )(sink, bq, bk, bk, bk, bv, bv, bv, bk, bv)


def _attn_b_ctx_kernel(sink_ref, q_ref, k_ref, v_ref, o_ref):
    kvh = pl.program_id(1)
    rows = q_ref.shape[0]
    q4 = _stack_heads(q_ref[...])
    sink = _sink_row(sink_ref, kvh, rows)
    s = _dot_nt(k_ref[...], q4)
    m = jnp.maximum(jnp.max(s, axis=0, keepdims=True), sink)
    e = jnp.exp2(s - m)
    l = jnp.sum(e, axis=0, keepdims=True) + jnp.exp2(sink - m)
    o = (_dot_tn(v_ref[...], e.astype(BF16)) * (1.0 / l)).T
    for g in range(B_GROUP):
        o_ref[:, g * LANES:(g + 1) * LANES] = o[g * rows:(g + 1) * rows].astype(BF16)


def _attn_b_ctx(sink, bq, bk, bv, n_batch, n_lat, n_ctx):
    gw = B_GROUP * B_DH
    ctx_blk0 = n_batch * n_lat // n_ctx
    return pl.pallas_call(
        _attn_b_ctx_kernel,
        out_shape=jax.ShapeDtypeStruct((n_batch * n_ctx, B_HEADS * B_DH), BF16),
        grid=(n_batch, B_KV_HEADS),
        in_specs=[
            pl.BlockSpec(memory_space=pltpu.SMEM),
            pl.BlockSpec((n_ctx, gw), lambda b, h: (ctx_blk0 + b, h)),
            pl.BlockSpec((n_ctx, B_DH), lambda b, h: (ctx_blk0 + b, h)),
            pl.BlockSpec((n_ctx, B_DH), lambda b, h: (ctx_blk0 + b, h)),
        ],
        out_specs=pl.BlockSpec((n_ctx, gw), lambda b, h: (b, h)),
        compiler_params=_cparams(("arbitrary", "arbitrary")),
        name="attn_b_ctx",
    )(sink, bq, bk, bv)


def _route(h2, rw_ref, rb_ref):
    tm = h2.shape[0]
    h_hi = h2.astype(BF16)
    h_lo = (h2 - h_hi.astype(F32)).astype(BF16)
    p = jnp.dot(jnp.concatenate([h_hi, h_lo], axis=0), rw_ref[...], preferred_element_type=F32)
    p = p[:tm] + p[tm:]
    logits = p[:, :N_EXPERTS] + p[:, N_EXPERTS:]
    scores = 1.0 / (1.0 + jnp.exp(-logits))
    sel = scores + rb_ref[...]
    lane_i = lax.broadcasted_iota(jnp.int32, sel.shape, 1)
    lane = lane_i.astype(F32)
    big = float(N_EXPERTS)

    def top2(mask):
        v = jnp.where(mask, sel, -jnp.inf)
        m1 = jnp.max(v, axis=-1, keepdims=True)
        i1 = jnp.min(jnp.where(v == m1, lane, big), axis=-1, keepdims=True)
        v2 = jnp.where(lane == i1, -jnp.inf, v)
        m2 = jnp.max(v2, axis=-1, keepdims=True)
        i2 = jnp.min(jnp.where(v2 == m2, lane, big), axis=-1, keepdims=True)
        return m1, i1, m2, i2

    best = None
    for g in range(N_GROUPS):
        m1, i1, m2, i2 = top2((lane_i >= g * EXPERTS_PER_GROUP) & (lane_i < (g + 1) * EXPERTS_PER_GROUP))
        gs = m1 + m2
        if best is None:
            best = (gs, i1, i2)
        else:
            take = gs > best[0]
            best = (jnp.where(take, gs, best[0]), jnp.where(take, i1, best[1]), jnp.where(take, i2, best[2]))
    _, e1, e2 = best
    w1 = jnp.sum(jnp.where(lane == e1, scores, 0.0), axis=-1, keepdims=True)
    w2 = jnp.sum(jnp.where(lane == e2, scores, 0.0), axis=-1, keepdims=True)
    tot = w1 + w2
    return e1, e2, w1 / tot, w2 / tot


def _outproj_kernel(oa1_ref, ob1_ref, oc1_ref, x1in_ref, oa2_ref, ob2_ref, oc2_ref, x2in_ref,
                    g1_ref, sh2_ref, sc2_ref, n2_ref, w_hbm, rw_ref, rb_ref,
                    x1_ref, h2_ref, idx_ref, wt_ref, w_ref, w_stage, w_sem, *, n_a_tiles, layer):
    @pl.when(pl.program_id(0) == 0)
    def _():
        _load_weight_bf16(w_hbm, layer, w_stage, w_sem, w_ref)

    first = pl.program_id(0) < n_a_tiles
    oa = jnp.where(first, oa1_ref[...], oa2_ref[...])
    ob = jnp.where(first, ob1_ref[...], ob2_ref[...])
    oc = jnp.where(first, oc1_ref[...], oc2_ref[...])
    x = jnp.where(first, x1in_ref[...], x2in_ref[...])
    y = jnp.dot(jnp.concatenate([oa, ob, oc], axis=1), w_ref[...], preferred_element_type=F32)
    x1 = x + g1_ref[0] * y
    x1_ref[...] = x1
    ms = jnp.mean(x1 * x1, axis=-1, keepdims=True)
    h2 = x1 * lax.rsqrt(ms + EPS) * n2_ref[...]
    h2 = h2 * (1.0 + sc2_ref[0]) + sh2_ref[0]
    tm = h2.shape[0]
    for cc in range(ROW_CHUNKS):
        h2_ref[pl.ds(cc, tm, stride=ROW_CHUNKS), :] = h2[:, cc * LANES:(cc + 1) * LANES]
    e1, e2, w1, w2 = _route(h2, rw_ref, rb_ref)
    lane = lax.broadcasted_iota(jnp.int32, idx_ref.shape, 1)
    idx_ref[...] = jnp.where(lane == 0, e1, jnp.where(lane == 1, e2, 0.0)).astype(jnp.int32)
    wt_ref[...] = jnp.where(lane == 0, w1, jnp.where(lane == 1, w2, 0.0))


def _outproj(layer, set_a, set_b, n_tiles, n_a_tiles, mod48, mod_row, n2, w_out, router_w, router_b):
    d = set_a[3].shape[1]
    tm = TM_PREP
    m = n_tiles * tm
    row = lambda i: (i, 0)
    const = lambda i: (0, 0)
    first, second = _split_rows(n_a_tiles)
    mod_spec = lambda j: pl.BlockSpec((1, 1, d), lambda i: (mod_row(i) * 6 + j, 0, 0))
    in_specs = (
        [pl.BlockSpec((tm, a.shape[1]), first) for a in set_a]
        + [pl.BlockSpec((tm, a.shape[1]), second) for a in set_b]
        + [mod_spec(2), mod_spec(3), mod_spec(4),
           pl.BlockSpec((1, d), const),
           pl.BlockSpec(memory_space=pl.ANY),
           pl.BlockSpec((d, 2 * N_EXPERTS), const),
           pl.BlockSpec((1, N_EXPERTS), const)])
    return pl.pallas_call(
        functools.partial(_outproj_kernel, n_a_tiles=n_a_tiles, layer=layer),
        out_shape=[jax.ShapeDtypeStruct((m, d), F32), jax.ShapeDtypeStruct((m * ROW_CHUNKS, LANES), F32),
                   jax.ShapeDtypeStruct((m, LANES), jnp.int32), jax.ShapeDtypeStruct((m, LANES), F32)],
        grid=(n_tiles,),
        in_specs=in_specs,
        out_specs=[pl.BlockSpec((tm, d), row), pl.BlockSpec((tm * ROW_CHUNKS, LANES), row),
                   pl.BlockSpec((tm, LANES), row), pl.BlockSpec((tm, LANES), row)],
        scratch_shapes=[pltpu.VMEM((d, d), BF16),
                        pltpu.VMEM((2, W_STAGE_ROWS, d), F32),
                        pltpu.SemaphoreType.DMA((2,))],
        compiler_params=_cparams(("arbitrary",)),
        name="outproj",
    )(*set_a, *set_b, mod48, mod48, mod48, n2, w_out, router_w, router_b)


def _experts_kernel(te_ref, nxt_ref, ws_ref, nv_ref, src_ref, h2_hbm, wg_hbm, wu_hbm, wd_hbm, o_ref,
                    xbuf, xsem, wg_f, wu_f, wd_f, wsem, wg_bf, wu_bf, wd_bf, *, layer):
    i = pl.program_id(0)
    n_valid = nv_ref[0]
    slot = i % X_SLOTS

    def issue_rows(tile):
        s = tile % X_SLOTS
        base = tile * TM_E
        for r in range(TM_E):
            row0 = pl.multiple_of(src_ref[base + r] * ROW_CHUNKS, ROW_CHUNKS)
            pltpu.make_async_copy(h2_hbm.at[pl.ds(row0, ROW_CHUNKS), :],
                                  xbuf.at[s, pl.ds(r * X_PITCH, ROW_CHUNKS), :], xsem.at[s]).start()

    def wait_rows(s):
        pltpu.make_async_copy(h2_hbm.at[pl.ds(0, TM_E * ROW_CHUNKS), :],
                              xbuf.at[s, pl.ds(0, TM_E * ROW_CHUNKS), :], xsem.at[s]).wait()

    def weight_copies(e, s):
        return (pltpu.make_async_copy(wg_hbm.at[layer, e], wg_f.at[s], wsem.at[s]),
                pltpu.make_async_copy(wu_hbm.at[layer, e], wu_f.at[s], wsem.at[s]),
                pltpu.make_async_copy(wd_hbm.at[layer, e], wd_f.at[s], wsem.at[s]))

    @pl.when(i == 0)
    def _():
        for cp in weight_copies(te_ref[0], ws_ref[0]):
            cp.start()
        for t in range(X_SLOTS - 1):
            issue_rows(t)

    first_of_expert = jnp.logical_or(i == 0, te_ref[i] != te_ref[jnp.maximum(i - 1, 0)])

    @pl.when(jnp.logical_and(first_of_expert, i < n_valid))
    def _():
        ws = ws_ref[i]
        for cp in weight_copies(te_ref[i], ws):
            cp.wait()

        @pl.when(nxt_ref[i] >= 0)
        def _():
            for cp in weight_copies(nxt_ref[i], 1 - ws):
                cp.start(priority=WEIGHT_DMA_PRIORITY)

        wg_bf[...] = wg_f[ws].astype(BF16)
        wu_bf[...] = wu_f[ws].astype(BF16)
        wd_bf[...] = wd_f[ws].astype(BF16)

    @pl.when(i < n_valid)
    def _():
        wait_rows(slot)
        x = jnp.concatenate([xbuf[slot, pl.ds(cc, TM_E, stride=X_PITCH), :] for cc in range(ROW_CHUNKS)],
                            axis=1).astype(BF16)
        issue_rows(i + X_SLOTS - 1)
        g = jnp.dot(x, wg_bf[...], preferred_element_type=F32)
        u = jnp.dot(x, wu_bf[...], preferred_element_type=F32)
        a = (_silu(g) * u).astype(BF16)
        y = jnp.dot(a, wd_bf[...], preferred_element_type=F32)
        for cc in range(ROW_CHUNKS):
            o_ref[pl.ds(cc, TM_E, stride=ROW_CHUNKS), :] = y[:, cc * LANES:(cc + 1) * LANES]

    @pl.when(i == n_valid - 1)
    def _():
        for t in range(1, X_SLOTS):
            wait_rows((i + t) % X_SLOTS)

    @pl.when(i >= n_valid)
    def _():
        o_ref[...] = jnp.zeros_like(o_ref)


def _experts(layer, tile_expert, next_expert, w_slot, n_valid, src, h2_all, w_gate, w_up, w_down):
    n_rows = src.shape[0] - (X_SLOTS - 1) * TM_E
    nt = n_rows // TM_E
    d, de = w_gate.shape[2], w_gate.shape[3]
    grid_spec = pltpu.PrefetchScalarGridSpec(
        num_scalar_prefetch=5,
        grid=(nt,),
        in_specs=[pl.BlockSpec(memory_space=pl.ANY)] * 4,
        out_specs=pl.BlockSpec((TM_E * ROW_CHUNKS, LANES), lambda i, *_: (i, 0)),
        scratch_shapes=[
            pltpu.VMEM((X_SLOTS, TM_E * X_PITCH, LANES), F32),
            pltpu.SemaphoreType.DMA((X_SLOTS,)),
            pltpu.VMEM((2, d, de), F32), pltpu.VMEM((2, d, de), F32), pltpu.VMEM((2, de, d), F32),
            pltpu.SemaphoreType.DMA((2,)),
            pltpu.VMEM((d, de), BF16), pltpu.VMEM((d, de), BF16), pltpu.VMEM((de, d), BF16),
        ],
    )
    return pl.pallas_call(
        functools.partial(_experts_kernel, layer=layer),
        out_shape=jax.ShapeDtypeStruct((n_rows * ROW_CHUNKS, LANES), F32),
        grid_spec=grid_spec,
        compiler_params=_cparams(("arbitrary",)),
        name="experts",
    )(tile_expert, next_expert, w_slot, n_valid, src, h2_all, w_gate, w_up, w_down)


def _combine_kernel(pos_ref, o_hbm, x1_ref, wt_ref, g2_ref, x2_ref, buf, sem):
    i = pl.program_id(0)
    nt = pl.num_programs(0)
    slot = i % 2

    def issue(tile, s):
        base = tile * (TM_C * TOP_K)
        for r in range(TM_C):
            for k in range(TOP_K):
                row0 = pl.multiple_of(pos_ref[base + TOP_K * r + k] * ROW_CHUNKS, ROW_CHUNKS)
                pltpu.make_async_copy(o_hbm.at[pl.ds(row0, ROW_CHUNKS), :],
                                      buf.at[s, k, pl.ds(r * X_PITCH, ROW_CHUNKS), :], sem.at[s]).start()

    @pl.when(i == 0)
    def _():
        issue(0, 0)

    @pl.when(i + 1 < nt)
    def _():
        issue(i + 1, 1 - slot)

    def expert_rows(k):
        return jnp.concatenate([buf[slot, k, pl.ds(cc, TM_C, stride=X_PITCH), :] for cc in range(ROW_CHUNKS)],
                               axis=1)

    for k in range(TOP_K):
        pltpu.make_async_copy(o_hbm.at[pl.ds(0, TM_C * ROW_CHUNKS), :],
                              buf.at[slot, k, pl.ds(0, TM_C * ROW_CHUNKS), :], sem.at[slot]).wait()
    wt = wt_ref[...]
    y = wt[:, 0:1] * expert_rows(0) + wt[:, 1:2] * expert_rows(1)
    x2_ref[...] = x1_ref[...] + g2_ref[0] * y


def _combine(pos, o_sorted, x1, wts, mod48, mod_row):
    m, d = x1.shape
    grid_spec = pltpu.PrefetchScalarGridSpec(
        num_scalar_prefetch=1,
        grid=(m // TM_C,),
        in_specs=[
            pl.BlockSpec(memory_space=pl.ANY),
            pl.BlockSpec((TM_C, d), lambda i, p: (i, 0)),
            pl.BlockSpec((TM_C, LANES), lambda i, p: (i, 0)),
            pl.BlockSpec((1, 1, d), lambda i, p: (mod_row(i) * 6 + 5, 0, 0)),
        ],
        out_specs=pl.BlockSpec((TM_C, d), lambda i, p: (i, 0)),
        scratch_shapes=[pltpu.VMEM((2, TOP_K, TM_C * X_PITCH, LANES), F32), pltpu.SemaphoreType.DMA((2,))],
    )
    return pl.pallas_call(
        _combine_kernel,
        out_shape=jax.ShapeDtypeStruct((m, d), F32),
        grid_spec=grid_spec,
        compiler_params=_cparams(("arbitrary",)),
        name="combine",
    )(pos, o_sorted, x1, wts, mod48)


def _rope_tables(n_lat, tm):
    t = jnp.arange(n_lat)
    r = (t // GRID_W).astype(F32)
    col = (t % GRID_W).astype(F32)

    def cos_sin(dim):
        nf = dim // 4
        inv = ROPE_THETA ** (-jnp.arange(nf, dtype=F32) / nf)
        ang = jnp.concatenate([r[:, None] * inv, col[:, None] * inv], axis=-1)
        return jnp.cos(ang), jnp.sin(ang)

    c64, s64 = cos_sin(A_DK)
    c128, s128 = cos_sin(B_DH)
    z32 = jnp.zeros_like(s64)
    tabs = [
        jnp.concatenate([c64, c64, c64, c64], axis=-1),
        jnp.concatenate([-s64, z32, -s64, z32], axis=-1),
        jnp.concatenate([z32, s64, z32, s64], axis=-1),
        jnp.concatenate([c128, c128], axis=-1),
        jnp.concatenate([-s128, s128], axis=-1),
    ]
    ident = [jnp.ones((tm, LANES), F32), jnp.zeros((tm, LANES), F32), jnp.zeros((tm, LANES), F32),
             jnp.ones((tm, LANES), F32), jnp.zeros((tm, LANES), F32)]
    return [jnp.concatenate([a, b], axis=0) for a, b in zip(tabs, ident)]


def _layer_params(l, a_qn, a_kn, b_qn, b_kn, c_qa_norm, c_kva_norm, c_wuq, c_wukv, c_qn, c_kn):
    z64 = jnp.zeros((C_ROPE,), F32)
    gains = jnp.stack([
        jnp.tile(a_qn[l], 2) * (A_DK ** -0.5 * LOG2E),
        jnp.tile(a_kn[l], 2),
        b_qn[l] * (B_DH ** -0.5 * LOG2E),
        b_kn[l],
        c_qn[l][:C_NOPE] * (C_DQK ** -0.5 * LOG2E),
        jnp.concatenate([c_qn[l][C_NOPE:] * (C_DQK ** -0.5 * LOG2E), z64]),
        c_kn[l][:C_NOPE],
        jnp.concatenate([c_kn[l][C_NOPE:], z64]),
    ])
    wq = c_wuq[l].reshape(C_Q_RANK, C_HEADS, C_DQK)
    wq = jnp.pad(wq, ((0, 0), (0, 0), (0, C_HEAD_PAD - C_DQK))).reshape(C_Q_RANK, C_HEADS * C_HEAD_PAD)
    return dict(gains=gains, gcq=c_qa_norm[l][None], gckv=c_kva_norm[l][None],
                wuq=wq.astype(BF16), wukv=c_wukv[l].astype(BF16))


def _sorted_rows(idx):
    t = idx.shape[0]
    n_pairs = t * TOP_K
    n_rows = ((n_pairs + N_EXPERTS * (TM_E - 1)) // TM_E) * TM_E
    nt = n_rows // TM_E
    flat_e = idx.reshape(-1)
    onehot = (flat_e[:, None] == jnp.arange(N_EXPERTS)[None, :]).astype(jnp.int32)
    csum = jnp.cumsum(onehot, axis=0)
    rank = jnp.take_along_axis(csum, flat_e[:, None], axis=1)[:, 0] - 1
    counts = csum[-1]
    padded = ((counts + TM_E - 1) // TM_E) * TM_E
    ends = jnp.cumsum(padded)
    pos = (ends - padded)[flat_e] + rank
    src = jnp.zeros((n_rows + (X_SLOTS - 1) * TM_E,), jnp.int32).at[pos].set(
        jnp.arange(n_pairs, dtype=jnp.int32) // TOP_K, unique_indices=True)
    tile_start = jnp.arange(nt, dtype=jnp.int32) * TM_E
    tile_expert = jnp.sum((ends[None, :] <= tile_start[:, None]).astype(jnp.int32), axis=1)
    last_used = jnp.sum((ends <= ends[-1] - 1).astype(jnp.int32))
    tile_expert = jnp.minimum(tile_expert, last_used)
    n_valid = (ends[-1:] // TM_E).astype(jnp.int32)
    used = counts > 0
    e_ids = jnp.arange(N_EXPERTS, dtype=jnp.int32)
    later = jnp.where(used[None, :] & (e_ids[None, :] > e_ids[:, None]), e_ids[None, :], N_EXPERTS)
    next_used = jnp.min(later, axis=1)
    next_used = jnp.where(next_used == N_EXPERTS, -1, next_used).astype(jnp.int32)
    parity = ((jnp.cumsum(used.astype(jnp.int32)) - 1) % 2).astype(jnp.int32)
    return (pos.astype(jnp.int32), src, tile_expert.astype(jnp.int32), next_used[tile_expert],
            parity[tile_expert], n_valid)


def kernel(x, c, ctx, c_ctx, ada_w, ada_b, norm1_g, norm2_g, w_in, w_out, a_qn, a_kn, a_lambda, a_subln,
           b_qn, b_kn, b_sink, c_qa_norm, c_kva_norm, c_wuq, c_wukv, c_qn, c_kn,
           router_w, router_bias, moe_w_gate, moe_w_up, moe_w_down):
    bsz, n_lat, d = x.shape
    n_ctx = ctx.shape[1]
    depth = ada_w.shape[0]
    t_lat, t_ctx = bsz * n_lat, bsz * n_ctx
    tm = TM_PREP
    lat_tiles = n_lat // tm
    n_lat_tiles, n_ctx_tiles = t_lat // tm, t_ctx // tm
    n_all_tiles = n_lat_tiles + n_ctx_tiles

    cond8 = jnp.concatenate([c, c_ctx[None], jnp.zeros((8 - bsz - 1, d), F32)], axis=0)
    mod = _ada_modulation(cond8, ada_w, ada_b)
    tables = _rope_tables(n_lat, tm)
    rb = router_bias[None]
    rw_hi = router_w.astype(BF16)
    rw_lo = (router_w - rw_hi.astype(F32)).astype(BF16)
    rw2 = jnp.concatenate([rw_hi, rw_lo], axis=1)

    mod_row = lambda i: jnp.minimum(i // lat_tiles, bsz)
    mod_row_c = lambda i: jnp.minimum(i // (n_lat // TM_C), bsz)
    rope_blk = lambda i: jnp.where(i < n_lat_tiles, i % lat_tiles, lat_tiles)

    xa, xb, n_a_tiles = x.reshape(t_lat, d), ctx.reshape(t_ctx, d), n_lat_tiles
    lat_src, ctx_src = (0, n_lat), (t_lat, n_ctx)
    for l in range(depth):
        last = l == depth - 1
        p = _layer_params(l, a_qn, a_kn, b_qn, b_kn, c_qa_norm, c_kva_norm, c_wuq, c_wukv, c_qn, c_kn)
        mod48 = mod[l].reshape(8 * 6, 1, d)
        aq, ak, av, bq, bk, bv, cq, ck, cv = _prep(
            l, xa, xb, n_all_tiles, n_a_tiles, mod48, mod_row, norm1_g[l][None], w_in, tables, rope_blk,
            p["gains"], p["gcq"], p["gckv"], p["wuq"], p["wukv"])

        lv, gsub, sink = a_lambda[l], a_subln[l][None], b_sink[l]
        o_a = _attn_a(lv, gsub, aq, ak, av, 0, n_lat, [lat_src, ctx_src], bsz, l)
        o_b = _attn_b(sink, bq, bk, bv, bsz, n_lat, n_ctx)
        o_c = _attn_c(cq, ck, cv, 0, n_lat, [lat_src, ctx_src], bsz)
        set_a = (o_a, o_b, o_c, xa)
        if not last:
            oc_a = _attn_a(lv, gsub, aq, ak, av, t_lat, n_ctx, [ctx_src], bsz, l)
            oc_b = _attn_b_ctx(sink, bq, bk, bv, bsz, n_lat, n_ctx)
            oc_c = _attn_c(cq, ck, cv, t_lat, n_ctx, [ctx_src], bsz)
            set_b, n_tok_tiles = (oc_a, oc_b, oc_c, xb), n_all_tiles
        else:
            set_b, n_tok_tiles = set_a, n_lat_tiles
        x1, h2, idx, wts = _outproj(l, set_a, set_b, n_tok_tiles, min(n_a_tiles, n_tok_tiles), mod48, mod_row,
                                    norm2_g[l][None], w_out, rw2, rb)

        pos, src, tile_expert, next_expert, w_slot, n_valid = _sorted_rows(idx[:, :TOP_K])
        o_sorted = _experts(l, tile_expert, next_expert, w_slot, n_valid, src, h2,
                            moe_w_gate, moe_w_up, moe_w_down)
        xa = _combine(pos, o_sorted, x1, wts, mod48, mod_row_c)
        xb, n_a_tiles = xa, n_all_tiles
    return xa.reshape(bsz, n_lat, d)
```

```python
import functools
import math

import jax
import jax.numpy as jnp
from jax import lax
from jax.experimental import pallas as pl
from jax.experimental.pallas import tpu as pltpu

F32 = jnp.float32
BF16 = jnp.bfloat16

D_MODEL = 2048
GRID_W = 64
BLOCK = 128
WINDOW = 128
ROPE_THETA = 10000.0
EPS = 1e-6
NEG_INF = -1e30
LOG2E = math.log2(math.e)
A_HEADS, A_DK = 4, 64
A_DV = 2 * A_DK
B_HEADS, B_KV_HEADS, B_DH = 8, 2, 128
B_GROUP = B_HEADS // B_KV_HEADS
C_HEADS, C_Q_RANK, C_KV_RANK, C_NOPE, C_ROPE, C_DV = 4, 512, 256, 128, 64, 128
C_DQK = C_NOPE + C_ROPE
SPLIT_SIZES = (A_HEADS * 2 * A_DK, A_HEADS * 2 * A_DK, A_HEADS * A_DV,
               B_HEADS * B_DH, B_KV_HEADS * B_DH, B_KV_HEADS * B_DH,
               C_Q_RANK, C_KV_RANK, C_ROPE)
D_IN = sum(SPLIT_SIZES)
N_EXPERTS, N_GROUPS, TOP_K = 32, 4, 2
EXPERTS_PER_GROUP = N_EXPERTS // N_GROUPS
D_EXPERT = 512

LANES = 128
V7X_VMEM_LIMIT = 56 * 1024 * 1024

D_IN_PAD = ((D_IN + LANES - 1) // LANES) * LANES
C_HEAD_PAD = 2 * LANES
TM_PREP = 256
TQ_A = 512
TQ_C = 1024
QB_B = 512
TM_E = 128
X_SLOTS = 3
ROW_CHUNKS = D_MODEL // LANES
X_PITCH = ROW_CHUNKS + 8
DMA_QUEUES = 2
WEIGHT_DMA_PRIORITY = 1
TM_C = 128
ADA_TN = 1024
W_STAGE_ROWS = 128

_OFF = [0]
for _s in SPLIT_SIZES:
    _OFF.append(_OFF[-1] + _s)
O_AQ, O_AK, O_AV, O_BQ, O_BK, O_BV, O_CQ, O_CKV, O_CKR, _ = _OFF


def _cparams(sem):
    return pltpu.CompilerParams(dimension_semantics=sem, vmem_limit_bytes=V7X_VMEM_LIMIT)


def _silu(v):
    return v * (1.0 / (1.0 + jnp.exp(-v)))


def _ada_kernel(cond_ref, w_ref, b_ref, o_ref):
    s = _silu(cond_ref[...]).astype(BF16)
    o_ref[0] = jnp.dot(s, w_ref[0].astype(BF16), preferred_element_type=F32) + b_ref[0]


def _ada_modulation(cond8, ada_w, ada_b):
    depth, d, n = ada_w.shape
    return pl.pallas_call(
        _ada_kernel,
        out_shape=jax.ShapeDtypeStruct((depth, 8, n), F32),
        grid=(depth, n // ADA_TN),
        in_specs=[
            pl.BlockSpec((8, d), lambda l, j: (0, 0)),
            pl.BlockSpec((1, d, ADA_TN), lambda l, j: (l, 0, j)),
            pl.BlockSpec((1, 1, ADA_TN), lambda l, j: (l, 0, j)),
        ],
        out_specs=pl.BlockSpec((1, 8, ADA_TN), lambda l, j: (l, 0, j)),
        compiler_params=_cparams(("arbitrary", "arbitrary")),
        name="ada_modulation",
    )(cond8, ada_w, ada_b.reshape(depth, 1, n))


def _rope64(v, c, sa, sb):
    return v * c + pltpu.roll(v, 96, 1) * sa + pltpu.roll(v, 32, 1) * sb


def _rope128(v, c, s):
    return v * c + pltpu.roll(v, 64, 1) * s


def _norm_seg128(v, g):
    ms = jnp.sum(v * v, axis=-1, keepdims=True) * (1.0 / 128)
    return v * lax.rsqrt(ms + EPS) * g


def _norm_seg64x2(v, g, lo):
    sq = v * v
    s_lo = jnp.sum(jnp.where(lo, sq, 0.0), axis=-1, keepdims=True)
    s_hi = jnp.sum(jnp.where(lo, 0.0, sq), axis=-1, keepdims=True)
    ms = jnp.where(lo, s_lo, s_hi) * (1.0 / 64)
    return v * lax.rsqrt(ms + EPS) * g


def _norm_low64(v, g):
    ms = jnp.sum(v * v, axis=-1, keepdims=True) * (1.0 / 64)
    return v * lax.rsqrt(ms + EPS) * g


def _load_weight_bf16(w_hbm, layer, stage, sem, w_bf):
    k, n = w_hbm.shape[1], w_hbm.shape[2]
    ch = stage.shape[1]

    def chunk_copy(c):
        return pltpu.make_async_copy(w_hbm.at[layer, pl.ds(c * ch, ch), :], stage.at[c % 2], sem.at[c % 2])

    n_pad = w_bf.shape[1]
    if n_pad > n:
        edge = (n // LANES) * LANES
        w_bf[:, edge:n_pad] = jnp.zeros((k, n_pad - edge), BF16)
    chunk_copy(0).start()
    for c in range(k // ch):
        if c + 1 < k // ch:
            chunk_copy(c + 1).start()
        chunk_copy(c).wait()
        w_bf[c * ch:(c + 1) * ch, 0:n] = stage[c % 2].astype(BF16)


def _prep_kernel(xa_ref, xb_ref, sh_ref, sc_ref, g1_ref, w_hbm, ca_ref, saa_ref, sab_ref, cb_ref, sb_ref,
                 gains_ref, gcq_ref, gckv_ref, wuq_ref, wukv_ref,
                 aq_ref, ak_ref, av_ref, bq_ref, bk_ref, bv_ref, cq_ref, ck_ref, cv_ref,
                 w_bf, w_stage, w_sem, *, n_a_tiles, layer):
    @pl.when(pl.program_id(0) == 0)
    def _():
        _load_weight_bf16(w_hbm, layer, w_stage, w_sem, w_bf)

    x = jnp.where(pl.program_id(0) < n_a_tiles, xa_ref[...], xb_ref[...])
    ms = jnp.mean(x * x, axis=-1, keepdims=True)
    h = x * lax.rsqrt(ms + EPS) * g1_ref[...]
    h = h * (1.0 + sc_ref[0]) + sh_ref[0]
    z = jnp.dot(h.astype(BF16), w_bf[...], preferred_element_type=F32)

    lane = lax.broadcasted_iota(jnp.int32, (1, LANES), 1)
    lo = lane < 64
    ca, saa, sab = ca_ref[...], saa_ref[...], sab_ref[...]
    cb, sb = cb_ref[...], sb_ref[...]
    g_aq, g_ak, g_bq, g_bk = gains_ref[0:1], gains_ref[1:2], gains_ref[2:3], gains_ref[3:4]
    g_cqn, g_cqr, g_ckn, g_ckr = gains_ref[4:5], gains_ref[5:6], gains_ref[6:7], gains_ref[7:8]

    def blk(off, j):
        return z[:, off + j * LANES: off + (j + 1) * LANES]

    for j in range(A_HEADS):
        sl = slice(j * LANES, (j + 1) * LANES)
        aq_ref[:, sl] = _rope64(_norm_seg64x2(blk(O_AQ, j), g_aq, lo), ca, saa, sab).astype(BF16)
        ak_ref[:, sl] = _rope64(_norm_seg64x2(blk(O_AK, j), g_ak, lo), ca, saa, sab).astype(BF16)
        av_ref[:, sl] = blk(O_AV, j).astype(BF16)
    for j in range(B_HEADS):
        sl = slice(j * LANES, (j + 1) * LANES)
        bq_ref[:, sl] = _rope128(_norm_seg128(blk(O_BQ, j), g_bq), cb, sb).astype(BF16)
    for j in range(B_KV_HEADS):
        sl = slice(j * LANES, (j + 1) * LANES)
        bk_ref[:, sl] = _rope128(_norm_seg128(blk(O_BK, j), g_bk), cb, sb).astype(BF16)
        bv_ref[:, sl] = blk(O_BV, j).astype(BF16)
    cq = z[:, O_CQ:O_CQ + C_Q_RANK]
    cqn = cq * lax.rsqrt(jnp.mean(cq * cq, axis=-1, keepdims=True) + EPS) * gcq_ref[...]
    q = jnp.dot(cqn.astype(BF16), wuq_ref[...], preferred_element_type=F32)
    ckv = z[:, O_CKV:O_CKV + C_KV_RANK]
    ckvn = ckv * lax.rsqrt(jnp.mean(ckv * ckv, axis=-1, keepdims=True) + EPS) * gckv_ref[...]
    kv = jnp.dot(ckvn.astype(BF16), wukv_ref[...], preferred_element_type=F32)
    krope = _rope64(_norm_low64(z[:, O_CKR:O_CKR + LANES], g_ckr), ca, saa, sab).astype(BF16)
    for hh in range(C_HEADS):
        b0 = hh * C_HEAD_PAD
        cq_ref[:, b0:b0 + LANES] = _norm_seg128(q[:, b0:b0 + LANES], g_cqn).astype(BF16)
        cq_ref[:, b0 + LANES:b0 + 2 * LANES] = _rope64(
            _norm_low64(q[:, b0 + LANES:b0 + 2 * LANES], g_cqr), ca, saa, sab).astype(BF16)
        ck_ref[:, b0:b0 + LANES] = _norm_seg128(kv[:, b0:b0 + LANES], g_ckn).astype(BF16)
        ck_ref[:, b0 + LANES:b0 + 2 * LANES] = krope
        cv_ref[:, hh * LANES:(hh + 1) * LANES] = kv[:, b0 + LANES:b0 + 2 * LANES].astype(BF16)


def _split_rows(n_a_tiles):
    first = lambda i: (jnp.minimum(i, n_a_tiles - 1), 0)
    second = lambda i: (jnp.maximum(i - n_a_tiles, 0), 0)
    return first, second


def _prep(layer, xa, xb, n_tiles, n_a_tiles, mod48, mod_row, g1, w_in, tables, rope_blk, gains, gcq, gckv,
          wuq_bf, wukv_bf):
    d = xa.shape[1]
    tm = TM_PREP
    m = n_tiles * tm
    row = lambda i: (i, 0)
    const = lambda i: (0, 0)
    first, second = _split_rows(n_a_tiles)
    tab_spec = pl.BlockSpec((tm, LANES), lambda i: (rope_blk(i), 0))
    widths = (512, 512, 512, 1024, 256, 256, C_HEADS * C_HEAD_PAD, C_HEADS * C_HEAD_PAD, 512)
    return pl.pallas_call(
        functools.partial(_prep_kernel, n_a_tiles=n_a_tiles, layer=layer),
        out_shape=[jax.ShapeDtypeStruct((m, w), BF16) for w in widths],
        grid=(n_tiles,),
        in_specs=[
            pl.BlockSpec((tm, d), first),
            pl.BlockSpec((tm, d), second),
            pl.BlockSpec((1, 1, d), lambda i: (mod_row(i) * 6 + 0, 0, 0)),
            pl.BlockSpec((1, 1, d), lambda i: (mod_row(i) * 6 + 1, 0, 0)),
            pl.BlockSpec((1, d), const),
            pl.BlockSpec(memory_space=pl.ANY),
            tab_spec, tab_spec, tab_spec, tab_spec, tab_spec,
            pl.BlockSpec((8, LANES), const),
            pl.BlockSpec((1, C_Q_RANK), const),
            pl.BlockSpec((1, C_KV_RANK), const),
            pl.BlockSpec((C_Q_RANK, C_HEADS * C_HEAD_PAD), const, pipeline_mode=pl.Buffered(1)),
            pl.BlockSpec((C_KV_RANK, C_HEADS * C_HEAD_PAD), const, pipeline_mode=pl.Buffered(1)),
        ],
        out_specs=[pl.BlockSpec((tm, w), row) for w in widths],
        scratch_shapes=[pltpu.VMEM((d, D_IN_PAD), BF16),
                        pltpu.VMEM((2, W_STAGE_ROWS, w_in.shape[2]), F32),
                        pltpu.SemaphoreType.DMA((2,))],
        compiler_params=_cparams(("arbitrary",)),
        name="prep",
    )(xa, xb, mod48, mod48, g1, w_in, *tables, gains, gcq, gckv, wuq_bf, wukv_bf)


def _dot_nt(a, b):
    return lax.dot_general(a, b, (((1,), (1,)), ((), ())), preferred_element_type=F32)


def _dot_tn(a, b):
    return lax.dot_general(a, b, (((0,), (0,)), ((), ())), preferred_element_type=F32)


def _softmax_pv_t(q, k_refs, v_refs):
    s = [_dot_nt(k[...], q) for k in k_refs]
    m = functools.reduce(jnp.maximum, [jnp.max(si, axis=0, keepdims=True) for si in s])
    e = [jnp.exp2(si - m) for si in s]
    l = functools.reduce(jnp.add, [jnp.sum(ei, axis=0, keepdims=True) for ei in e])
    o = functools.reduce(jnp.add, [_dot_tn(v[...], ei.astype(BF16)) for ei, v in zip(e, v_refs)])
    return o * (1.0 / l)


def _attn_a_kernel(*refs, n_src, lam_init):
    lv_ref, gsub_ref, q_ref = refs[0], refs[1], refs[2]
    k_refs = refs[3:3 + n_src]
    v_refs = refs[3 + n_src:3 + 2 * n_src]
    o_ref = refs[3 + 2 * n_src]
    tq = q_ref.shape[0]
    lv = lv_ref[...]
    lam = (jnp.exp(jnp.sum(lv[0:1] * lv[1:2], axis=-1, keepdims=True))
           - jnp.exp(jnp.sum(lv[2:3] * lv[3:4], axis=-1, keepdims=True)) + lam_init)
    q = q_ref[...]
    lo = lax.broadcasted_iota(jnp.int32, (1, LANES), 1) < 64
    zero = jnp.zeros_like(q)
    qq = jnp.concatenate([jnp.where(lo, q, zero), jnp.where(lo, zero, q)], axis=0)
    o2 = _softmax_pv_t(qq, k_refs, v_refs)
    o = o2[:, :tq] - lam * o2[:, tq:]
    ms = jnp.mean(o * o, axis=0, keepdims=True)
    o = (o * lax.rsqrt(ms + EPS)).T
    o_ref[...] = (o * gsub_ref[...] * (1.0 - lam_init)).astype(BF16)


def _attn_a(lv, gsub, aq, ak, av, q_row0, q_rows, srcs, n_batch, layer_idx):
    tq = min(TQ_A, q_rows)
    nq = q_rows // tq
    qb0 = q_row0 // tq
    lam_init = 0.8 - 0.6 * math.exp(-0.3 * layer_idx)
    in_specs = [
        pl.BlockSpec((4, A_DK), lambda b, h, i: (0, 0)),
        pl.BlockSpec((1, A_DV), lambda b, h, i: (0, 0)),
        pl.BlockSpec((tq, LANES), lambda b, h, i: (qb0 + b * nq + i, h)),
    ]
    kv_specs = [pl.BlockSpec((rows, LANES), lambda b, h, i, blk0=row0 // rows: (blk0 + b, h))
                for row0, rows in srcs]
    return pl.pallas_call(
        functools.partial(_attn_a_kernel, n_src=len(srcs), lam_init=lam_init),
        out_shape=jax.ShapeDtypeStruct((n_batch * q_rows, A_HEADS * A_DV), BF16),
        grid=(n_batch, A_HEADS, nq),
        in_specs=in_specs + kv_specs + kv_specs,
        out_specs=pl.BlockSpec((tq, LANES), lambda b, h, i: (b * nq + i, h)),
        compiler_params=_cparams(("arbitrary", "arbitrary", "arbitrary")),
        name="attn_a",
    )(lv, gsub, aq, *([ak] * len(srcs)), *([av] * len(srcs)))


def _attn_c_kernel(*refs, n_src):
    q_ref = refs[0]
    k_refs = refs[1:1 + n_src]
    v_refs = refs[1 + n_src:1 + 2 * n_src]
    o_ref = refs[1 + 2 * n_src]
    o_ref[...] = _softmax_pv_t(q_ref[...], k_refs, v_refs).T.astype(BF16)


def _attn_c(cq, ck, cv, q_row0, q_rows, srcs, n_batch):
    tq = min(TQ_C, q_rows)
    nq = q_rows // tq
    qb0 = q_row0 // tq
    in_specs = [pl.BlockSpec((tq, C_HEAD_PAD), lambda b, h, i: (qb0 + b * nq + i, h))]
    k_specs = [pl.BlockSpec((rows, C_HEAD_PAD), lambda b, h, i, blk0=row0 // rows: (blk0 + b, h))
               for row0, rows in srcs]
    v_specs = [pl.BlockSpec((rows, C_DV), lambda b, h, i, blk0=row0 // rows: (blk0 + b, h))
               for row0, rows in srcs]
    return pl.pallas_call(
        functools.partial(_attn_c_kernel, n_src=len(srcs)),
        out_shape=jax.ShapeDtypeStruct((n_batch * q_rows, C_HEADS * C_DV), BF16),
        grid=(n_batch, C_HEADS, nq),
        in_specs=in_specs + k_specs + v_specs,
        out_specs=pl.BlockSpec((tq, C_DV), lambda b, h, i: (b * nq + i, h)),
        compiler_params=_cparams(("arbitrary", "arbitrary", "arbitrary")),
        name="attn_c",
    )(cq, *([ck] * len(srcs)), *([cv] * len(srcs)))


def _stack_heads(q):
    return jnp.concatenate([q[:, g * LANES:(g + 1) * LANES] for g in range(B_GROUP)], axis=0)


def _sink_row(sink_ref, kvh, cols):
    return jnp.concatenate(
        [jnp.full((1, cols), sink_ref[kvh * B_GROUP + g] * LOG2E, F32) for g in range(B_GROUP)], axis=1)


def _attn_b_kernel(sink_ref, q_ref, kp_ref, km_ref, kn_ref, vp_ref, vm_ref, vn_ref, kc_ref, vc_ref, o_ref):
    kvh = pl.program_id(1)
    qb = pl.program_id(2)
    nqb = pl.num_programs(2)
    n_blk = QB_B // BLOCK
    kband = jnp.concatenate([kp_ref[...], km_ref[...], kn_ref[...]], axis=0)
    vband = jnp.concatenate([vp_ref[...], vm_ref[...], vn_ref[...]], axis=0)
    kc, vc = kc_ref[...], vc_ref[...]
    sink = _sink_row(sink_ref, kvh, BLOCK)
    c = lax.broadcasted_iota(jnp.int32, (3 * BLOCK, B_GROUP * BLOCK), 0)
    r = lax.broadcasted_iota(jnp.int32, (3 * BLOCK, B_GROUP * BLOCK), 1) % BLOCK
    cr = c - r
    band_ok = (cr >= 0) & (cr <= BLOCK + WINDOW)
    for j in range(n_blk):
        q4 = _stack_heads(q_ref[j * BLOCK:(j + 1) * BLOCK, :])
        s_loc = _dot_nt(kband[j * BLOCK:(j + 3) * BLOCK], q4)
        valid = band_ok
        if j == 0:
            valid = valid & (c >= jnp.where(qb > 0, 0, BLOCK))
        if j == n_blk - 1:
            valid = valid & (c < jnp.where(qb < nqb - 1, 3 * BLOCK, 2 * BLOCK))
        s_loc = jnp.where(valid, s_loc, NEG_INF)
        s_ctx = _dot_nt(kc, q4)
        m = jnp.maximum(jnp.maximum(jnp.max(s_loc, axis=0, keepdims=True),
                                    jnp.max(s_ctx, axis=0, keepdims=True)), sink)
        e_loc, e_ctx = jnp.exp2(s_loc - m), jnp.exp2(s_ctx - m)
        l = (jnp.sum(e_loc, axis=0, keepdims=True) + jnp.sum(e_ctx, axis=0, keepdims=True)
             + jnp.exp2(sink - m))
        o = (_dot_tn(vband[j * BLOCK:(j + 3) * BLOCK], e_loc.astype(BF16))
             + _dot_tn(vc, e_ctx.astype(BF16))) * (1.0 / l)
        o = o.T
        for g in range(B_GROUP):
            o_ref[j * BLOCK:(j + 1) * BLOCK, g * LANES:(g + 1) * LANES] = (
                o[g * BLOCK:(g + 1) * BLOCK].astype(BF16))


def _attn_b(sink, bq, bk, bv, n_batch, n_lat, n_ctx):
    nqb = n_lat // QB_B
    per = QB_B // BLOCK
    blocks_per_batch = n_lat // BLOCK
    ctx_blk0 = n_batch * n_lat // n_ctx
    gw = B_GROUP * B_DH
    prev = lambda b, h, i: (b * blocks_per_batch + jnp.maximum(i * per - 1, 0), h)
    main = lambda b, h, i: (b * nqb + i, h)
    nxt = lambda b, h, i: (b * blocks_per_batch + jnp.minimum(i * per + per, blocks_per_batch - 1), h)
    ctx = lambda b, h, i: (ctx_blk0 + b, h)
    return pl.pallas_call(
        _attn_b_kernel,
        out_shape=jax.ShapeDtypeStruct((n_batch * n_lat, B_HEADS * B_DH), BF16),
        grid=(n_batch, B_KV_HEADS, nqb),
        in_specs=[
            pl.BlockSpec(memory_space=pltpu.SMEM),
            pl.BlockSpec((QB_B, gw), main),
            pl.BlockSpec((BLOCK, B_DH), prev), pl.BlockSpec((QB_B, B_DH), main), pl.BlockSpec((BLOCK, B_DH), nxt),
            pl.BlockSpec((BLOCK, B_DH), prev), pl.BlockSpec((QB_B, B_DH), main), pl.BlockSpec((BLOCK, B_DH), nxt),
            pl.BlockSpec((n_ctx, B_DH), ctx), pl.BlockSpec((n_ctx, B_DH), ctx),
        ],
        out_specs=pl.BlockSpec((QB_B, gw), main),
        compiler_params=_cparams(("arbitrary", "arbitrary", "arbitrary")),
        name="attn_b",
    )(sink, bq, bk, bk, bk, bv, bv, bv, bk, bv)


def _attn_b_ctx_kernel(sink_ref, q_ref, k_ref, v_ref, o_ref):
    kvh = pl.program_id(1)
    rows = q_ref.shape[0]
    q4 = _stack_heads(q_ref[...])
    sink = _sink_row(sink_ref, kvh, rows)
    s = _dot_nt(k_ref[...], q4)
    m = jnp.maximum(jnp.max(s, axis=0, keepdims=True), sink)
    e = jnp.exp2(s - m)
    l = jnp.sum(e, axis=0, keepdims=True) + jnp.exp2(sink - m)
    o = (_dot_tn(v_ref[...], e.astype(BF16)) * (1.0 / l)).T
    for g in range(B_GROUP):
        o_ref[:, g * LANES:(g + 1) * LANES] = o[g * rows:(g + 1) * rows].astype(BF16)


def _attn_b_ctx(sink, bq, bk, bv, n_batch, n_lat, n_ctx):
    gw = B_GROUP * B_DH
    ctx_blk0 = n_batch * n_lat // n_ctx
    return pl.pallas_call(
        _attn_b_ctx_kernel,
        out_shape=jax.ShapeDtypeStruct((n_batch * n_ctx, B_HEADS * B_DH), BF16),
        grid=(n_batch, B_KV_HEADS),
        in_specs=[
            pl.BlockSpec(memory_space=pltpu.SMEM),
            pl.BlockSpec((n_ctx, gw), lambda b, h: (ctx_blk0 + b, h)),
            pl.BlockSpec((n_ctx, B_DH), lambda b, h: (ctx_blk0 + b, h)),
            pl.BlockSpec((n_ctx, B_DH), lambda b, h: (ctx_blk0 + b, h)),
        ],
        out_specs=pl.BlockSpec((n_ctx, gw), lambda b, h: (b, h)),
        compiler_params=_cparams(("arbitrary", "arbitrary")),
        name="attn_b_ctx",
    )(sink, bq, bk, bv)


def _route(h2, rw_ref, rb_ref):
    tm = h2.shape[0]
    h_hi = h2.astype(BF16)
    h_lo = (h2 - h_hi.astype(F32)).astype(BF16)
    p = jnp.dot(jnp.concatenate([h_hi, h_lo], axis=0), rw_ref[...], preferred_element_type=F32)
    p = p[:tm] + p[tm:]
    logits = p[:, :N_EXPERTS] + p[:, N_EXPERTS:]
    scores = 1.0 / (1.0 + jnp.exp(-logits))
    sel = scores + rb_ref[...]
    lane_i = lax.broadcasted_iota(jnp.int32, sel.shape, 1)
    lane = lane_i.astype(F32)
    big = float(N_EXPERTS)

    def top2(mask):
        v = jnp.where(mask, sel, -jnp.inf)
        m1 = jnp.max(v, axis=-1, keepdims=True)
        i1 = jnp.min(jnp.where(v == m1, lane, big), axis=-1, keepdims=True)
        v2 = jnp.where(lane == i1, -jnp.inf, v)
        m2 = jnp.max(v2, axis=-1, keepdims=True)
        i2 = jnp.min(jnp.where(v2 == m2, lane, big), axis=-1, keepdims=True)
        return m1, i1, m2, i2

    best = None
    for g in range(N_GROUPS):
        m1, i1, m2, i2 = top2((lane_i >= g * EXPERTS_PER_GROUP) & (lane_i < (g + 1) * EXPERTS_PER_GROUP))
        gs = m1 + m2
        if best is None:
            best = (gs, i1, i2)
        else:
            take = gs > best[0]
            best = (jnp.where(take, gs, best[0]), jnp.where(take, i1, best[1]), jnp.where(take, i2, best[2]))
    _, e1, e2 = best
    w1 = jnp.sum(jnp.where(lane == e1, scores, 0.0), axis=-1, keepdims=True)
    w2 = jnp.sum(jnp.where(lane == e2, scores, 0.0), axis=-1, keepdims=True)
    tot = w1 + w2
    return e1, e2, w1 / tot, w2 / tot


def _outproj_kernel(oa1_ref, ob1_ref, oc1_ref, x1in_ref, oa2_ref, ob2_ref, oc2_ref, x2in_ref,
                    g1_ref, sh2_ref, sc2_ref, n2_ref, w_hbm, rw_ref, rb_ref,
                    x1_ref, h2_ref, idx_ref, wt_ref, cnt_ref, w_ref, w_stage, w_sem, cnt_acc,
                    *, n_a_tiles, layer):
    @pl.when(pl.program_id(0) == 0)
    def _():
        _load_weight_bf16(w_hbm, layer, w_stage, w_sem, w_ref)

    first = pl.program_id(0) < n_a_tiles
    oa = jnp.where(first, oa1_ref[...], oa2_ref[...])
    ob = jnp.where(first, ob1_ref[...], ob2_ref[...])
    oc = jnp.where(first, oc1_ref[...], oc2_ref[...])
    x = jnp.where(first, x1in_ref[...], x2in_ref[...])
    y = jnp.dot(jnp.concatenate([oa, ob, oc], axis=1), w_ref[...], preferred_element_type=F32)
    x1 = x + g1_ref[0] * y
    x1_ref[...] = x1
    ms = jnp.mean(x1 * x1, axis=-1, keepdims=True)
    h2 = x1 * lax.rsqrt(ms + EPS) * n2_ref[...]
    h2 = h2 * (1.0 + sc2_ref[0]) + sh2_ref[0]
    tm = h2.shape[0]
    for cc in range(ROW_CHUNKS):
        h2_ref[pl.ds(cc, tm, stride=ROW_CHUNKS), :] = h2[:, cc * LANES:(cc + 1) * LANES]
    e1, e2, w1, w2 = _route(h2, rw_ref, rb_ref)
    lane = lax.broadcasted_iota(jnp.int32, idx_ref.shape, 1)
    wt_ref[...] = jnp.where(lane == 0, w1, jnp.where(lane == 1, w2, 0.0))

    @pl.when(pl.program_id(0) == 0)
    def _():
        cnt_acc[...] = jnp.zeros_like(cnt_acc)

    lane_f = lane.astype(F32)
    sel1, sel2 = lane_f == e1, lane_f == e2
    picks = jnp.where(sel1, 1.0, 0.0) + jnp.where(sel2, 1.0, 0.0)
    t_row = lax.broadcasted_iota(jnp.int32, (tm, tm), 0)
    t_col = lax.broadcasted_iota(jnp.int32, (tm, tm), 1)
    earlier = jnp.where(t_col < t_row, 1.0, 0.0).astype(BF16)
    before = jnp.dot(earlier, picks.astype(BF16), preferred_element_type=F32) + cnt_acc[0:1, :]
    r1 = jnp.sum(jnp.where(sel1, before, 0.0), axis=-1, keepdims=True)
    r2 = jnp.sum(jnp.where(sel2, before, 0.0), axis=-1, keepdims=True)
    idx_ref[...] = jnp.where(lane == 0, e1, jnp.where(lane == 1, e2, jnp.where(
        lane == 2, r1, jnp.where(lane == 3, r2, 0.0)))).astype(jnp.int32)
    cnt_acc[...] = cnt_acc[...] + jnp.sum(picks, axis=0, keepdims=True)
    cnt_ref[...] = cnt_acc[...]


def _outproj(layer, set_a, set_b, n_tiles, n_a_tiles, mod48, mod_row, n2, w_out, router_w, router_b):
    d = set_a[3].shape[1]
    tm = TM_PREP
    m = n_tiles * tm
    row = lambda i: (i, 0)
    const = lambda i: (0, 0)
    first, second = _split_rows(n_a_tiles)
    mod_spec = lambda j: pl.BlockSpec((1, 1, d), lambda i: (mod_row(i) * 6 + j, 0, 0))
    in_specs = (
        [pl.BlockSpec((tm, a.shape[1]), first) for a in set_a]
        + [pl.BlockSpec((tm, a.shape[1]), second) for a in set_b]
        + [mod_spec(2), mod_spec(3), mod_spec(4),
           pl.BlockSpec((1, d), const),
           pl.BlockSpec(memory_space=pl.ANY),
           pl.BlockSpec((d, 2 * N_EXPERTS), const),
           pl.BlockSpec((1, N_EXPERTS), const)])
    return pl.pallas_call(
        functools.partial(_outproj_kernel, n_a_tiles=n_a_tiles, layer=layer),
        out_shape=[jax.ShapeDtypeStruct((m, d), F32), jax.ShapeDtypeStruct((m * ROW_CHUNKS, LANES), F32),
                   jax.ShapeDtypeStruct((m, LANES), jnp.int32), jax.ShapeDtypeStruct((m, LANES), F32),
                   jax.ShapeDtypeStruct((8, LANES), F32)],
        grid=(n_tiles,),
        in_specs=in_specs,
        out_specs=[pl.BlockSpec((tm, d), row), pl.BlockSpec((tm * ROW_CHUNKS, LANES), row),
                   pl.BlockSpec((tm, LANES), row), pl.BlockSpec((tm, LANES), row),
                   pl.BlockSpec((8, LANES), const)],
        scratch_shapes=[pltpu.VMEM((d, d), BF16),
                        pltpu.VMEM((2, W_STAGE_ROWS, d), F32),
                        pltpu.SemaphoreType.DMA((2,)),
                        pltpu.VMEM((8, LANES), F32)],
        compiler_params=_cparams(("arbitrary",)),
        name="outproj",
    )(*set_a, *set_b, mod48, mod48, mod48, n2, w_out, router_w, router_b)


def _experts_kernel(te_ref, nxt_ref, ws_ref, nv_ref, src_ref, h2_hbm, wg_hbm, wu_hbm, wd_hbm, o_ref,
                    xbuf, xsem, wg_f, wu_f, wd_f, wsem, wg_bf, wu_bf, wd_bf, *, layer):
    i = pl.program_id(0)
    n_valid = nv_ref[0]
    slot = i % X_SLOTS

    def issue_rows(tile):
        s = tile % X_SLOTS
        base = tile * TM_E
        for r in range(TM_E):
            row0 = pl.multiple_of(src_ref[base + r] * ROW_CHUNKS, ROW_CHUNKS)
            pltpu.make_async_copy(h2_hbm.at[pl.ds(row0, ROW_CHUNKS), :],
                                  xbuf.at[s, pl.ds(r * X_PITCH, ROW_CHUNKS), :], xsem.at[s]).start()

    def wait_rows(s):
        pltpu.make_async_copy(h2_hbm.at[pl.ds(0, TM_E * ROW_CHUNKS), :],
                              xbuf.at[s, pl.ds(0, TM_E * ROW_CHUNKS), :], xsem.at[s]).wait()

    def weight_copies(e, s):
        return (pltpu.make_async_copy(wg_hbm.at[layer, e], wg_f.at[s], wsem.at[s]),
                pltpu.make_async_copy(wu_hbm.at[layer, e], wu_f.at[s], wsem.at[s]),
                pltpu.make_async_copy(wd_hbm.at[layer, e], wd_f.at[s], wsem.at[s]))

    @pl.when(i == 0)
    def _():
        for cp in weight_copies(te_ref[0], ws_ref[0]):
            cp.start()
        for t in range(X_SLOTS - 1):
            issue_rows(t)

    first_of_expert = jnp.logical_or(i == 0, te_ref[i] != te_ref[jnp.maximum(i - 1, 0)])

    @pl.when(jnp.logical_and(first_of_expert, i < n_valid))
    def _():
        ws = ws_ref[i]
        for cp in weight_copies(te_ref[i], ws):
            cp.wait()

        @pl.when(nxt_ref[i] >= 0)
        def _():
            for cp in weight_copies(nxt_ref[i], 1 - ws):
                cp.start(priority=WEIGHT_DMA_PRIORITY)

        wg_bf[...] = wg_f[ws].astype(BF16)
        wu_bf[...] = wu_f[ws].astype(BF16)
        wd_bf[...] = wd_f[ws].astype(BF16)

    @pl.when(i < n_valid)
    def _():
        wait_rows(slot)
        x = jnp.concatenate([xbuf[slot, pl.ds(cc, TM_E, stride=X_PITCH), :] for cc in range(ROW_CHUNKS)],
                            axis=1).astype(BF16)
        issue_rows(i + X_SLOTS - 1)
        g = jnp.dot(x, wg_bf[...], preferred_element_type=F32)
        u = jnp.dot(x, wu_bf[...], preferred_element_type=F32)
        a = (_silu(g) * u).astype(BF16)
        y = jnp.dot(a, wd_bf[...], preferred_element_type=F32)
        for cc in range(ROW_CHUNKS):
            o_ref[pl.ds(cc, TM_E, stride=ROW_CHUNKS), :] = y[:, cc * LANES:(cc + 1) * LANES]

    @pl.when(i == n_valid - 1)
    def _():
        for t in range(1, X_SLOTS):
            wait_rows((i + t) % X_SLOTS)

    @pl.when(i >= n_valid)
    def _():
        o_ref[...] = jnp.zeros_like(o_ref)


def _experts(layer, tile_expert, next_expert, w_slot, n_valid, src, h2_all, w_gate, w_up, w_down):
    n_rows = src.shape[0] - (X_SLOTS - 1) * TM_E
    nt = n_rows // TM_E
    d, de = w_gate.shape[2], w_gate.shape[3]
    grid_spec = pltpu.PrefetchScalarGridSpec(
        num_scalar_prefetch=5,
        grid=(nt,),
        in_specs=[pl.BlockSpec(memory_space=pl.ANY)] * 4,
        out_specs=pl.BlockSpec((TM_E * ROW_CHUNKS, LANES), lambda i, *_: (i, 0)),
        scratch_shapes=[
            pltpu.VMEM((X_SLOTS, TM_E * X_PITCH, LANES), F32),
            pltpu.SemaphoreType.DMA((X_SLOTS,)),
            pltpu.VMEM((2, d, de), F32), pltpu.VMEM((2, d, de), F32), pltpu.VMEM((2, de, d), F32),
            pltpu.SemaphoreType.DMA((2,)),
            pltpu.VMEM((d, de), BF16), pltpu.VMEM((d, de), BF16), pltpu.VMEM((de, d), BF16),
        ],
    )
    return pl.pallas_call(
        functools.partial(_experts_kernel, layer=layer),
        out_shape=jax.ShapeDtypeStruct((n_rows * ROW_CHUNKS, LANES), F32),
        grid_spec=grid_spec,
        compiler_params=_cparams(("arbitrary",)),
        name="experts",
    )(tile_expert, next_expert, w_slot, n_valid, src, h2_all, w_gate, w_up, w_down)


def _combine_kernel(pos_ref, o_hbm, x1_ref, wt_ref, g2_ref, x2_ref, buf, sem):
    i = pl.program_id(0)
    nt = pl.num_programs(0)
    slot = i % 2

    def issue(tile, s):
        base = tile * (TM_C * TOP_K)
        for r in range(TM_C):
            for k in range(TOP_K):
                row0 = pl.multiple_of(pos_ref[base + TOP_K * r + k] * ROW_CHUNKS, ROW_CHUNKS)
                pltpu.make_async_copy(o_hbm.at[pl.ds(row0, ROW_CHUNKS), :],
                                      buf.at[s, k, pl.ds(r * X_PITCH, ROW_CHUNKS), :], sem.at[s]
                                      ).start(priority=k % DMA_QUEUES)

    @pl.when(i == 0)
    def _():
        issue(0, 0)

    @pl.when(i + 1 < nt)
    def _():
        issue(i + 1, 1 - slot)

    def expert_rows(k):
        return jnp.concatenate([buf[slot, k, pl.ds(cc, TM_C, stride=X_PITCH), :] for cc in range(ROW_CHUNKS)],
                               axis=1)

    for k in range(TOP_K):
        pltpu.make_async_copy(o_hbm.at[pl.ds(0, TM_C * ROW_CHUNKS), :],
                              buf.at[slot, k, pl.ds(0, TM_C * ROW_CHUNKS), :], sem.at[slot]).wait()
    wt = wt_ref[...]
    y = wt[:, 0:1] * expert_rows(0) + wt[:, 1:2] * expert_rows(1)
    x2_ref[...] = x1_ref[...] + g2_ref[0] * y


def _combine(pos, o_sorted, x1, wts, mod48, mod_row):
    m, d = x1.shape
    grid_spec = pltpu.PrefetchScalarGridSpec(
        num_scalar_prefetch=1,
        grid=(m // TM_C,),
        in_specs=[
            pl.BlockSpec(memory_space=pl.ANY),
            pl.BlockSpec((TM_C, d), lambda i, p: (i, 0)),
            pl.BlockSpec((TM_C, LANES), lambda i, p: (i, 0)),
            pl.BlockSpec((1, 1, d), lambda i, p: (mod_row(i) * 6 + 5, 0, 0)),
        ],
        out_specs=pl.BlockSpec((TM_C, d), lambda i, p: (i, 0)),
        scratch_shapes=[pltpu.VMEM((2, TOP_K, TM_C * X_PITCH, LANES), F32), pltpu.SemaphoreType.DMA((2,))],
    )
    return pl.pallas_call(
        _combine_kernel,
        out_shape=jax.ShapeDtypeStruct((m, d), F32),
        grid_spec=grid_spec,
        compiler_params=_cparams(("arbitrary",)),
        name="combine",
    )(pos, o_sorted, x1, wts, mod48)


def _rope_tables(n_lat, tm):
    t = jnp.arange(n_lat)
    r = (t // GRID_W).astype(F32)
    col = (t % GRID_W).astype(F32)

    def cos_sin(dim):
        nf = dim // 4
        inv = ROPE_THETA ** (-jnp.arange(nf, dtype=F32) / nf)
        ang = jnp.concatenate([r[:, None] * inv, col[:, None] * inv], axis=-1)
        return jnp.cos(ang), jnp.sin(ang)

    c64, s64 = cos_sin(A_DK)
    c128, s128 = cos_sin(B_DH)
    z32 = jnp.zeros_like(s64)
    tabs = [
        jnp.concatenate([c64, c64, c64, c64], axis=-1),
        jnp.concatenate([-s64, z32, -s64, z32], axis=-1),
        jnp.concatenate([z32, s64, z32, s64], axis=-1),
        jnp.concatenate([c128, c128], axis=-1),
        jnp.concatenate([-s128, s128], axis=-1),
    ]
    ident = [jnp.ones((tm, LANES), F32), jnp.zeros((tm, LANES), F32), jnp.zeros((tm, LANES), F32),
             jnp.ones((tm, LANES), F32), jnp.zeros((tm, LANES), F32)]
    return [jnp.concatenate([a, b], axis=0) for a, b in zip(tabs, ident)]


def _layer_params(l, a_qn, a_kn, b_qn, b_kn, c_qa_norm, c_kva_norm, c_wuq, c_wukv, c_qn, c_kn):
    z64 = jnp.zeros((C_ROPE,), F32)
    gains = jnp.stack([
        jnp.tile(a_qn[l], 2) * (A_DK ** -0.5 * LOG2E),
        jnp.tile(a_kn[l], 2),
        b_qn[l] * (B_DH ** -0.5 * LOG2E),
        b_kn[l],
        c_qn[l][:C_NOPE] * (C_DQK ** -0.5 * LOG2E),
        jnp.concatenate([c_qn[l][C_NOPE:] * (C_DQK ** -0.5 * LOG2E), z64]),
        c_kn[l][:C_NOPE],
        jnp.concatenate([c_kn[l][C_NOPE:], z64]),
    ])
    wq = c_wuq[l].reshape(C_Q_RANK, C_HEADS, C_DQK)
    wq = jnp.pad(wq, ((0, 0), (0, 0), (0, C_HEAD_PAD - C_DQK))).reshape(C_Q_RANK, C_HEADS * C_HEAD_PAD)
    return dict(gains=gains, gcq=c_qa_norm[l][None], gckv=c_kva_norm[l][None],
                wuq=wq.astype(BF16), wukv=c_wukv[l].astype(BF16))


def _sorted_rows(idx, rank, counts):
    t = idx.shape[0]
    n_pairs = t * TOP_K
    n_rows = ((n_pairs + N_EXPERTS * (TM_E - 1)) // TM_E) * TM_E
    nt = n_rows // TM_E
    flat_e = idx.reshape(-1)
    rank = rank.reshape(-1)
    padded = ((counts + TM_E - 1) // TM_E) * TM_E
    ends = jnp.cumsum(padded)
    pos = (ends - padded)[flat_e] + rank
    src = jnp.zeros((n_rows + (X_SLOTS - 1) * TM_E,), jnp.int32).at[pos].set(
        jnp.arange(n_pairs, dtype=jnp.int32) // TOP_K, unique_indices=True)
    tile_start = jnp.arange(nt, dtype=jnp.int32) * TM_E
    tile_expert = jnp.sum((ends[None, :] <= tile_start[:, None]).astype(jnp.int32), axis=1)
    last_used = jnp.sum((ends <= ends[-1] - 1).astype(jnp.int32))
    tile_expert = jnp.minimum(tile_expert, last_used)
    n_valid = (ends[-1:] // TM_E).astype(jnp.int32)
    used = counts > 0
    e_ids = jnp.arange(N_EXPERTS, dtype=jnp.int32)
    later = jnp.where(used[None, :] & (e_ids[None, :] > e_ids[:, None]), e_ids[None, :], N_EXPERTS)
    next_used = jnp.min(later, axis=1)
    next_used = jnp.where(next_used == N_EXPERTS, -1, next_used).astype(jnp.int32)
    parity = ((jnp.cumsum(used.astype(jnp.int32)) - 1) % 2).astype(jnp.int32)
    return (pos.astype(jnp.int32), src, tile_expert.astype(jnp.int32), next_used[tile_expert],
            parity[tile_expert], n_valid)


def kernel(x, c, ctx, c_ctx, ada_w, ada_b, norm1_g, norm2_g, w_in, w_out, a_qn, a_kn, a_lambda, a_subln,
           b_qn, b_kn, b_sink, c_qa_norm, c_kva_norm, c_wuq, c_wukv, c_qn, c_kn,
           router_w, router_bias, moe_w_gate, moe_w_up, moe_w_down):
    bsz, n_lat, d = x.shape
    n_ctx = ctx.shape[1]
    depth = ada_w.shape[0]
    t_lat, t_ctx = bsz * n_lat, bsz * n_ctx
    tm = TM_PREP
    lat_tiles = n_lat // tm
    n_lat_tiles, n_ctx_tiles = t_lat // tm, t_ctx // tm
    n_all_tiles = n_lat_tiles + n_ctx_tiles

    cond8 = jnp.concatenate([c, c_ctx[None], jnp.zeros((8 - bsz - 1, d), F32)], axis=0)
    mod = _ada_modulation(cond8, ada_w, ada_b)
    tables = _rope_tables(n_lat, tm)
    rb = router_bias[None]
    rw_hi = router_w.astype(BF16)
    rw_lo = (router_w - rw_hi.astype(F32)).astype(BF16)
    rw2 = jnp.concatenate([rw_hi, rw_lo], axis=1)

    mod_row = lambda i: jnp.minimum(i // lat_tiles, bsz)
    mod_row_c = lambda i: jnp.minimum(i // (n_lat // TM_C), bsz)
    rope_blk = lambda i: jnp.where(i < n_lat_tiles, i % lat_tiles, lat_tiles)

    xa, xb, n_a_tiles = x.reshape(t_lat, d), ctx.reshape(t_ctx, d), n_lat_tiles
    lat_src, ctx_src = (0, n_lat), (t_lat, n_ctx)
    for l in range(depth):
        last = l == depth - 1
        p = _layer_params(l, a_qn, a_kn, b_qn, b_kn, c_qa_norm, c_kva_norm, c_wuq, c_wukv, c_qn, c_kn)
        mod48 = mod[l].reshape(8 * 6, 1, d)
        aq, ak, av, bq, bk, bv, cq, ck, cv = _prep(
            l, xa, xb, n_all_tiles, n_a_tiles, mod48, mod_row, norm1_g[l][None], w_in, tables, rope_blk,
            p["gains"], p["gcq"], p["gckv"], p["wuq"], p["wukv"])

        lv, gsub, sink = a_lambda[l], a_subln[l][None], b_sink[l]
        o_a = _attn_a(lv, gsub, aq, ak, av, 0, n_lat, [lat_src, ctx_src], bsz, l)
        o_b = _attn_b(sink, bq, bk, bv, bsz, n_lat, n_ctx)
        o_c = _attn_c(cq, ck, cv, 0, n_lat, [lat_src, ctx_src], bsz)
        set_a = (o_a, o_b, o_c, xa)
        if not last:
            oc_a = _attn_a(lv, gsub, aq, ak, av, t_lat, n_ctx, [ctx_src], bsz, l)
            oc_b = _attn_b_ctx(sink, bq, bk, bv, bsz, n_lat, n_ctx)
            oc_c = _attn_c(cq, ck, cv, t_lat, n_ctx, [ctx_src], bsz)
            set_b, n_tok_tiles = (oc_a, oc_b, oc_c, xb), n_all_tiles
        else:
            set_b, n_tok_tiles = set_a, n_lat_tiles
        x1, h2, idx, wts, cnt = _outproj(l, set_a, set_b, n_tok_tiles, min(n_a_tiles, n_tok_tiles), mod48,
                                         mod_row, norm2_g[l][None], w_out, rw2, rb)

        pos, src, tile_expert, next_expert, w_slot, n_valid = _sorted_rows(
            idx[:, :TOP_K], idx[:, TOP_K:2 * TOP_K], cnt[0, :N_EXPERTS].astype(jnp.int32))
        o_sorted = _experts(l, tile_expert, next_expert, w_slot, n_valid, src, h2,
                            moe_w_gate, moe_w_up, moe_w_down)
        xa = _combine(pos, o_sorted, x1, wts, mod48, mod_row_c)
        xb, n_a_tiles = xa, n_all_tiles
    return xa.reshape(bsz, n_lat, d)
```

```python
import functools
import math

import jax
import jax.numpy as jnp
import numpy as np
from jax import lax
from jax.experimental import pallas as pl
from jax.experimental.pallas import tpu as pltpu

F32 = jnp.float32
BF16 = jnp.bfloat16

D_MODEL = 2048
GRID_W = 64
BLOCK = 128
WINDOW = 128
ROPE_THETA = 10000.0
EPS = 1e-6
NEG_INF = -1e30
LOG2E = math.log2(math.e)
A_HEADS, A_DK = 4, 64
A_DV = 2 * A_DK
B_HEADS, B_KV_HEADS, B_DH = 8, 2, 128
B_GROUP = B_HEADS // B_KV_HEADS
C_HEADS, C_Q_RANK, C_KV_RANK, C_NOPE, C_ROPE, C_DV = 4, 512, 256, 128, 64, 128
C_DQK = C_NOPE + C_ROPE
SPLIT_SIZES = (A_HEADS * 2 * A_DK, A_HEADS * 2 * A_DK, A_HEADS * A_DV,
               B_HEADS * B_DH, B_KV_HEADS * B_DH, B_KV_HEADS * B_DH,
               C_Q_RANK, C_KV_RANK, C_ROPE)
D_IN = sum(SPLIT_SIZES)
N_EXPERTS, N_GROUPS, TOP_K = 32, 4, 2
EXPERTS_PER_GROUP = N_EXPERTS // N_GROUPS
D_EXPERT = 512

LANES = 128
V7X_VMEM_LIMIT = 56 * 1024 * 1024

D_IN_PAD = ((D_IN + LANES - 1) // LANES) * LANES
C_HEAD_PAD = 2 * LANES
TM_PREP = 256
TQ_A = 512
TQ_C = 1024
C_HEADS_PER_STEP = 1
QB_B = 512
TM_E = 128
X_SLOTS = 3
ROW_CHUNKS = D_MODEL // LANES
X_PITCH = ROW_CHUNKS + 8
DMA_QUEUES = 2
WEIGHT_DMA_PRIORITY = 1
TM_C = 128
ADA_TN = 1024
W_STAGE_ROWS = 128

_OFF = [0]
for _s in SPLIT_SIZES:
    _OFF.append(_OFF[-1] + _s)
O_AQ, O_AK, O_AV, O_BQ, O_BK, O_BV, O_CQ, O_CKV, O_CKR, _ = _OFF


def _cparams(sem):
    return pltpu.CompilerParams(dimension_semantics=sem, vmem_limit_bytes=V7X_VMEM_LIMIT)


def _silu(v):
    return v * (1.0 / (1.0 + jnp.exp(-v)))


def _ada_kernel(cond_ref, w_ref, b_ref, o_ref):
    s = _silu(cond_ref[...]).astype(BF16)
    o_ref[0] = jnp.dot(s, w_ref[0].astype(BF16), preferred_element_type=F32) + b_ref[0]


def _ada_modulation(cond8, ada_w, ada_b):
    depth, d, n = ada_w.shape
    return pl.pallas_call(
        _ada_kernel,
        out_shape=jax.ShapeDtypeStruct((depth, 8, n), F32),
        grid=(depth, n // ADA_TN),
        in_specs=[
            pl.BlockSpec((8, d), lambda l, j: (0, 0)),
            pl.BlockSpec((1, d, ADA_TN), lambda l, j: (l, 0, j)),
            pl.BlockSpec((1, 1, ADA_TN), lambda l, j: (l, 0, j)),
        ],
        out_specs=pl.BlockSpec((1, 8, ADA_TN), lambda l, j: (l, 0, j)),
        compiler_params=_cparams(("arbitrary", "arbitrary")),
        name="ada_modulation",
    )(cond8, ada_w, ada_b.reshape(depth, 1, n))


def _rope64(v, c, sa, sb):
    return v * c + pltpu.roll(v, 96, 1) * sa + pltpu.roll(v, 32, 1) * sb


def _rope128(v, c, s):
    return v * c + pltpu.roll(v, 64, 1) * s


def _norm_seg128(v, g):
    ms = jnp.sum(v * v, axis=-1, keepdims=True) * (1.0 / 128)
    return v * lax.rsqrt(ms + EPS) * g


def _norm_seg64x2(v, g, lo):
    sq = v * v
    s_lo = jnp.sum(jnp.where(lo, sq, 0.0), axis=-1, keepdims=True)
    s_hi = jnp.sum(jnp.where(lo, 0.0, sq), axis=-1, keepdims=True)
    ms = jnp.where(lo, s_lo, s_hi) * (1.0 / 64)
    return v * lax.rsqrt(ms + EPS) * g


def _norm_low64(v, g):
    ms = jnp.sum(v * v, axis=-1, keepdims=True) * (1.0 / 64)
    return v * lax.rsqrt(ms + EPS) * g


def _load_weight_bf16(w_hbm, layer, stage, sem, w_bf):
    k, n = w_hbm.shape[1], w_hbm.shape[2]
    ch = stage.shape[1]

    def chunk_copy(c):
        return pltpu.make_async_copy(w_hbm.at[layer, pl.ds(c * ch, ch), :], stage.at[c % 2], sem.at[c % 2])

    n_pad = w_bf.shape[1]
    if n_pad > n:
        edge = (n // LANES) * LANES
        w_bf[:, edge:n_pad] = jnp.zeros((k, n_pad - edge), BF16)
    chunk_copy(0).start()
    for c in range(k // ch):
        if c + 1 < k // ch:
            chunk_copy(c + 1).start()
        chunk_copy(c).wait()
        w_bf[c * ch:(c + 1) * ch, 0:n] = stage[c % 2].astype(BF16)


def _prep_kernel(xa_ref, xb_ref, sh_ref, sc_ref, g1_ref, w_hbm, ca_ref, saa_ref, sab_ref, cb_ref, sb_ref,
                 gains_ref, gcq_ref, gckv_ref, wuq_ref, wukv_ref,
                 aq_ref, ak_ref, av_ref, bq_ref, bk_ref, bv_ref, cq_ref, ck_ref, cv_ref,
                 w_bf, w_stage, w_sem, *, n_a_tiles, layer):
    @pl.when(pl.program_id(0) == 0)
    def _():
        _load_weight_bf16(w_hbm, layer, w_stage, w_sem, w_bf)

    x = jnp.where(pl.program_id(0) < n_a_tiles, xa_ref[...], xb_ref[...])
    ms = jnp.mean(x * x, axis=-1, keepdims=True)
    h = x * lax.rsqrt(ms + EPS) * g1_ref[...]
    h = h * (1.0 + sc_ref[0]) + sh_ref[0]
    z = jnp.dot(h.astype(BF16), w_bf[...], preferred_element_type=F32)

    lane = lax.broadcasted_iota(jnp.int32, (1, LANES), 1)
    lo = lane < 64
    ca, saa, sab = ca_ref[...], saa_ref[...], sab_ref[...]
    cb, sb = cb_ref[...], sb_ref[...]
    g_aq, g_ak, g_bq, g_bk = gains_ref[0:1], gains_ref[1:2], gains_ref[2:3], gains_ref[3:4]
    g_cqn, g_cqr, g_ckn, g_ckr = gains_ref[4:5], gains_ref[5:6], gains_ref[6:7], gains_ref[7:8]

    def blk(off, j):
        return z[:, off + j * LANES: off + (j + 1) * LANES]

    for j in range(A_HEADS):
        sl = slice(j * LANES, (j + 1) * LANES)
        aq_ref[:, sl] = _rope64(_norm_seg64x2(blk(O_AQ, j), g_aq, lo), ca, saa, sab).astype(BF16)
        ak_ref[:, sl] = _rope64(_norm_seg64x2(blk(O_AK, j), g_ak, lo), ca, saa, sab).astype(BF16)
        av_ref[:, sl] = blk(O_AV, j).astype(BF16)
    for j in range(B_HEADS):
        sl = slice(j * LANES, (j + 1) * LANES)
        bq_ref[:, sl] = _rope128(_norm_seg128(blk(O_BQ, j), g_bq), cb, sb).astype(BF16)
    for j in range(B_KV_HEADS):
        sl = slice(j * LANES, (j + 1) * LANES)
        bk_ref[:, sl] = _rope128(_norm_seg128(blk(O_BK, j), g_bk), cb, sb).astype(BF16)
        bv_ref[:, sl] = blk(O_BV, j).astype(BF16)
    cq = z[:, O_CQ:O_CQ + C_Q_RANK]
    cqn = cq * lax.rsqrt(jnp.mean(cq * cq, axis=-1, keepdims=True) + EPS) * gcq_ref[...]
    q = jnp.dot(cqn.astype(BF16), wuq_ref[...], preferred_element_type=F32)
    ckv = z[:, O_CKV:O_CKV + C_KV_RANK]
    ckvn = ckv * lax.rsqrt(jnp.mean(ckv * ckv, axis=-1, keepdims=True) + EPS) * gckv_ref[...]
    kv = jnp.dot(ckvn.astype(BF16), wukv_ref[...], preferred_element_type=F32)
    krope = _rope64(_norm_low64(z[:, O_CKR:O_CKR + LANES], g_ckr), ca, saa, sab).astype(BF16)
    for hh in range(C_HEADS):
        b0 = hh * C_HEAD_PAD
        cq_ref[:, b0:b0 + LANES] = _norm_seg128(q[:, b0:b0 + LANES], g_cqn).astype(BF16)
        cq_ref[:, b0 + LANES:b0 + 2 * LANES] = _rope64(
            _norm_low64(q[:, b0 + LANES:b0 + 2 * LANES], g_cqr), ca, saa, sab).astype(BF16)
        ck_ref[:, b0:b0 + LANES] = _norm_seg128(kv[:, b0:b0 + LANES], g_ckn).astype(BF16)
        ck_ref[:, b0 + LANES:b0 + 2 * LANES] = krope
        cv_ref[:, hh * LANES:(hh + 1) * LANES] = kv[:, b0 + LANES:b0 + 2 * LANES].astype(BF16)


def _split_rows(n_a_tiles):
    first = lambda i: (jnp.minimum(i, n_a_tiles - 1), 0)
    second = lambda i: (jnp.maximum(i - n_a_tiles, 0), 0)
    return first, second


def _prep(layer, xa, xb, n_tiles, n_a_tiles, mod48, mod_row, g1, w_in, tables, rope_blk, gains, gcq, gckv,
          wuq_bf, wukv_bf):
    d = xa.shape[1]
    tm = TM_PREP
    m = n_tiles * tm
    row = lambda i: (i, 0)
    const = lambda i: (0, 0)
    first, second = _split_rows(n_a_tiles)
    tab_spec = pl.BlockSpec((tm, LANES), lambda i: (rope_blk(i), 0))
    widths = (512, 512, 512, 1024, 256, 256, C_HEADS * C_HEAD_PAD, C_HEADS * C_HEAD_PAD, 512)
    return pl.pallas_call(
        functools.partial(_prep_kernel, n_a_tiles=n_a_tiles, layer=layer),
        out_shape=[jax.ShapeDtypeStruct((m, w), BF16) for w in widths],
        grid=(n_tiles,),
        in_specs=[
            pl.BlockSpec((tm, d), first),
            pl.BlockSpec((tm, d), second),
            pl.BlockSpec((1, 1, d), lambda i: (mod_row(i) * 6 + 0, 0, 0)),
            pl.BlockSpec((1, 1, d), lambda i: (mod_row(i) * 6 + 1, 0, 0)),
            pl.BlockSpec((1, d), const),
            pl.BlockSpec(memory_space=pl.ANY),
            tab_spec, tab_spec, tab_spec, tab_spec, tab_spec,
            pl.BlockSpec((8, LANES), const),
            pl.BlockSpec((1, C_Q_RANK), const),
            pl.BlockSpec((1, C_KV_RANK), const),
            pl.BlockSpec((C_Q_RANK, C_HEADS * C_HEAD_PAD), const, pipeline_mode=pl.Buffered(1)),
            pl.BlockSpec((C_KV_RANK, C_HEADS * C_HEAD_PAD), const, pipeline_mode=pl.Buffered(1)),
        ],
        out_specs=[pl.BlockSpec((tm, w), row) for w in widths],
        scratch_shapes=[pltpu.VMEM((d, D_IN_PAD), BF16),
                        pltpu.VMEM((2, W_STAGE_ROWS, w_in.shape[2]), F32),
                        pltpu.SemaphoreType.DMA((2,))],
        compiler_params=_cparams(("arbitrary",)),
        name="prep",
    )(xa, xb, mod48, mod48, g1, w_in, *tables, gains, gcq, gckv, wuq_bf, wukv_bf)


def _dot_nt(a, b):
    return lax.dot_general(a, b, (((1,), (1,)), ((), ())), preferred_element_type=F32)


def _dot_tn(a, b):
    return lax.dot_general(a, b, (((0,), (0,)), ((), ())), preferred_element_type=F32)


def _softmax_pv_t(q, k_refs, v_refs):
    s = [_dot_nt(k[...], q) for k in k_refs]
    m = functools.reduce(jnp.maximum, [jnp.max(si, axis=0, keepdims=True) for si in s])
    e = [jnp.exp2(si - m) for si in s]
    l = functools.reduce(jnp.add, [jnp.sum(ei, axis=0, keepdims=True) for ei in e])
    o = functools.reduce(jnp.add, [_dot_tn(v[...], ei.astype(BF16)) for ei, v in zip(e, v_refs)])
    return o * (1.0 / l)


def _attn_a_kernel(*refs, n_src, lam_init):
    lv_ref, gsub_ref, q_ref = refs[0], refs[1], refs[2]
    k_refs = refs[3:3 + n_src]
    v_refs = refs[3 + n_src:3 + 2 * n_src]
    o_ref = refs[3 + 2 * n_src]
    tq = q_ref.shape[0]
    lv = lv_ref[...]
    lam = (jnp.exp(jnp.sum(lv[0:1] * lv[1:2], axis=-1, keepdims=True))
           - jnp.exp(jnp.sum(lv[2:3] * lv[3:4], axis=-1, keepdims=True)) + lam_init)
    q = q_ref[...]
    lo = lax.broadcasted_iota(jnp.int32, (1, LANES), 1) < 64
    zero = jnp.zeros_like(q)
    qq = jnp.concatenate([jnp.where(lo, q, zero), jnp.where(lo, zero, q)], axis=0)
    o2 = _softmax_pv_t(qq, k_refs, v_refs)
    o = o2[:, :tq] - lam * o2[:, tq:]
    ms = jnp.mean(o * o, axis=0, keepdims=True)
    o = (o * lax.rsqrt(ms + EPS)).T
    o_ref[...] = (o * gsub_ref[...] * (1.0 - lam_init)).astype(BF16)


def _attn_a(lv, gsub, aq, ak, av, q_row0, q_rows, srcs, n_batch, layer_idx):
    tq = min(TQ_A, q_rows)
    nq = q_rows // tq
    qb0 = q_row0 // tq
    lam_init = 0.8 - 0.6 * math.exp(-0.3 * layer_idx)
    in_specs = [
        pl.BlockSpec((4, A_DK), lambda b, h, i: (0, 0)),
        pl.BlockSpec((1, A_DV), lambda b, h, i: (0, 0)),
        pl.BlockSpec((tq, LANES), lambda b, h, i: (qb0 + b * nq + i, h)),
    ]
    kv_specs = [pl.BlockSpec((rows, LANES), lambda b, h, i, blk0=row0 // rows: (blk0 + b, h))
                for row0, rows in srcs]
    return pl.pallas_call(
        functools.partial(_attn_a_kernel, n_src=len(srcs), lam_init=lam_init),
        out_shape=jax.ShapeDtypeStruct((n_batch * q_rows, A_HEADS * A_DV), BF16),
        grid=(n_batch, A_HEADS, nq),
        in_specs=in_specs + kv_specs + kv_specs,
        out_specs=pl.BlockSpec((tq, LANES), lambda b, h, i: (b * nq + i, h)),
        compiler_params=_cparams(("arbitrary", "arbitrary", "arbitrary")),
        name="attn_a",
    )(lv, gsub, aq, *([ak] * len(srcs)), *([av] * len(srcs)))


def _attn_c_kernel(*refs, n_src):
    q_ref = refs[0]
    k_refs = refs[1:1 + n_src]
    v_refs = refs[1 + n_src:1 + 2 * n_src]
    o_ref = refs[1 + 2 * n_src]
    for hh in range(C_HEADS_PER_STEP):
        qk = slice(hh * C_HEAD_PAD, (hh + 1) * C_HEAD_PAD)
        dv = slice(hh * C_DV, (hh + 1) * C_DV)
        o = _softmax_pv_t(q_ref[:, qk], [k.at[:, qk] for k in k_refs], [v.at[:, dv] for v in v_refs])
        o_ref[:, dv] = o.T.astype(BF16)


def _attn_c(cq, ck, cv, q_row0, q_rows, srcs, n_batch):
    tq = min(TQ_C, q_rows)
    nq = q_rows // tq
    qb0 = q_row0 // tq
    hp = C_HEADS_PER_STEP
    in_specs = [pl.BlockSpec((tq, hp * C_HEAD_PAD), lambda b, h, i: (qb0 + b * nq + i, h))]
    k_specs = [pl.BlockSpec((rows, hp * C_HEAD_PAD), lambda b, h, i, blk0=row0 // rows: (blk0 + b, h))
               for row0, rows in srcs]
    v_specs = [pl.BlockSpec((rows, hp * C_DV), lambda b, h, i, blk0=row0 // rows: (blk0 + b, h))
               for row0, rows in srcs]
    return pl.pallas_call(
        functools.partial(_attn_c_kernel, n_src=len(srcs)),
        out_shape=jax.ShapeDtypeStruct((n_batch * q_rows, C_HEADS * C_DV), BF16),
        grid=(n_batch, C_HEADS // hp, nq),
        in_specs=in_specs + k_specs + v_specs,
        out_specs=pl.BlockSpec((tq, hp * C_DV), lambda b, h, i: (b * nq + i, h)),
        compiler_params=_cparams(("arbitrary", "arbitrary", "arbitrary")),
        name="attn_c",
    )(cq, *([ck] * len(srcs)), *([cv] * len(srcs)))


def _stack_heads(q):
    return jnp.concatenate([q[:, g * LANES:(g + 1) * LANES] for g in range(B_GROUP)], axis=0)


def _sink_row(sink_ref, kvh, cols):
    return jnp.concatenate(
        [jnp.full((1, cols), sink_ref[kvh * B_GROUP + g] * LOG2E, F32) for g in range(B_GROUP)], axis=1)


def _attn_b_kernel(sink_ref, q_ref, kp_ref, km_ref, kn_ref, vp_ref, vm_ref, vn_ref, kc_ref, vc_ref, o_ref):
    kvh = pl.program_id(1)
    qb = pl.program_id(2)
    nqb = pl.num_programs(2)
    n_blk = QB_B // BLOCK
    kband = jnp.concatenate([kp_ref[...], km_ref[...], kn_ref[...]], axis=0)
    vband = jnp.concatenate([vp_ref[...], vm_ref[...], vn_ref[...]], axis=0)
    kc, vc = kc_ref[...], vc_ref[...]
    sink = _sink_row(sink_ref, kvh, BLOCK)
    c = lax.broadcasted_iota(jnp.int32, (3 * BLOCK, B_GROUP * BLOCK), 0)
    r = lax.broadcasted_iota(jnp.int32, (3 * BLOCK, B_GROUP * BLOCK), 1) % BLOCK
    cr = c - r
    band_ok = (cr >= 0) & (cr <= BLOCK + WINDOW)
    for j in range(n_blk):
        q4 = _stack_heads(q_ref[j * BLOCK:(j + 1) * BLOCK, :])
        s_loc = _dot_nt(kband[j * BLOCK:(j + 3) * BLOCK], q4)
        valid = band_ok
        if j == 0:
            valid = valid & (c >= jnp.where(qb > 0, 0, BLOCK))
        if j == n_blk - 1:
            valid = valid & (c < jnp.where(qb < nqb - 1, 3 * BLOCK, 2 * BLOCK))
        s_loc = jnp.where(valid, s_loc, NEG_INF)
        s_ctx = _dot_nt(kc, q4)
        m = jnp.maximum(jnp.maximum(jnp.max(s_loc, axis=0, keepdims=True),
                                    jnp.max(s_ctx, axis=0, keepdims=True)), sink)
        e_loc, e_ctx = jnp.exp2(s_loc - m), jnp.exp2(s_ctx - m)
        l = (jnp.sum(e_loc, axis=0, keepdims=True) + jnp.sum(e_ctx, axis=0, keepdims=True)
             + jnp.exp2(sink - m))
        o = (_dot_tn(vband[j * BLOCK:(j + 3) * BLOCK], e_loc.astype(BF16))
             + _dot_tn(vc, e_ctx.astype(BF16))) * (1.0 / l)
        o = o.T
        for g in range(B_GROUP):
            o_ref[j * BLOCK:(j + 1) * BLOCK, g * LANES:(g + 1) * LANES] = (
                o[g * BLOCK:(g + 1) * BLOCK].astype(BF16))


def _attn_b(sink, bq, bk, bv, n_batch, n_lat, n_ctx):
    nqb = n_lat // QB_B
    per = QB_B // BLOCK
    blocks_per_batch = n_lat // BLOCK
    ctx_blk0 = n_batch * n_lat // n_ctx
    gw = B_GROUP * B_DH
    prev = lambda b, h, i: (b * blocks_per_batch + jnp.maximum(i * per - 1, 0), h)
    main = lambda b, h, i: (b * nqb + i, h)
    nxt = lambda b, h, i: (b * blocks_per_batch + jnp.minimum(i * per + per, blocks_per_batch - 1), h)
    ctx = lambda b, h, i: (ctx_blk0 + b, h)
    return pl.pallas_call(
        _attn_b_kernel,
        out_shape=jax.ShapeDtypeStruct((n_batch * n_lat, B_HEADS * B_DH), BF16),
        grid=(n_batch, B_KV_HEADS, nqb),
        in_specs=[
            pl.BlockSpec(memory_space=pltpu.SMEM),
            pl.BlockSpec((QB_B, gw), main),
            pl.BlockSpec((BLOCK, B_DH), prev), pl.BlockSpec((QB_B, B_DH), main), pl.BlockSpec((BLOCK, B_DH), nxt),
            pl.BlockSpec((BLOCK, B_DH), prev), pl.BlockSpec((QB_B, B_DH), main), pl.BlockSpec((BLOCK, B_DH), nxt),
            pl.BlockSpec((n_ctx, B_DH), ctx), pl.BlockSpec((n_ctx, B_DH), ctx),
        ],
        out_specs=pl.BlockSpec((QB_B, gw), main),
        compiler_params=_cparams(("arbitrary", "arbitrary", "arbitrary")),
        name="attn_b",
    )(sink, bq, bk, bk, bk, bv, bv, bv, bk, bv)


def _attn_b_ctx_kernel(sink_ref, q_ref, k_ref, v_ref, o_ref):
    kvh = pl.program_id(1)
    rows = q_ref.shape[0]
    q4 = _stack_heads(q_ref[...])
    sink = _sink_row(sink_ref, kvh, rows)
    s = _dot_nt(k_ref[...], q4)
    m = jnp.maximum(jnp.max(s, axis=0, keepdims=True), sink)
    e = jnp.exp2(s - m)
    l = jnp.sum(e, axis=0, keepdims=True) + jnp.exp2(sink - m)
    o = (_dot_tn(v_ref[...], e.astype(BF16)) * (1.0 / l)).T
    for g in range(B_GROUP):
        o_ref[:, g * LANES:(g + 1) * LANES] = o[g * rows:(g + 1) * rows].astype(BF16)


def _attn_b_ctx(sink, bq, bk, bv, n_batch, n_lat, n_ctx):
    gw = B_GROUP * B_DH
    ctx_blk0 = n_batch * n_lat // n_ctx
    return pl.pallas_call(
        _attn_b_ctx_kernel,
        out_shape=jax.ShapeDtypeStruct((n_batch * n_ctx, B_HEADS * B_DH), BF16),
        grid=(n_batch, B_KV_HEADS),
        in_specs=[
            pl.BlockSpec(memory_space=pltpu.SMEM),
            pl.BlockSpec((n_ctx, gw), lambda b, h: (ctx_blk0 + b, h)),
            pl.BlockSpec((n_ctx, B_DH), lambda b, h: (ctx_blk0 + b, h)),
            pl.BlockSpec((n_ctx, B_DH), lambda b, h: (ctx_blk0 + b, h)),
        ],
        out_specs=pl.BlockSpec((n_ctx, gw), lambda b, h: (b, h)),
        compiler_params=_cparams(("arbitrary", "arbitrary")),
        name="attn_b_ctx",
    )(sink, bq, bk, bv)


def _route(h2, rw_ref, rb_ref):
    tm = h2.shape[0]
    h_hi = h2.astype(BF16)
    h_lo = (h2 - h_hi.astype(F32)).astype(BF16)
    p = jnp.dot(jnp.concatenate([h_hi, h_lo], axis=0), rw_ref[...], preferred_element_type=F32)
    p = p[:tm] + p[tm:]
    logits = p[:, :N_EXPERTS] + p[:, N_EXPERTS:]
    scores = 1.0 / (1.0 + jnp.exp(-logits))
    sel = scores + rb_ref[...]
    lane_i = lax.broadcasted_iota(jnp.int32, sel.shape, 1)
    lane = lane_i.astype(F32)
    big = float(N_EXPERTS)

    def top2(mask):
        v = jnp.where(mask, sel, -jnp.inf)
        m1 = jnp.max(v, axis=-1, keepdims=True)
        i1 = jnp.min(jnp.where(v == m1, lane, big), axis=-1, keepdims=True)
        v2 = jnp.where(lane == i1, -jnp.inf, v)
        m2 = jnp.max(v2, axis=-1, keepdims=True)
        i2 = jnp.min(jnp.where(v2 == m2, lane, big), axis=-1, keepdims=True)
        return m1, i1, m2, i2

    best = None
    for g in range(N_GROUPS):
        m1, i1, m2, i2 = top2((lane_i >= g * EXPERTS_PER_GROUP) & (lane_i < (g + 1) * EXPERTS_PER_GROUP))
        gs = m1 + m2
        if best is None:
            best = (gs, i1, i2)
        else:
            take = gs > best[0]
            best = (jnp.where(take, gs, best[0]), jnp.where(take, i1, best[1]), jnp.where(take, i2, best[2]))
    _, e1, e2 = best
    w1 = jnp.sum(jnp.where(lane == e1, scores, 0.0), axis=-1, keepdims=True)
    w2 = jnp.sum(jnp.where(lane == e2, scores, 0.0), axis=-1, keepdims=True)
    tot = w1 + w2
    return e1, e2, w1 / tot, w2 / tot


def _outproj_kernel(oa1_ref, ob1_ref, oc1_ref, x1in_ref, oa2_ref, ob2_ref, oc2_ref, x2in_ref,
                    g1_ref, sh2_ref, sc2_ref, n2_ref, w_hbm, rw_ref, rb_ref,
                    x1_ref, h2_ref, idx_ref, wt_ref, cnt_ref, w_ref, w_stage, w_sem, cnt_acc,
                    *, n_a_tiles, layer):
    @pl.when(pl.program_id(0) == 0)
    def _():
        _load_weight_bf16(w_hbm, layer, w_stage, w_sem, w_ref)

    first = pl.program_id(0) < n_a_tiles
    oa = jnp.where(first, oa1_ref[...], oa2_ref[...])
    ob = jnp.where(first, ob1_ref[...], ob2_ref[...])
    oc = jnp.where(first, oc1_ref[...], oc2_ref[...])
    x = jnp.where(first, x1in_ref[...], x2in_ref[...])
    y = jnp.dot(jnp.concatenate([oa, ob, oc], axis=1), w_ref[...], preferred_element_type=F32)
    x1 = x + g1_ref[0] * y
    x1_ref[...] = x1
    ms = jnp.mean(x1 * x1, axis=-1, keepdims=True)
    h2 = x1 * lax.rsqrt(ms + EPS) * n2_ref[...]
    h2 = h2 * (1.0 + sc2_ref[0]) + sh2_ref[0]
    tm = h2.shape[0]
    for cc in range(ROW_CHUNKS):
        h2_ref[pl.ds(cc, tm, stride=ROW_CHUNKS), :] = h2[:, cc * LANES:(cc + 1) * LANES]
    e1, e2, w1, w2 = _route(h2, rw_ref, rb_ref)
    lane = lax.broadcasted_iota(jnp.int32, idx_ref.shape, 1)
    wt_ref[...] = jnp.where(lane == 0, w1, jnp.where(lane == 1, w2, 0.0))

    @pl.when(pl.program_id(0) == 0)
    def _():
        cnt_acc[...] = jnp.zeros_like(cnt_acc)

    lane_f = lane.astype(F32)
    sel1, sel2 = lane_f == e1, lane_f == e2
    picks = jnp.where(sel1, 1.0, 0.0) + jnp.where(sel2, 1.0, 0.0)
    t_row = lax.broadcasted_iota(jnp.int32, (tm, tm), 0)
    t_col = lax.broadcasted_iota(jnp.int32, (tm, tm), 1)
    earlier = jnp.where(t_col < t_row, 1.0, 0.0).astype(BF16)
    before = jnp.dot(earlier, picks.astype(BF16), preferred_element_type=F32) + cnt_acc[0:1, :]
    r1 = jnp.sum(jnp.where(sel1, before, 0.0), axis=-1, keepdims=True)
    r2 = jnp.sum(jnp.where(sel2, before, 0.0), axis=-1, keepdims=True)
    idx_ref[...] = jnp.where(lane == 0, e1, jnp.where(lane == 1, e2, jnp.where(
        lane == 2, r1, jnp.where(lane == 3, r2, 0.0)))).astype(jnp.int32)
    cnt_acc[...] = cnt_acc[...] + jnp.sum(picks, axis=0, keepdims=True)
    cnt_ref[...] = cnt_acc[...]


def _outproj(layer, set_a, set_b, n_tiles, n_a_tiles, mod48, mod_row, n2, w_out, router_w, router_b):
    d = set_a[3].shape[1]
    tm = TM_PREP
    m = n_tiles * tm
    row = lambda i: (i, 0)
    const = lambda i: (0, 0)
    first, second = _split_rows(n_a_tiles)
    mod_spec = lambda j: pl.BlockSpec((1, 1, d), lambda i: (mod_row(i) * 6 + j, 0, 0))
    in_specs = (
        [pl.BlockSpec((tm, a.shape[1]), first) for a in set_a]
        + [pl.BlockSpec((tm, a.shape[1]), second) for a in set_b]
        + [mod_spec(2), mod_spec(3), mod_spec(4),
           pl.BlockSpec((1, d), const),
           pl.BlockSpec(memory_space=pl.ANY),
           pl.BlockSpec((d, 2 * N_EXPERTS), const),
           pl.BlockSpec((1, N_EXPERTS), const)])
    return pl.pallas_call(
        functools.partial(_outproj_kernel, n_a_tiles=n_a_tiles, layer=layer),
        out_shape=[jax.ShapeDtypeStruct((m, d), F32), jax.ShapeDtypeStruct((m * ROW_CHUNKS, LANES), F32),
                   jax.ShapeDtypeStruct((m, LANES), jnp.int32), jax.ShapeDtypeStruct((m, LANES), F32),
                   jax.ShapeDtypeStruct((8, LANES), F32)],
        grid=(n_tiles,),
        in_specs=in_specs,
        out_specs=[pl.BlockSpec((tm, d), row), pl.BlockSpec((tm * ROW_CHUNKS, LANES), row),
                   pl.BlockSpec((tm, LANES), row), pl.BlockSpec((tm, LANES), row),
                   pl.BlockSpec((8, LANES), const)],
        scratch_shapes=[pltpu.VMEM((d, d), BF16),
                        pltpu.VMEM((2, W_STAGE_ROWS, d), F32),
                        pltpu.SemaphoreType.DMA((2,)),
                        pltpu.VMEM((8, LANES), F32)],
        compiler_params=_cparams(("arbitrary",)),
        name="outproj",
    )(*set_a, *set_b, mod48, mod48, mod48, n2, w_out, router_w, router_b)


def _experts_kernel(te_ref, nxt_ref, ws_ref, nv_ref, src_ref, h2_hbm, wg_hbm, wu_hbm, wd_hbm, o_ref,
                    xbuf, xsem, wg_f, wu_f, wd_f, wsem, wg_bf, wu_bf, wd_bf, *, layer):
    i = pl.program_id(0)
    n_valid = nv_ref[0]
    slot = i % X_SLOTS

    def issue_rows(tile):
        s = tile % X_SLOTS
        base = tile * TM_E
        for r in range(TM_E):
            row0 = pl.multiple_of(src_ref[base + r] * ROW_CHUNKS, ROW_CHUNKS)
            pltpu.make_async_copy(h2_hbm.at[pl.ds(row0, ROW_CHUNKS), :],
                                  xbuf.at[s, pl.ds(r * X_PITCH, ROW_CHUNKS), :], xsem.at[s]).start()

    def wait_rows(s):
        pltpu.make_async_copy(h2_hbm.at[pl.ds(0, TM_E * ROW_CHUNKS), :],
                              xbuf.at[s, pl.ds(0, TM_E * ROW_CHUNKS), :], xsem.at[s]).wait()

    def weight_copies(e, s):
        return (pltpu.make_async_copy(wg_hbm.at[layer, e], wg_f.at[s], wsem.at[s]),
                pltpu.make_async_copy(wu_hbm.at[layer, e], wu_f.at[s], wsem.at[s]),
                pltpu.make_async_copy(wd_hbm.at[layer, e], wd_f.at[s], wsem.at[s]))

    @pl.when(i == 0)
    def _():
        for cp in weight_copies(te_ref[0], ws_ref[0]):
            cp.start()
        for t in range(X_SLOTS - 1):
            issue_rows(t)

    first_of_expert = jnp.logical_or(i == 0, te_ref[i] != te_ref[jnp.maximum(i - 1, 0)])

    @pl.when(jnp.logical_and(first_of_expert, i < n_valid))
    def _():
        ws = ws_ref[i]
        for cp in weight_copies(te_ref[i], ws):
            cp.wait()

        @pl.when(nxt_ref[i] >= 0)
        def _():
            for cp in weight_copies(nxt_ref[i], 1 - ws):
                cp.start(priority=WEIGHT_DMA_PRIORITY)

        wg_bf[...] = wg_f[ws].astype(BF16)
        wu_bf[...] = wu_f[ws].astype(BF16)
        wd_bf[...] = wd_f[ws].astype(BF16)

    @pl.when(i < n_valid)
    def _():
        wait_rows(slot)
        x = jnp.concatenate([xbuf[slot, pl.ds(cc, TM_E, stride=X_PITCH), :] for cc in range(ROW_CHUNKS)],
                            axis=1).astype(BF16)
        issue_rows(i + X_SLOTS - 1)
        g = jnp.dot(x, wg_bf[...], preferred_element_type=F32)
        u = jnp.dot(x, wu_bf[...], preferred_element_type=F32)
        a = (_silu(g) * u).astype(BF16)
        y = jnp.dot(a, wd_bf[...], preferred_element_type=F32)
        for cc in range(ROW_CHUNKS):
            o_ref[pl.ds(cc, TM_E, stride=ROW_CHUNKS), :] = y[:, cc * LANES:(cc + 1) * LANES]

    @pl.when(i == n_valid - 1)
    def _():
        for t in range(1, X_SLOTS):
            wait_rows((i + t) % X_SLOTS)

    @pl.when(i >= n_valid)
    def _():
        o_ref[...] = jnp.zeros_like(o_ref)


def _experts(layer, tile_expert, next_expert, w_slot, n_valid, src, h2_all, w_gate, w_up, w_down):
    n_rows = src.shape[0] - (X_SLOTS - 1) * TM_E
    nt = n_rows // TM_E
    d, de = w_gate.shape[2], w_gate.shape[3]
    grid_spec = pltpu.PrefetchScalarGridSpec(
        num_scalar_prefetch=5,
        grid=(nt,),
        in_specs=[pl.BlockSpec(memory_space=pl.ANY)] * 4,
        out_specs=pl.BlockSpec((TM_E * ROW_CHUNKS, LANES), lambda i, *_: (i, 0)),
        scratch_shapes=[
            pltpu.VMEM((X_SLOTS, TM_E * X_PITCH, LANES), F32),
            pltpu.SemaphoreType.DMA((X_SLOTS,)),
            pltpu.VMEM((2, d, de), F32), pltpu.VMEM((2, d, de), F32), pltpu.VMEM((2, de, d), F32),
            pltpu.SemaphoreType.DMA((2,)),
            pltpu.VMEM((d, de), BF16), pltpu.VMEM((d, de), BF16), pltpu.VMEM((de, d), BF16),
        ],
    )
    return pl.pallas_call(
        functools.partial(_experts_kernel, layer=layer),
        out_shape=jax.ShapeDtypeStruct((n_rows * ROW_CHUNKS, LANES), F32),
        grid_spec=grid_spec,
        compiler_params=_cparams(("arbitrary",)),
        name="experts",
    )(tile_expert, next_expert, w_slot, n_valid, src, h2_all, w_gate, w_up, w_down)


def _combine_kernel(pos_ref, o_hbm, x1_ref, wt_ref, g2_ref, x2_ref, buf, sem):
    i = pl.program_id(0)
    nt = pl.num_programs(0)
    slot = i % 2

    def issue(tile, s):
        base = tile * (TM_C * TOP_K)
        for r in range(TM_C):
            for k in range(TOP_K):
                row0 = pl.multiple_of(pos_ref[base + TOP_K * r + k] * ROW_CHUNKS, ROW_CHUNKS)
                pltpu.make_async_copy(o_hbm.at[pl.ds(row0, ROW_CHUNKS), :],
                                      buf.at[s, k, pl.ds(r * X_PITCH, ROW_CHUNKS), :], sem.at[s]
                                      ).start(priority=k % DMA_QUEUES)

    @pl.when(i == 0)
    def _():
        issue(0, 0)

    @pl.when(i + 1 < nt)
    def _():
        issue(i + 1, 1 - slot)

    def expert_rows(k):
        return jnp.concatenate([buf[slot, k, pl.ds(cc, TM_C, stride=X_PITCH), :] for cc in range(ROW_CHUNKS)],
                               axis=1)

    for k in range(TOP_K):
        pltpu.make_async_copy(o_hbm.at[pl.ds(0, TM_C * ROW_CHUNKS), :],
                              buf.at[slot, k, pl.ds(0, TM_C * ROW_CHUNKS), :], sem.at[slot]).wait()
    wt = wt_ref[...]
    y = wt[:, 0:1] * expert_rows(0) + wt[:, 1:2] * expert_rows(1)
    x2_ref[...] = x1_ref[...] + g2_ref[0] * y


def _combine(pos, o_sorted, x1, wts, mod48, mod_row):
    m, d = x1.shape
    grid_spec = pltpu.PrefetchScalarGridSpec(
        num_scalar_prefetch=1,
        grid=(m // TM_C,),
        in_specs=[
            pl.BlockSpec(memory_space=pl.ANY),
            pl.BlockSpec((TM_C, d), lambda i, p: (i, 0)),
            pl.BlockSpec((TM_C, LANES), lambda i, p: (i, 0)),
            pl.BlockSpec((1, 1, d), lambda i, p: (mod_row(i) * 6 + 5, 0, 0)),
        ],
        out_specs=pl.BlockSpec((TM_C, d), lambda i, p: (i, 0)),
        scratch_shapes=[pltpu.VMEM((2, TOP_K, TM_C * X_PITCH, LANES), F32), pltpu.SemaphoreType.DMA((2,))],
    )
    return pl.pallas_call(
        _combine_kernel,
        out_shape=jax.ShapeDtypeStruct((m, d), F32),
        grid_spec=grid_spec,
        compiler_params=_cparams(("arbitrary",)),
        name="combine",
    )(pos, o_sorted, x1, wts, mod48)


def _rope_tables(n_lat, tm):
    f32 = np.float32
    t = np.arange(n_lat)
    r = (t // GRID_W).astype(f32)
    col = (t % GRID_W).astype(f32)

    def cos_sin(dim):
        nf = dim // 4
        inv = (f32(ROPE_THETA) ** (-np.arange(nf, dtype=f32) / f32(nf))).astype(f32)
        ang = np.concatenate([r[:, None] * inv, col[:, None] * inv], axis=-1).astype(f32)
        return np.cos(ang).astype(f32), np.sin(ang).astype(f32)

    c64, s64 = cos_sin(A_DK)
    c128, s128 = cos_sin(B_DH)
    z32 = np.zeros_like(s64)
    tabs = [
        np.concatenate([c64, c64, c64, c64], axis=-1),
        np.concatenate([-s64, z32, -s64, z32], axis=-1),
        np.concatenate([z32, s64, z32, s64], axis=-1),
        np.concatenate([c128, c128], axis=-1),
        np.concatenate([-s128, s128], axis=-1),
    ]
    ident = [np.ones((tm, LANES), f32), np.zeros((tm, LANES), f32), np.zeros((tm, LANES), f32),
             np.ones((tm, LANES), f32), np.zeros((tm, LANES), f32)]
    return [jnp.asarray(np.concatenate([a, b], axis=0)) for a, b in zip(tabs, ident)]


def _layer_params(l, a_qn, a_kn, b_qn, b_kn, c_qa_norm, c_kva_norm, c_wuq, c_wukv, c_qn, c_kn):
    z64 = jnp.zeros((C_ROPE,), F32)
    gains = jnp.stack([
        jnp.tile(a_qn[l], 2) * (A_DK ** -0.5 * LOG2E),
        jnp.tile(a_kn[l], 2),
        b_qn[l] * (B_DH ** -0.5 * LOG2E),
        b_kn[l],
        c_qn[l][:C_NOPE] * (C_DQK ** -0.5 * LOG2E),
        jnp.concatenate([c_qn[l][C_NOPE:] * (C_DQK ** -0.5 * LOG2E), z64]),
        c_kn[l][:C_NOPE],
        jnp.concatenate([c_kn[l][C_NOPE:], z64]),
    ])
    wq = c_wuq[l].reshape(C_Q_RANK, C_HEADS, C_DQK)
    wq = jnp.pad(wq, ((0, 0), (0, 0), (0, C_HEAD_PAD - C_DQK))).reshape(C_Q_RANK, C_HEADS * C_HEAD_PAD)
    return dict(gains=gains, gcq=c_qa_norm[l][None], gckv=c_kva_norm[l][None],
                wuq=wq.astype(BF16), wukv=c_wukv[l].astype(BF16))


def _sorted_rows(idx, rank, counts):
    t = idx.shape[0]
    n_pairs = t * TOP_K
    n_rows = ((n_pairs + N_EXPERTS * (TM_E - 1)) // TM_E) * TM_E
    nt = n_rows // TM_E
    flat_e = idx.reshape(-1)
    rank = rank.reshape(-1)
    padded = ((counts + TM_E - 1) // TM_E) * TM_E
    ends = jnp.cumsum(padded)
    pos = (ends - padded)[flat_e] + rank
    src = jnp.zeros((n_rows + (X_SLOTS - 1) * TM_E,), jnp.int32).at[pos].set(
        jnp.arange(n_pairs, dtype=jnp.int32) // TOP_K, unique_indices=True)
    tile_start = jnp.arange(nt, dtype=jnp.int32) * TM_E
    tile_expert = jnp.sum((ends[None, :] <= tile_start[:, None]).astype(jnp.int32), axis=1)
    last_used = jnp.sum((ends <= ends[-1] - 1).astype(jnp.int32))
    tile_expert = jnp.minimum(tile_expert, last_used)
    n_valid = (ends[-1:] // TM_E).astype(jnp.int32)
    used = counts > 0
    e_ids = jnp.arange(N_EXPERTS, dtype=jnp.int32)
    later = jnp.where(used[None, :] & (e_ids[None, :] > e_ids[:, None]), e_ids[None, :], N_EXPERTS)
    next_used = jnp.min(later, axis=1)
    next_used = jnp.where(next_used == N_EXPERTS, -1, next_used).astype(jnp.int32)
    parity = ((jnp.cumsum(used.astype(jnp.int32)) - 1) % 2).astype(jnp.int32)
    return (pos.astype(jnp.int32), src, tile_expert.astype(jnp.int32), next_used[tile_expert],
            parity[tile_expert], n_valid)


def kernel(x, c, ctx, c_ctx, ada_w, ada_b, norm1_g, norm2_g, w_in, w_out, a_qn, a_kn, a_lambda, a_subln,
           b_qn, b_kn, b_sink, c_qa_norm, c_kva_norm, c_wuq, c_wukv, c_qn, c_kn,
           router_w, router_bias, moe_w_gate, moe_w_up, moe_w_down):
    bsz, n_lat, d = x.shape
    n_ctx = ctx.shape[1]
    depth = ada_w.shape[0]
    t_lat, t_ctx = bsz * n_lat, bsz * n_ctx
    tm = TM_PREP
    lat_tiles = n_lat // tm
    n_lat_tiles, n_ctx_tiles = t_lat // tm, t_ctx // tm
    n_all_tiles = n_lat_tiles + n_ctx_tiles

    cond8 = jnp.concatenate([c, c_ctx[None], jnp.zeros((8 - bsz - 1, d), F32)], axis=0)
    mod = _ada_modulation(cond8, ada_w, ada_b)
    tables = _rope_tables(n_lat, tm)
    rb = router_bias[None]
    rw_hi = router_w.astype(BF16)
    rw_lo = (router_w - rw_hi.astype(F32)).astype(BF16)
    rw2 = jnp.concatenate([rw_hi, rw_lo], axis=1)

    mod_row = lambda i: jnp.minimum(i // lat_tiles, bsz)
    mod_row_c = lambda i: jnp.minimum(i // (n_lat // TM_C), bsz)
    rope_blk = lambda i: jnp.where(i < n_lat_tiles, i % lat_tiles, lat_tiles)

    xa, xb, n_a_tiles = x.reshape(t_lat, d), ctx.reshape(t_ctx, d), n_lat_tiles
    lat_src, ctx_src = (0, n_lat), (t_lat, n_ctx)
    for l in range(depth):
        last = l == depth - 1
        p = _layer_params(l, a_qn, a_kn, b_qn, b_kn, c_qa_norm, c_kva_norm, c_wuq, c_wukv, c_qn, c_kn)
        mod48 = mod[l].reshape(8 * 6, 1, d)
        aq, ak, av, bq, bk, bv, cq, ck, cv = _prep(
            l, xa, xb, n_all_tiles, n_a_tiles, mod48, mod_row, norm1_g[l][None], w_in, tables, rope_blk,
            p["gains"], p["gcq"], p["gckv"], p["wuq"], p["wukv"])

        lv, gsub, sink = a_lambda[l], a_subln[l][None], b_sink[l]
        o_a = _attn_a(lv, gsub, aq, ak, av, 0, n_lat, [lat_src, ctx_src], bsz, l)
        o_b = _attn_b(sink, bq, bk, bv, bsz, n_lat, n_ctx)
        o_c = _attn_c(cq, ck, cv, 0, n_lat, [lat_src, ctx_src], bsz)
        set_a = (o_a, o_b, o_c, xa)
        if not last:
            oc_a = _attn_a(lv, gsub, aq, ak, av, t_lat, n_ctx, [ctx_src], bsz, l)
            oc_b = _attn_b_ctx(sink, bq, bk, bv, bsz, n_lat, n_ctx)
            oc_c = _attn_c(cq, ck, cv, t_lat, n_ctx, [ctx_src], bsz)
            set_b, n_tok_tiles = (oc_a, oc_b, oc_c, xb), n_all_tiles
        else:
            set_b, n_tok_tiles = set_a, n_lat_tiles
        x1, h2, idx, wts, cnt = _outproj(l, set_a, set_b, n_tok_tiles, min(n_a_tiles, n_tok_tiles), mod48,
                                         mod_row, norm2_g[l][None], w_out, rw2, rb)

        pos, src, tile_expert, next_expert, w_slot, n_valid = _sorted_rows(
            idx[:, :TOP_K], idx[:, TOP_K:2 * TOP_K], cnt[0, :N_EXPERTS].astype(jnp.int32))
        o_sorted = _experts(l, tile_expert, next_expert, w_slot, n_valid, src, h2,
                            moe_w_gate, moe_w_up, moe_w_down)
        xa = _combine(pos, o_sorted, x1, wts, mod48, mod_row_c)
        xb, n_a_tiles = xa, n_all_tiles
    return xa.reshape(bsz, n_lat, d)
```

```python
import functools
import math

import jax
import jax.numpy as jnp
import numpy as np
from jax import lax
from jax.experimental import pallas as pl
from jax.experimental.pallas import tpu as pltpu

F32 = jnp.float32
BF16 = jnp.bfloat16

D_MODEL = 2048
GRID_W = 64
BLOCK = 128
WINDOW = 128
ROPE_THETA = 10000.0
EPS = 1e-6
NEG_INF = -1e30
LOG2E = math.log2(math.e)
A_HEADS, A_DK = 4, 64
A_DV = 2 * A_DK
B_HEADS, B_KV_HEADS, B_DH = 8, 2, 128
B_GROUP = B_HEADS // B_KV_HEADS
C_HEADS, C_Q_RANK, C_KV_RANK, C_NOPE, C_ROPE, C_DV = 4, 512, 256, 128, 64, 128
C_DQK = C_NOPE + C_ROPE
SPLIT_SIZES = (A_HEADS * 2 * A_DK, A_HEADS * 2 * A_DK, A_HEADS * A_DV,
               B_HEADS * B_DH, B_KV_HEADS * B_DH, B_KV_HEADS * B_DH,
               C_Q_RANK, C_KV_RANK, C_ROPE)
D_IN = sum(SPLIT_SIZES)
N_EXPERTS, N_GROUPS, TOP_K = 32, 4, 2
EXPERTS_PER_GROUP = N_EXPERTS // N_GROUPS
D_EXPERT = 512

LANES = 128
V7X_VMEM_LIMIT = 56 * 1024 * 1024

D_IN_PAD = ((D_IN + LANES - 1) // LANES) * LANES
C_HEAD_PAD = 2 * LANES
TM_PREP = 256
TQ_A = 512
TQ_C = 1024
C_HEADS_PER_STEP = 1
QB_B = 512
TM_E = 128
X_SLOTS = 3
ROW_CHUNKS = D_MODEL // LANES
X_PITCH = ROW_CHUNKS + 8
DMA_QUEUES = 2
WEIGHT_DMA_PRIORITY = 1
TM_C = 128
ADA_TN = 1024
W_STAGE_ROWS = 128

_OFF = [0]
for _s in SPLIT_SIZES:
    _OFF.append(_OFF[-1] + _s)
O_AQ, O_AK, O_AV, O_BQ, O_BK, O_BV, O_CQ, O_CKV, O_CKR, _ = _OFF


def _cparams(sem):
    return pltpu.CompilerParams(dimension_semantics=sem, vmem_limit_bytes=V7X_VMEM_LIMIT)


def _silu(v):
    return v * (1.0 / (1.0 + jnp.exp(-v)))


def _ada_kernel(cond_ref, w_ref, b_ref, o_ref):
    s = _silu(cond_ref[...]).astype(BF16)
    o_ref[0] = jnp.dot(s, w_ref[0].astype(BF16), preferred_element_type=F32) + b_ref[0]


def _ada_modulation(cond8, ada_w, ada_b):
    depth, d, n = ada_w.shape
    return pl.pallas_call(
        _ada_kernel,
        out_shape=jax.ShapeDtypeStruct((depth, 8, n), F32),
        grid=(depth, n // ADA_TN),
        in_specs=[
            pl.BlockSpec((8, d), lambda l, j: (0, 0)),
            pl.BlockSpec((1, d, ADA_TN), lambda l, j: (l, 0, j)),
            pl.BlockSpec((1, 1, ADA_TN), lambda l, j: (l, 0, j)),
        ],
        out_specs=pl.BlockSpec((1, 8, ADA_TN), lambda l, j: (l, 0, j)),
        compiler_params=_cparams(("arbitrary", "arbitrary")),
        name="ada_modulation",
    )(cond8, ada_w, ada_b.reshape(depth, 1, n))


def _rope64(v, c, sa, sb):
    return v * c + pltpu.roll(v, 96, 1) * sa + pltpu.roll(v, 32, 1) * sb


def _rope128(v, c, s):
    return v * c + pltpu.roll(v, 64, 1) * s


def _norm_seg128(v, g):
    ms = jnp.sum(v * v, axis=-1, keepdims=True) * (1.0 / 128)
    return v * lax.rsqrt(ms + EPS) * g


def _norm_seg64x2(v, g, lo):
    sq = v * v
    s_lo = jnp.sum(jnp.where(lo, sq, 0.0), axis=-1, keepdims=True)
    s_hi = jnp.sum(jnp.where(lo, 0.0, sq), axis=-1, keepdims=True)
    ms = jnp.where(lo, s_lo, s_hi) * (1.0 / 64)
    return v * lax.rsqrt(ms + EPS) * g


def _norm_low64(v, g):
    ms = jnp.sum(v * v, axis=-1, keepdims=True) * (1.0 / 64)
    return v * lax.rsqrt(ms + EPS) * g


def _load_weight_bf16(w_hbm, layer, stage, sem, w_bf):
    k, n = w_hbm.shape[1], w_hbm.shape[2]
    ch = stage.shape[1]

    def chunk_copy(c):
        return pltpu.make_async_copy(w_hbm.at[layer, pl.ds(c * ch, ch), :], stage.at[c % 2], sem.at[c % 2])

    n_pad = w_bf.shape[1]
    if n_pad > n:
        edge = (n // LANES) * LANES
        w_bf[:, edge:n_pad] = jnp.zeros((k, n_pad - edge), BF16)
    chunk_copy(0).start()
    for c in range(k // ch):
        if c + 1 < k // ch:
            chunk_copy(c + 1).start()
        chunk_copy(c).wait()
        w_bf[c * ch:(c + 1) * ch, 0:n] = stage[c % 2].astype(BF16)


def _prep_kernel(xa_ref, xb_ref, sh_ref, sc_ref, g1_ref, w_hbm, ca_ref, saa_ref, sab_ref, cb_ref, sb_ref,
                 gains_ref, gcq_ref, gckv_ref, wuq_ref, wukv_ref,
                 aq_ref, ak_ref, av_ref, bq_ref, bk_ref, bv_ref, cq_ref, ck_ref, cv_ref,
                 w_bf, w_stage, w_sem, *, n_a_tiles, layer):
    @pl.when(pl.program_id(0) == 0)
    def _():
        _load_weight_bf16(w_hbm, layer, w_stage, w_sem, w_bf)

    x = jnp.where(pl.program_id(0) < n_a_tiles, xa_ref[...], xb_ref[...])
    _prep_core(x, sh_ref, sc_ref, g1_ref, w_bf, ca_ref, saa_ref, sab_ref, cb_ref, sb_ref,
               gains_ref, gcq_ref, gckv_ref, wuq_ref, wukv_ref,
               aq_ref, ak_ref, av_ref, bq_ref, bk_ref, bv_ref, cq_ref, ck_ref, cv_ref)


def _prep_combine_kernel(pos_ref, x1_ref, wt_ref, g2_ref, o_hbm,
                         sh_ref, sc_ref, g1_ref, w_hbm, ca_ref, saa_ref, sab_ref, cb_ref, sb_ref,
                         gains_ref, gcq_ref, gckv_ref, wuq_ref, wukv_ref,
                         x2_ref, aq_ref, ak_ref, av_ref, bq_ref, bk_ref, bv_ref, cq_ref, ck_ref, cv_ref,
                         w_bf, w_stage, w_sem, gbuf, gsem, *, layer):
    i = pl.program_id(0)
    nt = pl.num_programs(0)
    slot = i % 2
    tm = x1_ref.shape[0]

    def issue(tile, s):
        base = tile * (tm * TOP_K)
        for r in range(tm):
            for k in range(TOP_K):
                row0 = pl.multiple_of(pos_ref[base + TOP_K * r + k] * ROW_CHUNKS, ROW_CHUNKS)
                pltpu.make_async_copy(o_hbm.at[pl.ds(row0, ROW_CHUNKS), :],
                                      gbuf.at[s, k, pl.ds(r * X_PITCH, ROW_CHUNKS), :], gsem.at[s]
                                      ).start(priority=k % DMA_QUEUES)

    def wait(s):
        for k in range(TOP_K):
            pltpu.make_async_copy(o_hbm.at[pl.ds(0, tm * ROW_CHUNKS), :],
                                  gbuf.at[s, k, pl.ds(0, tm * ROW_CHUNKS), :], gsem.at[s]).wait()

    @pl.when(i == 0)
    def _():
        _load_weight_bf16(w_hbm, layer, w_stage, w_sem, w_bf)
        issue(0, 0)

    wait(slot)

    def expert_rows(k):
        return jnp.concatenate([gbuf[slot, k, pl.ds(cc, tm, stride=X_PITCH), :] for cc in range(ROW_CHUNKS)],
                               axis=1)

    rows0, rows1 = expert_rows(0), expert_rows(1)
    issue(i + 1, 1 - slot)
    wt = wt_ref[...]
    x = x1_ref[...] + g2_ref[0] * (wt[:, 0:1] * rows0 + wt[:, 1:2] * rows1)
    x2_ref[...] = x
    _prep_core(x, sh_ref, sc_ref, g1_ref, w_bf, ca_ref, saa_ref, sab_ref, cb_ref, sb_ref,
               gains_ref, gcq_ref, gckv_ref, wuq_ref, wukv_ref,
               aq_ref, ak_ref, av_ref, bq_ref, bk_ref, bv_ref, cq_ref, ck_ref, cv_ref)

    @pl.when(i == nt - 1)
    def _():
        wait(1 - slot)


def _prep_core(x, sh_ref, sc_ref, g1_ref, w_bf, ca_ref, saa_ref, sab_ref, cb_ref, sb_ref,
               gains_ref, gcq_ref, gckv_ref, wuq_ref, wukv_ref,
               aq_ref, ak_ref, av_ref, bq_ref, bk_ref, bv_ref, cq_ref, ck_ref, cv_ref):
    ms = jnp.mean(x * x, axis=-1, keepdims=True)
    h = x * lax.rsqrt(ms + EPS) * g1_ref[...]
    h = h * (1.0 + sc_ref[0]) + sh_ref[0]
    z = jnp.dot(h.astype(BF16), w_bf[...], preferred_element_type=F32)

    lane = lax.broadcasted_iota(jnp.int32, (1, LANES), 1)
    lo = lane < 64
    ca, saa, sab = ca_ref[...], saa_ref[...], sab_ref[...]
    cb, sb = cb_ref[...], sb_ref[...]
    g_aq, g_ak, g_bq, g_bk = gains_ref[0:1], gains_ref[1:2], gains_ref[2:3], gains_ref[3:4]
    g_cqn, g_cqr, g_ckn, g_ckr = gains_ref[4:5], gains_ref[5:6], gains_ref[6:7], gains_ref[7:8]

    def blk(off, j):
        return z[:, off + j * LANES: off + (j + 1) * LANES]

    for j in range(A_HEADS):
        sl = slice(j * LANES, (j + 1) * LANES)
        aq_ref[:, sl] = _rope64(_norm_seg64x2(blk(O_AQ, j), g_aq, lo), ca, saa, sab).astype(BF16)
        ak_ref[:, sl] = _rope64(_norm_seg64x2(blk(O_AK, j), g_ak, lo), ca, saa, sab).astype(BF16)
        av_ref[:, sl] = blk(O_AV, j).astype(BF16)
    for j in range(B_HEADS):
        sl = slice(j * LANES, (j + 1) * LANES)
        bq_ref[:, sl] = _rope128(_norm_seg128(blk(O_BQ, j), g_bq), cb, sb).astype(BF16)
    for j in range(B_KV_HEADS):
        sl = slice(j * LANES, (j + 1) * LANES)
        bk_ref[:, sl] = _rope128(_norm_seg128(blk(O_BK, j), g_bk), cb, sb).astype(BF16)
        bv_ref[:, sl] = blk(O_BV, j).astype(BF16)
    cq = z[:, O_CQ:O_CQ + C_Q_RANK]
    cqn = cq * lax.rsqrt(jnp.mean(cq * cq, axis=-1, keepdims=True) + EPS) * gcq_ref[...]
    q = jnp.dot(cqn.astype(BF16), wuq_ref[...], preferred_element_type=F32)
    ckv = z[:, O_CKV:O_CKV + C_KV_RANK]
    ckvn = ckv * lax.rsqrt(jnp.mean(ckv * ckv, axis=-1, keepdims=True) + EPS) * gckv_ref[...]
    kv = jnp.dot(ckvn.astype(BF16), wukv_ref[...], preferred_element_type=F32)
    krope = _rope64(_norm_low64(z[:, O_CKR:O_CKR + LANES], g_ckr), ca, saa, sab).astype(BF16)
    for hh in range(C_HEADS):
        b0 = hh * C_HEAD_PAD
        cq_ref[:, b0:b0 + LANES] = _norm_seg128(q[:, b0:b0 + LANES], g_cqn).astype(BF16)
        cq_ref[:, b0 + LANES:b0 + 2 * LANES] = _rope64(
            _norm_low64(q[:, b0 + LANES:b0 + 2 * LANES], g_cqr), ca, saa, sab).astype(BF16)
        ck_ref[:, b0:b0 + LANES] = _norm_seg128(kv[:, b0:b0 + LANES], g_ckn).astype(BF16)
        ck_ref[:, b0 + LANES:b0 + 2 * LANES] = krope
        cv_ref[:, hh * LANES:(hh + 1) * LANES] = kv[:, b0 + LANES:b0 + 2 * LANES].astype(BF16)


def _split_rows(n_a_tiles):
    first = lambda i: (jnp.minimum(i, n_a_tiles - 1), 0)
    second = lambda i: (jnp.maximum(i - n_a_tiles, 0), 0)
    return first, second


def _prep(layer, xa, xb, n_tiles, n_a_tiles, mod48, mod_row, g1, w_in, tables, rope_blk, gains, gcq, gckv,
          wuq_bf, wukv_bf):
    d = xa.shape[1]
    tm = TM_PREP
    m = n_tiles * tm
    row = lambda i: (i, 0)
    const = lambda i: (0, 0)
    first, second = _split_rows(n_a_tiles)
    tab_spec = pl.BlockSpec((tm, LANES), lambda i: (rope_blk(i), 0))
    widths = (512, 512, 512, 1024, 256, 256, C_HEADS * C_HEAD_PAD, C_HEADS * C_HEAD_PAD, 512)
    return pl.pallas_call(
        functools.partial(_prep_kernel, n_a_tiles=n_a_tiles, layer=layer),
        out_shape=[jax.ShapeDtypeStruct((m, w), BF16) for w in widths],
        grid=(n_tiles,),
        in_specs=[
            pl.BlockSpec((tm, d), first),
            pl.BlockSpec((tm, d), second),
            pl.BlockSpec((1, 1, d), lambda i: (mod_row(i) * 6 + 0, 0, 0)),
            pl.BlockSpec((1, 1, d), lambda i: (mod_row(i) * 6 + 1, 0, 0)),
            pl.BlockSpec((1, d), const),
            pl.BlockSpec(memory_space=pl.ANY),
            tab_spec, tab_spec, tab_spec, tab_spec, tab_spec,
            pl.BlockSpec((8, LANES), const),
            pl.BlockSpec((1, C_Q_RANK), const),
            pl.BlockSpec((1, C_KV_RANK), const),
            pl.BlockSpec((C_Q_RANK, C_HEADS * C_HEAD_PAD), const, pipeline_mode=pl.Buffered(1)),
            pl.BlockSpec((C_KV_RANK, C_HEADS * C_HEAD_PAD), const, pipeline_mode=pl.Buffered(1)),
        ],
        out_specs=[pl.BlockSpec((tm, w), row) for w in widths],
        scratch_shapes=[pltpu.VMEM((d, D_IN_PAD), BF16),
                        pltpu.VMEM((2, W_STAGE_ROWS, w_in.shape[2]), F32),
                        pltpu.SemaphoreType.DMA((2,))],
        compiler_params=_cparams(("arbitrary",)),
        name="prep",
    )(xa, xb, mod48, mod48, g1, w_in, *tables, gains, gcq, gckv, wuq_bf, wukv_bf)


def _prep_combine(layer, pos, o_sorted, x1, wts, mod48_prev, mod48, mod_row, g1, w_in, tables, rope_blk,
                  gains, gcq, gckv, wuq_bf, wukv_bf):
    m, d = x1.shape
    tm = TM_PREP
    n_tiles = m // tm
    pos = jnp.concatenate([pos, jnp.zeros((tm * TOP_K,), jnp.int32)])
    row = lambda i, p: (i, 0)
    const = lambda i, p: (0, 0)
    tab_spec = pl.BlockSpec((tm, LANES), lambda i, p: (rope_blk(i), 0))
    widths = (512, 512, 512, 1024, 256, 256, C_HEADS * C_HEAD_PAD, C_HEADS * C_HEAD_PAD, 512)
    grid_spec = pltpu.PrefetchScalarGridSpec(
        num_scalar_prefetch=1,
        grid=(n_tiles,),
        in_specs=[
            pl.BlockSpec((tm, d), row),
            pl.BlockSpec((tm, LANES), row),
            pl.BlockSpec((1, 1, d), lambda i, p: (mod_row(i) * 6 + 5, 0, 0)),
            pl.BlockSpec(memory_space=pl.ANY),
            pl.BlockSpec((1, 1, d), lambda i, p: (mod_row(i) * 6 + 0, 0, 0)),
            pl.BlockSpec((1, 1, d), lambda i, p: (mod_row(i) * 6 + 1, 0, 0)),
            pl.BlockSpec((1, d), const),
            pl.BlockSpec(memory_space=pl.ANY),
            tab_spec, tab_spec, tab_spec, tab_spec, tab_spec,
            pl.BlockSpec((8, LANES), const),
            pl.BlockSpec((1, C_Q_RANK), const),
            pl.BlockSpec((1, C_KV_RANK), const),
            pl.BlockSpec((C_Q_RANK, C_HEADS * C_HEAD_PAD), const, pipeline_mode=pl.Buffered(1)),
            pl.BlockSpec((C_KV_RANK, C_HEADS * C_HEAD_PAD), const, pipeline_mode=pl.Buffered(1)),
        ],
        out_specs=[pl.BlockSpec((tm, d), row)] + [pl.BlockSpec((tm, w), row) for w in widths],
        scratch_shapes=[pltpu.VMEM((d, D_IN_PAD), BF16),
                        pltpu.VMEM((2, W_STAGE_ROWS, w_in.shape[2]), F32),
                        pltpu.SemaphoreType.DMA((2,)),
                        pltpu.VMEM((2, TOP_K, tm * X_PITCH, LANES), F32),
                        pltpu.SemaphoreType.DMA((2,))],
    )
    return pl.pallas_call(
        functools.partial(_prep_combine_kernel, layer=layer),
        out_shape=[jax.ShapeDtypeStruct((m, d), F32)] + [jax.ShapeDtypeStruct((m, w), BF16) for w in widths],
        grid_spec=grid_spec,
        compiler_params=_cparams(("arbitrary",)),
        name="prep_combine",
    )(pos, x1, wts, mod48_prev, o_sorted, mod48, mod48, g1, w_in, *tables, gains, gcq, gckv, wuq_bf, wukv_bf)


def _dot_nt(a, b):
    return lax.dot_general(a, b, (((1,), (1,)), ((), ())), preferred_element_type=F32)


def _dot_tn(a, b):
    return lax.dot_general(a, b, (((0,), (0,)), ((), ())), preferred_element_type=F32)


def _softmax_pv_t(q, k_refs, v_refs):
    s = [_dot_nt(k[...], q) for k in k_refs]
    m = functools.reduce(jnp.maximum, [jnp.max(si, axis=0, keepdims=True) for si in s])
    e = [jnp.exp2(si - m) for si in s]
    l = functools.reduce(jnp.add, [jnp.sum(ei, axis=0, keepdims=True) for ei in e])
    o = functools.reduce(jnp.add, [_dot_tn(v[...], ei.astype(BF16)) for ei, v in zip(e, v_refs)])
    return o * (1.0 / l)


def _attn_a_kernel(*refs, n_src, lam_init):
    lv_ref, gsub_ref, q_ref = refs[0], refs[1], refs[2]
    k_refs = refs[3:3 + n_src]
    v_refs = refs[3 + n_src:3 + 2 * n_src]
    o_ref = refs[3 + 2 * n_src]
    tq = q_ref.shape[0]
    lv = lv_ref[...]
    lam = (jnp.exp(jnp.sum(lv[0:1] * lv[1:2], axis=-1, keepdims=True))
           - jnp.exp(jnp.sum(lv[2:3] * lv[3:4], axis=-1, keepdims=True)) + lam_init)
    q = q_ref[...]
    lo = lax.broadcasted_iota(jnp.int32, (1, LANES), 1) < 64
    zero = jnp.zeros_like(q)
    qq = jnp.concatenate([jnp.where(lo, q, zero), jnp.where(lo, zero, q)], axis=0)
    o2 = _softmax_pv_t(qq, k_refs, v_refs)
    o = o2[:, :tq] - lam * o2[:, tq:]
    ms = jnp.mean(o * o, axis=0, keepdims=True)
    o = (o * lax.rsqrt(ms + EPS)).T
    o_ref[...] = (o * gsub_ref[...] * (1.0 - lam_init)).astype(BF16)


def _attn_a(lv, gsub, aq, ak, av, q_row0, q_rows, srcs, n_batch, layer_idx):
    tq = min(TQ_A, q_rows)
    nq = q_rows // tq
    qb0 = q_row0 // tq
    lam_init = 0.8 - 0.6 * math.exp(-0.3 * layer_idx)
    in_specs = [
        pl.BlockSpec((4, A_DK), lambda b, h, i: (0, 0)),
        pl.BlockSpec((1, A_DV), lambda b, h, i: (0, 0)),
        pl.BlockSpec((tq, LANES), lambda b, h, i: (qb0 + b * nq + i, h)),
    ]
    kv_specs = [pl.BlockSpec((rows, LANES), lambda b, h, i, blk0=row0 // rows: (blk0 + b, h))
                for row0, rows in srcs]
    return pl.pallas_call(
        functools.partial(_attn_a_kernel, n_src=len(srcs), lam_init=lam_init),
        out_shape=jax.ShapeDtypeStruct((n_batch * q_rows, A_HEADS * A_DV), BF16),
        grid=(n_batch, A_HEADS, nq),
        in_specs=in_specs + kv_specs + kv_specs,
        out_specs=pl.BlockSpec((tq, LANES), lambda b, h, i: (b * nq + i, h)),
        compiler_params=_cparams(("arbitrary", "arbitrary", "arbitrary")),
        name="attn_a",
    )(lv, gsub, aq, *([ak] * len(srcs)), *([av] * len(srcs)))


def _attn_c_kernel(*refs, n_src):
    q_ref = refs[0]
    k_refs = refs[1:1 + n_src]
    v_refs = refs[1 + n_src:1 + 2 * n_src]
    o_ref = refs[1 + 2 * n_src]
    for hh in range(C_HEADS_PER_STEP):
        qk = slice(hh * C_HEAD_PAD, (hh + 1) * C_HEAD_PAD)
        dv = slice(hh * C_DV, (hh + 1) * C_DV)
        o = _softmax_pv_t(q_ref[:, qk], [k.at[:, qk] for k in k_refs], [v.at[:, dv] for v in v_refs])
        o_ref[:, dv] = o.T.astype(BF16)


def _attn_c(cq, ck, cv, q_row0, q_rows, srcs, n_batch):
    tq = min(TQ_C, q_rows)
    nq = q_rows // tq
    qb0 = q_row0 // tq
    hp = C_HEADS_PER_STEP
    in_specs = [pl.BlockSpec((tq, hp * C_HEAD_PAD), lambda b, h, i: (qb0 + b * nq + i, h))]
    k_specs = [pl.BlockSpec((rows, hp * C_HEAD_PAD), lambda b, h, i, blk0=row0 // rows: (blk0 + b, h))
               for row0, rows in srcs]
    v_specs = [pl.BlockSpec((rows, hp * C_DV), lambda b, h, i, blk0=row0 // rows: (blk0 + b, h))
               for row0, rows in srcs]
    return pl.pallas_call(
        functools.partial(_attn_c_kernel, n_src=len(srcs)),
        out_shape=jax.ShapeDtypeStruct((n_batch * q_rows, C_HEADS * C_DV), BF16),
        grid=(n_batch, C_HEADS // hp, nq),
        in_specs=in_specs + k_specs + v_specs,
        out_specs=pl.BlockSpec((tq, hp * C_DV), lambda b, h, i: (b * nq + i, h)),
        compiler_params=_cparams(("arbitrary", "arbitrary", "arbitrary")),
        name="attn_c",
    )(cq, *([ck] * len(srcs)), *([cv] * len(srcs)))


def _stack_heads(q):
    return jnp.concatenate([q[:, g * LANES:(g + 1) * LANES] for g in range(B_GROUP)], axis=0)


def _sink_row(sink_ref, kvh, cols):
    return jnp.concatenate(
        [jnp.full((1, cols), sink_ref[kvh * B_GROUP + g] * LOG2E, F32) for g in range(B_GROUP)], axis=1)


def _attn_b_kernel(sink_ref, q_ref, kp_ref, km_ref, kn_ref, vp_ref, vm_ref, vn_ref, kc_ref, vc_ref, o_ref):
    kvh = pl.program_id(1)
    qb = pl.program_id(2)
    nqb = pl.num_programs(2)
    n_blk = QB_B // BLOCK
    kband = jnp.concatenate([kp_ref[...], km_ref[...], kn_ref[...]], axis=0)
    vband = jnp.concatenate([vp_ref[...], vm_ref[...], vn_ref[...]], axis=0)
    kc, vc = kc_ref[...], vc_ref[...]
    sink = _sink_row(sink_ref, kvh, BLOCK)
    c = lax.broadcasted_iota(jnp.int32, (3 * BLOCK, B_GROUP * BLOCK), 0)
    r = lax.broadcasted_iota(jnp.int32, (3 * BLOCK, B_GROUP * BLOCK), 1) % BLOCK
    cr = c - r
    band_ok = (cr >= 0) & (cr <= BLOCK + WINDOW)
    for j in range(n_blk):
        q4 = _stack_heads(q_ref[j * BLOCK:(j + 1) * BLOCK, :])
        s_loc = _dot_nt(kband[j * BLOCK:(j + 3) * BLOCK], q4)
        valid = band_ok
        if j == 0:
            valid = valid & (c >= jnp.where(qb > 0, 0, BLOCK))
        if j == n_blk - 1:
            valid = valid & (c < jnp.where(qb < nqb - 1, 3 * BLOCK, 2 * BLOCK))
        s_loc = jnp.where(valid, s_loc, NEG_INF)
        s_ctx = _dot_nt(kc, q4)
        m = jnp.maximum(jnp.maximum(jnp.max(s_loc, axis=0, keepdims=True),
                                    jnp.max(s_ctx, axis=0, keepdims=True)), sink)
        e_loc, e_ctx = jnp.exp2(s_loc - m), jnp.exp2(s_ctx - m)
        l = (jnp.sum(e_loc, axis=0, keepdims=True) + jnp.sum(e_ctx, axis=0, keepdims=True)
             + jnp.exp2(sink - m))
        o = (_dot_tn(vband[j * BLOCK:(j + 3) * BLOCK], e_loc.astype(BF16))
             + _dot_tn(vc, e_ctx.astype(BF16))) * (1.0 / l)
        o = o.T
        for g in range(B_GROUP):
            o_ref[j * BLOCK:(j + 1) * BLOCK, g * LANES:(g + 1) * LANES] = (
                o[g * BLOCK:(g + 1) * BLOCK].astype(BF16))


def _attn_b(sink, bq, bk, bv, n_batch, n_lat, n_ctx):
    nqb = n_lat // QB_B
    per = QB_B // BLOCK
    blocks_per_batch = n_lat // BLOCK
    ctx_blk0 = n_batch * n_lat // n_ctx
    gw = B_GROUP * B_DH
    prev = lambda b, h, i: (b * blocks_per_batch + jnp.maximum(i * per - 1, 0), h)
    main = lambda b, h, i: (b * nqb + i, h)
    nxt = lambda b, h, i: (b * blocks_per_batch + jnp.minimum(i * per + per, blocks_per_batch - 1), h)
    ctx = lambda b, h, i: (ctx_blk0 + b, h)
    return pl.pallas_call(
        _attn_b_kernel,
        out_shape=jax.ShapeDtypeStruct((n_batch * n_lat, B_HEADS * B_DH), BF16),
        grid=(n_batch, B_KV_HEADS, nqb),
        in_specs=[
            pl.BlockSpec(memory_space=pltpu.SMEM),
            pl.BlockSpec((QB_B, gw), main),
            pl.BlockSpec((BLOCK, B_DH), prev), pl.BlockSpec((QB_B, B_DH), main), pl.BlockSpec((BLOCK, B_DH), nxt),
            pl.BlockSpec((BLOCK, B_DH), prev), pl.BlockSpec((QB_B, B_DH), main), pl.BlockSpec((BLOCK, B_DH), nxt),
            pl.BlockSpec((n_ctx, B_DH), ctx), pl.BlockSpec((n_ctx, B_DH), ctx),
        ],
        out_specs=pl.BlockSpec((QB_B, gw), main),
        compiler_params=_cparams(("arbitrary", "arbitrary", "arbitrary")),
        name="attn_b",
    )(sink, bq, bk, bk, bk, bv, bv, bv, bk, bv)


def _attn_b_ctx_kernel(sink_ref, q_ref, k_ref, v_ref, o_ref):
    kvh = pl.program_id(1)
    rows = q_ref.shape[0]
    q4 = _stack_heads(q_ref[...])
    sink = _sink_row(sink_ref, kvh, rows)
    s = _dot_nt(k_ref[...], q4)
    m = jnp.maximum(jnp.max(s, axis=0, keepdims=True), sink)
    e = jnp.exp2(s - m)
    l = jnp.sum(e, axis=0, keepdims=True) + jnp.exp2(sink - m)
    o = (_dot_tn(v_ref[...], e.astype(BF16)) * (1.0 / l)).T
    for g in range(B_GROUP):
        o_ref[:, g * LANES:(g + 1) * LANES] = o[g * rows:(g + 1) * rows].astype(BF16)


def _attn_b_ctx(sink, bq, bk, bv, n_batch, n_lat, n_ctx):
    gw = B_GROUP * B_DH
    ctx_blk0 = n_batch * n_lat // n_ctx
    return pl.pallas_call(
        _attn_b_ctx_kernel,
        out_shape=jax.ShapeDtypeStruct((n_batch * n_ctx, B_HEADS * B_DH), BF16),
        grid=(n_batch, B_KV_HEADS),
        in_specs=[
            pl.BlockSpec(memory_space=pltpu.SMEM),
            pl.BlockSpec((n_ctx, gw), lambda b, h: (ctx_blk0 + b, h)),
            pl.BlockSpec((n_ctx, B_DH), lambda b, h: (ctx_blk0 + b, h)),
            pl.BlockSpec((n_ctx, B_DH), lambda b, h: (ctx_blk0 + b, h)),
        ],
        out_specs=pl.BlockSpec((n_ctx, gw), lambda b, h: (b, h)),
        compiler_params=_cparams(("arbitrary", "arbitrary")),
        name="attn_b_ctx",
    )(sink, bq, bk, bv)


def _route(h2, rw_ref, rb_ref):
    tm = h2.shape[0]
    h_hi = h2.astype(BF16)
    h_lo = (h2 - h_hi.astype(F32)).astype(BF16)
    p = jnp.dot(jnp.concatenate([h_hi, h_lo], axis=0), rw_ref[...], preferred_element_type=F32)
    p = p[:tm] + p[tm:]
    logits = p[:, :N_EXPERTS] + p[:, N_EXPERTS:]
    scores = 1.0 / (1.0 + jnp.exp(-logits))
    sel = scores + rb_ref[...]
    lane_i = lax.broadcasted_iota(jnp.int32, sel.shape, 1)
    lane = lane_i.astype(F32)
    big = float(N_EXPERTS)

    def top2(mask):
        v = jnp.where(mask, sel, -jnp.inf)
        m1 = jnp.max(v, axis=-1, keepdims=True)
        i1 = jnp.min(jnp.where(v == m1, lane, big), axis=-1, keepdims=True)
        v2 = jnp.where(lane == i1, -jnp.inf, v)
        m2 = jnp.max(v2, axis=-1, keepdims=True)
        i2 = jnp.min(jnp.where(v2 == m2, lane, big), axis=-1, keepdims=True)
        return m1, i1, m2, i2

    best = None
    for g in range(N_GROUPS):
        m1, i1, m2, i2 = top2((lane_i >= g * EXPERTS_PER_GROUP) & (lane_i < (g + 1) * EXPERTS_PER_GROUP))
        gs = m1 + m2
        if best is None:
            best = (gs, i1, i2)
        else:
            take = gs > best[0]
            best = (jnp.where(take, gs, best[0]), jnp.where(take, i1, best[1]), jnp.where(take, i2, best[2]))
    _, e1, e2 = best
    w1 = jnp.sum(jnp.where(lane == e1, scores, 0.0), axis=-1, keepdims=True)
    w2 = jnp.sum(jnp.where(lane == e2, scores, 0.0), axis=-1, keepdims=True)
    tot = w1 + w2
    return e1, e2, w1 / tot, w2 / tot


def _outproj_kernel(oa1_ref, ob1_ref, oc1_ref, x1in_ref, oa2_ref, ob2_ref, oc2_ref, x2in_ref,
                    g1_ref, sh2_ref, sc2_ref, n2_ref, w_hbm, rw_ref, rb_ref,
                    x1_ref, h2_ref, idx_ref, wt_ref, cnt_ref, w_ref, w_stage, w_sem, cnt_acc,
                    *, n_a_tiles, layer):
    @pl.when(pl.program_id(0) == 0)
    def _():
        _load_weight_bf16(w_hbm, layer, w_stage, w_sem, w_ref)

    first = pl.program_id(0) < n_a_tiles
    oa = jnp.where(first, oa1_ref[...], oa2_ref[...])
    ob = jnp.where(first, ob1_ref[...], ob2_ref[...])
    oc = jnp.where(first, oc1_ref[...], oc2_ref[...])
    x = jnp.where(first, x1in_ref[...], x2in_ref[...])
    y = jnp.dot(jnp.concatenate([oa, ob, oc], axis=1), w_ref[...], preferred_element_type=F32)
    x1 = x + g1_ref[0] * y
    x1_ref[...] = x1
    ms = jnp.mean(x1 * x1, axis=-1, keepdims=True)
    h2 = x1 * lax.rsqrt(ms + EPS) * n2_ref[...]
    h2 = h2 * (1.0 + sc2_ref[0]) + sh2_ref[0]
    tm = h2.shape[0]
    for cc in range(ROW_CHUNKS):
        h2_ref[pl.ds(cc, tm, stride=ROW_CHUNKS), :] = h2[:, cc * LANES:(cc + 1) * LANES]
    e1, e2, w1, w2 = _route(h2, rw_ref, rb_ref)
    lane = lax.broadcasted_iota(jnp.int32, idx_ref.shape, 1)
    wt_ref[...] = jnp.where(lane == 0, w1, jnp.where(lane == 1, w2, 0.0))

    @pl.when(pl.program_id(0) == 0)
    def _():
        cnt_acc[...] = jnp.zeros_like(cnt_acc)

    lane_f = lane.astype(F32)
    sel1, sel2 = lane_f == e1, lane_f == e2
    picks = jnp.where(sel1, 1.0, 0.0) + jnp.where(sel2, 1.0, 0.0)
    t_row = lax.broadcasted_iota(jnp.int32, (tm, tm), 0)
    t_col = lax.broadcasted_iota(jnp.int32, (tm, tm), 1)
    earlier = jnp.where(t_col < t_row, 1.0, 0.0).astype(BF16)
    before = jnp.dot(earlier, picks.astype(BF16), preferred_element_type=F32) + cnt_acc[0:1, :]
    r1 = jnp.sum(jnp.where(sel1, before, 0.0), axis=-1, keepdims=True)
    r2 = jnp.sum(jnp.where(sel2, before, 0.0), axis=-1, keepdims=True)
    idx_ref[...] = jnp.where(lane == 0, e1, jnp.where(lane == 1, e2, jnp.where(
        lane == 2, r1, jnp.where(lane == 3, r2, 0.0)))).astype(jnp.int32)
    cnt_acc[...] = cnt_acc[...] + jnp.sum(picks, axis=0, keepdims=True)
    cnt_ref[...] = cnt_acc[...]


def _outproj(layer, set_a, set_b, n_tiles, n_a_tiles, mod48, mod_row, n2, w_out, router_w, router_b):
    d = set_a[3].shape[1]
    tm = TM_PREP
    m = n_tiles * tm
    row = lambda i: (i, 0)
    const = lambda i: (0, 0)
    first, second = _split_rows(n_a_tiles)
    mod_spec = lambda j: pl.BlockSpec((1, 1, d), lambda i: (mod_row(i) * 6 + j, 0, 0))
    in_specs = (
        [pl.BlockSpec((tm, a.shape[1]), first) for a in set_a]
        + [pl.BlockSpec((tm, a.shape[1]), second) for a in set_b]
        + [mod_spec(2), mod_spec(3), mod_spec(4),
           pl.BlockSpec((1, d), const),
           pl.BlockSpec(memory_space=pl.ANY),
           pl.BlockSpec((d, 2 * N_EXPERTS), const),
           pl.BlockSpec((1, N_EXPERTS), const)])
    return pl.pallas_call(
        functools.partial(_outproj_kernel, n_a_tiles=n_a_tiles, layer=layer),
        out_shape=[jax.ShapeDtypeStruct((m, d), F32), jax.ShapeDtypeStruct((m * ROW_CHUNKS, LANES), F32),
                   jax.ShapeDtypeStruct((m, LANES), jnp.int32), jax.ShapeDtypeStruct((m, LANES), F32),
                   jax.ShapeDtypeStruct((8, LANES), F32)],
        grid=(n_tiles,),
        in_specs=in_specs,
        out_specs=[pl.BlockSpec((tm, d), row), pl.BlockSpec((tm * ROW_CHUNKS, LANES), row),
                   pl.BlockSpec((tm, LANES), row), pl.BlockSpec((tm, LANES), row),
                   pl.BlockSpec((8, LANES), const)],
        scratch_shapes=[pltpu.VMEM((d, d), BF16),
                        pltpu.VMEM((2, W_STAGE_ROWS, d), F32),
                        pltpu.SemaphoreType.DMA((2,)),
                        pltpu.VMEM((8, LANES), F32)],
        compiler_params=_cparams(("arbitrary",)),
        name="outproj",
    )(*set_a, *set_b, mod48, mod48, mod48, n2, w_out, router_w, router_b)


def _experts_kernel(te_ref, nxt_ref, ws_ref, nv_ref, src_ref, h2_hbm, wg_hbm, wu_hbm, wd_hbm, o_ref,
                    xbuf, xsem, wg_f, wu_f, wd_f, wsem, wg_bf, wu_bf, wd_bf, *, layer):
    i = pl.program_id(0)
    n_valid = nv_ref[0]
    slot = i % X_SLOTS

    def issue_rows(tile):
        s = tile % X_SLOTS
        base = tile * TM_E
        for r in range(TM_E):
            row0 = pl.multiple_of(src_ref[base + r] * ROW_CHUNKS, ROW_CHUNKS)
            pltpu.make_async_copy(h2_hbm.at[pl.ds(row0, ROW_CHUNKS), :],
                                  xbuf.at[s, pl.ds(r * X_PITCH, ROW_CHUNKS), :], xsem.at[s]).start()

    def wait_rows(s):
        pltpu.make_async_copy(h2_hbm.at[pl.ds(0, TM_E * ROW_CHUNKS), :],
                              xbuf.at[s, pl.ds(0, TM_E * ROW_CHUNKS), :], xsem.at[s]).wait()

    def weight_copies(e, s):
        return (pltpu.make_async_copy(wg_hbm.at[layer, e], wg_f.at[s], wsem.at[s]),
                pltpu.make_async_copy(wu_hbm.at[layer, e], wu_f.at[s], wsem.at[s]),
                pltpu.make_async_copy(wd_hbm.at[layer, e], wd_f.at[s], wsem.at[s]))

    @pl.when(i == 0)
    def _():
        for cp in weight_copies(te_ref[0], ws_ref[0]):
            cp.start()
        for t in range(X_SLOTS - 1):
            issue_rows(t)

    first_of_expert = jnp.logical_or(i == 0, te_ref[i] != te_ref[jnp.maximum(i - 1, 0)])

    @pl.when(jnp.logical_and(first_of_expert, i < n_valid))
    def _():
        ws = ws_ref[i]
        for cp in weight_copies(te_ref[i], ws):
            cp.wait()

        @pl.when(nxt_ref[i] >= 0)
        def _():
            for cp in weight_copies(nxt_ref[i], 1 - ws):
                cp.start(priority=WEIGHT_DMA_PRIORITY)

        wg_bf[...] = wg_f[ws].astype(BF16)
        wu_bf[...] = wu_f[ws].astype(BF16)
        wd_bf[...] = wd_f[ws].astype(BF16)

    @pl.when(i < n_valid)
    def _():
        wait_rows(slot)
        x = jnp.concatenate([xbuf[slot, pl.ds(cc, TM_E, stride=X_PITCH), :] for cc in range(ROW_CHUNKS)],
                            axis=1).astype(BF16)
        issue_rows(i + X_SLOTS - 1)
        g = jnp.dot(x, wg_bf[...], preferred_element_type=F32)
        u = jnp.dot(x, wu_bf[...], preferred_element_type=F32)
        a = (_silu(g) * u).astype(BF16)
        y = jnp.dot(a, wd_bf[...], preferred_element_type=F32)
        for cc in range(ROW_CHUNKS):
            o_ref[pl.ds(cc, TM_E, stride=ROW_CHUNKS), :] = y[:, cc * LANES:(cc + 1) * LANES]

    @pl.when(i == n_valid - 1)
    def _():
        for t in range(1, X_SLOTS):
            wait_rows((i + t) % X_SLOTS)

    @pl.when(i >= n_valid)
    def _():
        o_ref[...] = jnp.zeros_like(o_ref)


def _experts(layer, tile_expert, next_expert, w_slot, n_valid, src, h2_all, w_gate, w_up, w_down):
    n_rows = src.shape[0] - (X_SLOTS - 1) * TM_E
    nt = n_rows // TM_E
    d, de = w_gate.shape[2], w_gate.shape[3]
    grid_spec = pltpu.PrefetchScalarGridSpec(
        num_scalar_prefetch=5,
        grid=(nt,),
        in_specs=[pl.BlockSpec(memory_space=pl.ANY)] * 4,
        out_specs=pl.BlockSpec((TM_E * ROW_CHUNKS, LANES), lambda i, *_: (i, 0)),
        scratch_shapes=[
            pltpu.VMEM((X_SLOTS, TM_E * X_PITCH, LANES), F32),
            pltpu.SemaphoreType.DMA((X_SLOTS,)),
            pltpu.VMEM((2, d, de), F32), pltpu.VMEM((2, d, de), F32), pltpu.VMEM((2, de, d), F32),
            pltpu.SemaphoreType.DMA((2,)),
            pltpu.VMEM((d, de), BF16), pltpu.VMEM((d, de), BF16), pltpu.VMEM((de, d), BF16),
        ],
    )
    return pl.pallas_call(
        functools.partial(_experts_kernel, layer=layer),
        out_shape=jax.ShapeDtypeStruct((n_rows * ROW_CHUNKS, LANES), F32),
        grid_spec=grid_spec,
        compiler_params=_cparams(("arbitrary",)),
        name="experts",
    )(tile_expert, next_expert, w_slot, n_valid, src, h2_all, w_gate, w_up, w_down)


def _combine_kernel(pos_ref, o_hbm, x1_ref, wt_ref, g2_ref, x2_ref, buf, sem):
    i = pl.program_id(0)
    nt = pl.num_programs(0)
    slot = i % 2

    def issue(tile, s):
        base = tile * (TM_C * TOP_K)
        for r in range(TM_C):
            for k in range(TOP_K):
                row0 = pl.multiple_of(pos_ref[base + TOP_K * r + k] * ROW_CHUNKS, ROW_CHUNKS)
                pltpu.make_async_copy(o_hbm.at[pl.ds(row0, ROW_CHUNKS), :],
                                      buf.at[s, k, pl.ds(r * X_PITCH, ROW_CHUNKS), :], sem.at[s]
                                      ).start(priority=k % DMA_QUEUES)

    @pl.when(i == 0)
    def _():
        issue(0, 0)

    @pl.when(i + 1 < nt)
    def _():
        issue(i + 1, 1 - slot)

    def expert_rows(k):
        return jnp.concatenate([buf[slot, k, pl.ds(cc, TM_C, stride=X_PITCH), :] for cc in range(ROW_CHUNKS)],
                               axis=1)

    for k in range(TOP_K):
        pltpu.make_async_copy(o_hbm.at[pl.ds(0, TM_C * ROW_CHUNKS), :],
                              buf.at[slot, k, pl.ds(0, TM_C * ROW_CHUNKS), :], sem.at[slot]).wait()
    wt = wt_ref[...]
    y = wt[:, 0:1] * expert_rows(0) + wt[:, 1:2] * expert_rows(1)
    x2_ref[...] = x1_ref[...] + g2_ref[0] * y


def _combine(pos, o_sorted, x1, wts, mod48, mod_row):
    m, d = x1.shape
    grid_spec = pltpu.PrefetchScalarGridSpec(
        num_scalar_prefetch=1,
        grid=(m // TM_C,),
        in_specs=[
            pl.BlockSpec(memory_space=pl.ANY),
            pl.BlockSpec((TM_C, d), lambda i, p: (i, 0)),
            pl.BlockSpec((TM_C, LANES), lambda i, p: (i, 0)),
            pl.BlockSpec((1, 1, d), lambda i, p: (mod_row(i) * 6 + 5, 0, 0)),
        ],
        out_specs=pl.BlockSpec((TM_C, d), lambda i, p: (i, 0)),
        scratch_shapes=[pltpu.VMEM((2, TOP_K, TM_C * X_PITCH, LANES), F32), pltpu.SemaphoreType.DMA((2,))],
    )
    return pl.pallas_call(
        _combine_kernel,
        out_shape=jax.ShapeDtypeStruct((m, d), F32),
        grid_spec=grid_spec,
        compiler_params=_cparams(("arbitrary",)),
        name="combine",
    )(pos, o_sorted, x1, wts, mod48)


def _rope_tables(n_lat, tm):
    f32 = np.float32
    t = np.arange(n_lat)
    r = (t // GRID_W).astype(f32)
    col = (t % GRID_W).astype(f32)

    def cos_sin(dim):
        nf = dim // 4
        inv = (f32(ROPE_THETA) ** (-np.arange(nf, dtype=f32) / f32(nf))).astype(f32)
        ang = np.concatenate([r[:, None] * inv, col[:, None] * inv], axis=-1).astype(f32)
        return np.cos(ang).astype(f32), np.sin(ang).astype(f32)

    c64, s64 = cos_sin(A_DK)
    c128, s128 = cos_sin(B_DH)
    z32 = np.zeros_like(s64)
    tabs = [
        np.concatenate([c64, c64, c64, c64], axis=-1),
        np.concatenate([-s64, z32, -s64, z32], axis=-1),
        np.concatenate([z32, s64, z32, s64], axis=-1),
        np.concatenate([c128, c128], axis=-1),
        np.concatenate([-s128, s128], axis=-1),
    ]
    ident = [np.ones((tm, LANES), f32), np.zeros((tm, LANES), f32), np.zeros((tm, LANES), f32),
             np.ones((tm, LANES), f32), np.zeros((tm, LANES), f32)]
    return [jnp.asarray(np.concatenate([a, b], axis=0)) for a, b in zip(tabs, ident)]


def _layer_params(l, a_qn, a_kn, b_qn, b_kn, c_qa_norm, c_kva_norm, c_wuq, c_wukv, c_qn, c_kn):
    z64 = jnp.zeros((C_ROPE,), F32)
    gains = jnp.stack([
        jnp.tile(a_qn[l], 2) * (A_DK ** -0.5 * LOG2E),
        jnp.tile(a_kn[l], 2),
        b_qn[l] * (B_DH ** -0.5 * LOG2E),
        b_kn[l],
        c_qn[l][:C_NOPE] * (C_DQK ** -0.5 * LOG2E),
        jnp.concatenate([c_qn[l][C_NOPE:] * (C_DQK ** -0.5 * LOG2E), z64]),
        c_kn[l][:C_NOPE],
        jnp.concatenate([c_kn[l][C_NOPE:], z64]),
    ])
    wq = c_wuq[l].reshape(C_Q_RANK, C_HEADS, C_DQK)
    wq = jnp.pad(wq, ((0, 0), (0, 0), (0, C_HEAD_PAD - C_DQK))).reshape(C_Q_RANK, C_HEADS * C_HEAD_PAD)
    return dict(gains=gains, gcq=c_qa_norm[l][None], gckv=c_kva_norm[l][None],
                wuq=wq.astype(BF16), wukv=c_wukv[l].astype(BF16))


def _sorted_rows(idx, rank, counts):
    t = idx.shape[0]
    n_pairs = t * TOP_K
    n_rows = ((n_pairs + N_EXPERTS * (TM_E - 1)) // TM_E) * TM_E
    nt = n_rows // TM_E
    flat_e = idx.reshape(-1)
    rank = rank.reshape(-1)
    padded = ((counts + TM_E - 1) // TM_E) * TM_E
    ends = jnp.cumsum(padded)
    pos = (ends - padded)[flat_e] + rank
    src = jnp.zeros((n_rows + (X_SLOTS - 1) * TM_E,), jnp.int32).at[pos].set(
        jnp.arange(n_pairs, dtype=jnp.int32) // TOP_K, unique_indices=True)
    tile_start = jnp.arange(nt, dtype=jnp.int32) * TM_E
    tile_expert = jnp.sum((ends[None, :] <= tile_start[:, None]).astype(jnp.int32), axis=1)
    last_used = jnp.sum((ends <= ends[-1] - 1).astype(jnp.int32))
    tile_expert = jnp.minimum(tile_expert, last_used)
    n_valid = (ends[-1:] // TM_E).astype(jnp.int32)
    used = counts > 0
    e_ids = jnp.arange(N_EXPERTS, dtype=jnp.int32)
    later = jnp.where(used[None, :] & (e_ids[None, :] > e_ids[:, None]), e_ids[None, :], N_EXPERTS)
    next_used = jnp.min(later, axis=1)
    next_used = jnp.where(next_used == N_EXPERTS, -1, next_used).astype(jnp.int32)
    parity = ((jnp.cumsum(used.astype(jnp.int32)) - 1) % 2).astype(jnp.int32)
    return (pos.astype(jnp.int32), src, tile_expert.astype(jnp.int32), next_used[tile_expert],
            parity[tile_expert], n_valid)


def kernel(x, c, ctx, c_ctx, ada_w, ada_b, norm1_g, norm2_g, w_in, w_out, a_qn, a_kn, a_lambda, a_subln,
           b_qn, b_kn, b_sink, c_qa_norm, c_kva_norm, c_wuq, c_wukv, c_qn, c_kn,
           router_w, router_bias, moe_w_gate, moe_w_up, moe_w_down):
    bsz, n_lat, d = x.shape
    n_ctx = ctx.shape[1]
    depth = ada_w.shape[0]
    t_lat, t_ctx = bsz * n_lat, bsz * n_ctx
    tm = TM_PREP
    lat_tiles = n_lat // tm
    n_lat_tiles, n_ctx_tiles = t_lat // tm, t_ctx // tm
    n_all_tiles = n_lat_tiles + n_ctx_tiles

    cond8 = jnp.concatenate([c, c_ctx[None], jnp.zeros((8 - bsz - 1, d), F32)], axis=0)
    mod = _ada_modulation(cond8, ada_w, ada_b)
    tables = _rope_tables(n_lat, tm)
    rb = router_bias[None]
    rw_hi = router_w.astype(BF16)
    rw_lo = (router_w - rw_hi.astype(F32)).astype(BF16)
    rw2 = jnp.concatenate([rw_hi, rw_lo], axis=1)

    mod_row = lambda i: jnp.minimum(i // lat_tiles, bsz)
    mod_row_c = lambda i: jnp.minimum(i // (n_lat // TM_C), bsz)
    rope_blk = lambda i: jnp.where(i < n_lat_tiles, i % lat_tiles, lat_tiles)

    xa, xb, n_a_tiles = x.reshape(t_lat, d), ctx.reshape(t_ctx, d), n_lat_tiles
    lat_src, ctx_src = (0, n_lat), (t_lat, n_ctx)
    moe_out = None
    for l in range(depth):
        last = l == depth - 1
        p = _layer_params(l, a_qn, a_kn, b_qn, b_kn, c_qa_norm, c_kva_norm, c_wuq, c_wukv, c_qn, c_kn)
        mod48 = mod[l].reshape(8 * 6, 1, d)
        prep_tail = (mod48, mod_row, norm1_g[l][None], w_in, tables, rope_blk,
                     p["gains"], p["gcq"], p["gckv"], p["wuq"], p["wukv"])
        if moe_out is None:
            aq, ak, av, bq, bk, bv, cq, ck, cv = _prep(l, xa, xb, n_all_tiles, n_a_tiles, *prep_tail)
        else:
            xa, aq, ak, av, bq, bk, bv, cq, ck, cv = _prep_combine(l, *moe_out, *prep_tail)
            xb, n_a_tiles = xa, n_all_tiles

        lv, gsub, sink = a_lambda[l], a_subln[l][None], b_sink[l]
        o_a = _attn_a(lv, gsub, aq, ak, av, 0, n_lat, [lat_src, ctx_src], bsz, l)
        o_b = _attn_b(sink, bq, bk, bv, bsz, n_lat, n_ctx)
        o_c = _attn_c(cq, ck, cv, 0, n_lat, [lat_src, ctx_src], bsz)
        set_a = (o_a, o_b, o_c, xa)
        if not last:
            oc_a = _attn_a(lv, gsub, aq, ak, av, t_lat, n_ctx, [ctx_src], bsz, l)
            oc_b = _attn_b_ctx(sink, bq, bk, bv, bsz, n_lat, n_ctx)
            oc_c = _attn_c(cq, ck, cv, t_lat, n_ctx, [ctx_src], bsz)
            set_b, n_tok_tiles = (oc_a, oc_b, oc_c, xb), n_all_tiles
        else:
            set_b, n_tok_tiles = set_a, n_lat_tiles
        x1, h2, idx, wts, cnt = _outproj(l, set_a, set_b, n_tok_tiles, min(n_a_tiles, n_tok_tiles), mod48,
                                         mod_row, norm2_g[l][None], w_out, rw2, rb)

        pos, src, tile_expert, next_expert, w_slot, n_valid = _sorted_rows(
            idx[:, :TOP_K], idx[:, TOP_K:2 * TOP_K], cnt[0, :N_EXPERTS].astype(jnp.int32))
        o_sorted = _experts(l, tile_expert, next_expert, w_slot, n_valid, src, h2,
                            moe_w_gate, moe_w_up, moe_w_down)
        moe_out = (pos, o_sorted, x1, wts, mod48)
    return _combine(pos, o_sorted, x1, wts, mod48, mod_row_c).reshape(bsz, n_lat, d)
```

```python
import functools
import math

import jax
import jax.numpy as jnp
import numpy as np
from jax import lax
from jax.experimental import pallas as pl
from jax.experimental.pallas import tpu as pltpu

F32 = jnp.float32
BF16 = jnp.bfloat16

D_MODEL = 2048
GRID_W = 64
BLOCK = 128
WINDOW = 128
ROPE_THETA = 10000.0
EPS = 1e-6
NEG_INF = -1e30
LOG2E = math.log2(math.e)
A_HEADS, A_DK = 4, 64
A_DV = 2 * A_DK
B_HEADS, B_KV_HEADS, B_DH = 8, 2, 128
B_GROUP = B_HEADS // B_KV_HEADS
C_HEADS, C_Q_RANK, C_KV_RANK, C_NOPE, C_ROPE, C_DV = 4, 512, 256, 128, 64, 128
C_DQK = C_NOPE + C_ROPE
SPLIT_SIZES = (A_HEADS * 2 * A_DK, A_HEADS * 2 * A_DK, A_HEADS * A_DV,
               B_HEADS * B_DH, B_KV_HEADS * B_DH, B_KV_HEADS * B_DH,
               C_Q_RANK, C_KV_RANK, C_ROPE)
D_IN = sum(SPLIT_SIZES)
N_EXPERTS, N_GROUPS, TOP_K = 32, 4, 2
EXPERTS_PER_GROUP = N_EXPERTS // N_GROUPS
D_EXPERT = 512

LANES = 128
V7X_VMEM_LIMIT = 56 * 1024 * 1024

D_IN_PAD = ((D_IN + LANES - 1) // LANES) * LANES
C_HEAD_PAD = 2 * LANES
TM_PREP = 256
TQ_A = 512
TQ_C = 1024
C_HEADS_PER_STEP = 1
QB_B = 512
TM_E = 128
X_SLOTS = 3
ROW_CHUNKS = D_MODEL // LANES
X_PITCH = ROW_CHUNKS + 8
W_SLOTS = 3
DMA_QUEUES = 2
WEIGHT_DMA_PRIORITY = 1
TM_C = 128
ADA_TN = 1024
W_STAGE_ROWS = 128

_OFF = [0]
for _s in SPLIT_SIZES:
    _OFF.append(_OFF[-1] + _s)
O_AQ, O_AK, O_AV, O_BQ, O_BK, O_BV, O_CQ, O_CKV, O_CKR, _ = _OFF


def _cparams(sem):
    return pltpu.CompilerParams(dimension_semantics=sem, vmem_limit_bytes=V7X_VMEM_LIMIT)


def _silu(v):
    return v * (1.0 / (1.0 + jnp.exp(-v)))


def _ada_kernel(cond_ref, w_ref, b_ref, o_ref):
    s = _silu(cond_ref[...]).astype(BF16)
    o_ref[0] = jnp.dot(s, w_ref[0].astype(BF16), preferred_element_type=F32) + b_ref[0]


def _ada_modulation(cond8, ada_w, ada_b):
    depth, d, n = ada_w.shape
    return pl.pallas_call(
        _ada_kernel,
        out_shape=jax.ShapeDtypeStruct((depth, 8, n), F32),
        grid=(depth, n // ADA_TN),
        in_specs=[
            pl.BlockSpec((8, d), lambda l, j: (0, 0)),
            pl.BlockSpec((1, d, ADA_TN), lambda l, j: (l, 0, j)),
            pl.BlockSpec((1, 1, ADA_TN), lambda l, j: (l, 0, j)),
        ],
        out_specs=pl.BlockSpec((1, 8, ADA_TN), lambda l, j: (l, 0, j)),
        compiler_params=_cparams(("arbitrary", "arbitrary")),
        name="ada_modulation",
    )(cond8, ada_w, ada_b.reshape(depth, 1, n))


def _rope64(v, c, sa, sb):
    return v * c + pltpu.roll(v, 96, 1) * sa + pltpu.roll(v, 32, 1) * sb


def _rope128(v, c, s):
    return v * c + pltpu.roll(v, 64, 1) * s


def _norm_seg128(v, g):
    ms = jnp.sum(v * v, axis=-1, keepdims=True) * (1.0 / 128)
    return v * lax.rsqrt(ms + EPS) * g


def _norm_seg64x2(v, g, lo):
    sq = v * v
    s_lo = jnp.sum(jnp.where(lo, sq, 0.0), axis=-1, keepdims=True)
    s_hi = jnp.sum(jnp.where(lo, 0.0, sq), axis=-1, keepdims=True)
    ms = jnp.where(lo, s_lo, s_hi) * (1.0 / 64)
    return v * lax.rsqrt(ms + EPS) * g


def _norm_low64(v, g):
    ms = jnp.sum(v * v, axis=-1, keepdims=True) * (1.0 / 64)
    return v * lax.rsqrt(ms + EPS) * g


def _load_weight_bf16(w_hbm, layer, stage, sem, w_bf):
    k, n = w_bf.shape[0], w_hbm.shape[1]
    ch = stage.shape[1]

    def chunk_copy(c):
        return pltpu.make_async_copy(w_hbm.at[pl.ds(layer * k + c * ch, ch), :], stage.at[c % 2],
                                     sem.at[c % 2])

    n_pad = w_bf.shape[1]
    if n_pad > n:
        edge = (n // LANES) * LANES
        w_bf[:, edge:n_pad] = jnp.zeros((k, n_pad - edge), BF16)
    chunk_copy(0).start()
    for c in range(k // ch):
        if c + 1 < k // ch:
            chunk_copy(c + 1).start()
        chunk_copy(c).wait()
        w_bf[c * ch:(c + 1) * ch, 0:n] = stage[c % 2].astype(BF16)


def _prep_kernel(xa_ref, xb_ref, sh_ref, sc_ref, g1_ref, w_hbm, ca_ref, saa_ref, sab_ref, cb_ref, sb_ref,
                 gains_ref, gcq_ref, gckv_ref, wuq_ref, wukv_ref,
                 aq_ref, ak_ref, av_ref, bq_ref, bk_ref, bv_ref, cq_ref, ck_ref, cv_ref,
                 w_bf, w_stage, w_sem, *, n_a_tiles, layer):
    @pl.when(pl.program_id(0) == 0)
    def _():
        _load_weight_bf16(w_hbm, layer, w_stage, w_sem, w_bf)

    x = jnp.where(pl.program_id(0) < n_a_tiles, xa_ref[...], xb_ref[...])
    ms = jnp.mean(x * x, axis=-1, keepdims=True)
    h = x * lax.rsqrt(ms + EPS) * g1_ref[...]
    h = h * (1.0 + sc_ref[0]) + sh_ref[0]
    z = jnp.dot(h.astype(BF16), w_bf[...], preferred_element_type=F32)

    lane = lax.broadcasted_iota(jnp.int32, (1, LANES), 1)
    lo = lane < 64
    ca, saa, sab = ca_ref[...], saa_ref[...], sab_ref[...]
    cb, sb = cb_ref[...], sb_ref[...]
    g_aq, g_ak, g_bq, g_bk = gains_ref[0:1], gains_ref[1:2], gains_ref[2:3], gains_ref[3:4]
    g_cqn, g_cqr, g_ckn, g_ckr = gains_ref[4:5], gains_ref[5:6], gains_ref[6:7], gains_ref[7:8]

    def blk(off, j):
        return z[:, off + j * LANES: off + (j + 1) * LANES]

    for j in range(A_HEADS):
        sl = slice(j * LANES, (j + 1) * LANES)
        aq_ref[:, sl] = _rope64(_norm_seg64x2(blk(O_AQ, j), g_aq, lo), ca, saa, sab).astype(BF16)
        ak_ref[:, sl] = _rope64(_norm_seg64x2(blk(O_AK, j), g_ak, lo), ca, saa, sab).astype(BF16)
        av_ref[:, sl] = blk(O_AV, j).astype(BF16)
    for j in range(B_HEADS):
        sl = slice(j * LANES, (j + 1) * LANES)
        bq_ref[:, sl] = _rope128(_norm_seg128(blk(O_BQ, j), g_bq), cb, sb).astype(BF16)
    for j in range(B_KV_HEADS):
        sl = slice(j * LANES, (j + 1) * LANES)
        bk_ref[:, sl] = _rope128(_norm_seg128(blk(O_BK, j), g_bk), cb, sb).astype(BF16)
        bv_ref[:, sl] = blk(O_BV, j).astype(BF16)
    cq = z[:, O_CQ:O_CQ + C_Q_RANK]
    cqn = cq * lax.rsqrt(jnp.mean(cq * cq, axis=-1, keepdims=True) + EPS) * gcq_ref[...]
    q = jnp.dot(cqn.astype(BF16), wuq_ref[...], preferred_element_type=F32)
    ckv = z[:, O_CKV:O_CKV + C_KV_RANK]
    ckvn = ckv * lax.rsqrt(jnp.mean(ckv * ckv, axis=-1, keepdims=True) + EPS) * gckv_ref[...]
    kv = jnp.dot(ckvn.astype(BF16), wukv_ref[...], preferred_element_type=F32)
    krope = _rope64(_norm_low64(z[:, O_CKR:O_CKR + LANES], g_ckr), ca, saa, sab).astype(BF16)
    for hh in range(C_HEADS):
        b0 = hh * C_HEAD_PAD
        cq_ref[:, b0:b0 + LANES] = _norm_seg128(q[:, b0:b0 + LANES], g_cqn).astype(BF16)
        cq_ref[:, b0 + LANES:b0 + 2 * LANES] = _rope64(
            _norm_low64(q[:, b0 + LANES:b0 + 2 * LANES], g_cqr), ca, saa, sab).astype(BF16)
        ck_ref[:, b0:b0 + LANES] = _norm_seg128(kv[:, b0:b0 + LANES], g_ckn).astype(BF16)
        ck_ref[:, b0 + LANES:b0 + 2 * LANES] = krope
        cv_ref[:, hh * LANES:(hh + 1) * LANES] = kv[:, b0 + LANES:b0 + 2 * LANES].astype(BF16)


def _split_rows(n_a_tiles):
    first = lambda i: (jnp.minimum(i, n_a_tiles - 1), 0)
    second = lambda i: (jnp.maximum(i - n_a_tiles, 0), 0)
    return first, second


def _prep(layer, xa, xb, n_tiles, n_a_tiles, mod48, mod_row, g1, w_in, tables, rope_blk, gains, gcq, gckv,
          wuq_bf, wukv_bf):
    d = xa.shape[1]
    tm = TM_PREP
    m = n_tiles * tm
    row = lambda i: (i, 0)
    const = lambda i: (0, 0)
    first, second = _split_rows(n_a_tiles)
    tab_spec = pl.BlockSpec((tm, LANES), lambda i: (rope_blk(i), 0))
    widths = (512, 512, 512, 1024, 256, 256, C_HEADS * C_HEAD_PAD, C_HEADS * C_HEAD_PAD, 512)
    return pl.pallas_call(
        functools.partial(_prep_kernel, n_a_tiles=n_a_tiles, layer=layer),
        out_shape=[jax.ShapeDtypeStruct((m, w), BF16) for w in widths],
        grid=(n_tiles,),
        in_specs=[
            pl.BlockSpec((tm, d), first),
            pl.BlockSpec((tm, d), second),
            pl.BlockSpec((1, 1, d), lambda i: (mod_row(i) * 6 + 0, 0, 0)),
            pl.BlockSpec((1, 1, d), lambda i: (mod_row(i) * 6 + 1, 0, 0)),
            pl.BlockSpec((1, d), const),
            pl.BlockSpec(memory_space=pl.ANY),
            tab_spec, tab_spec, tab_spec, tab_spec, tab_spec,
            pl.BlockSpec((8, LANES), const),
            pl.BlockSpec((1, C_Q_RANK), const),
            pl.BlockSpec((1, C_KV_RANK), const),
            pl.BlockSpec((C_Q_RANK, C_HEADS * C_HEAD_PAD), const, pipeline_mode=pl.Buffered(1)),
            pl.BlockSpec((C_KV_RANK, C_HEADS * C_HEAD_PAD), const, pipeline_mode=pl.Buffered(1)),
        ],
        out_specs=[pl.BlockSpec((tm, w), row) for w in widths],
        scratch_shapes=[pltpu.VMEM((d, D_IN_PAD), BF16),
                        pltpu.VMEM((2, W_STAGE_ROWS, w_in.shape[1]), F32),
                        pltpu.SemaphoreType.DMA((2,))],
        compiler_params=_cparams(("arbitrary",)),
        name="prep",
    )(xa, xb, mod48, mod48, g1, w_in, *tables, gains, gcq, gckv, wuq_bf, wukv_bf)


def _dot_nt(a, b):
    return lax.dot_general(a, b, (((1,), (1,)), ((), ())), preferred_element_type=F32)


def _dot_tn(a, b):
    return lax.dot_general(a, b, (((0,), (0,)), ((), ())), preferred_element_type=F32)


def _softmax_pv_t(q, k_refs, v_refs):
    s = [_dot_nt(k[...], q) for k in k_refs]
    m = functools.reduce(jnp.maximum, [jnp.max(si, axis=0, keepdims=True) for si in s])
    e = [jnp.exp2(si - m) for si in s]
    l = functools.reduce(jnp.add, [jnp.sum(ei, axis=0, keepdims=True) for ei in e])
    o = functools.reduce(jnp.add, [_dot_tn(v[...], ei.astype(BF16)) for ei, v in zip(e, v_refs)])
    return o * (1.0 / l)


def _attn_a_kernel(*refs, n_src, lam_init):
    lv_ref, gsub_ref, q_ref = refs[0], refs[1], refs[2]
    k_refs = refs[3:3 + n_src]
    v_refs = refs[3 + n_src:3 + 2 * n_src]
    o_ref = refs[3 + 2 * n_src]
    tq = q_ref.shape[0]
    lv = lv_ref[...]
    lam = (jnp.exp(jnp.sum(lv[0:1] * lv[1:2], axis=-1, keepdims=True))
           - jnp.exp(jnp.sum(lv[2:3] * lv[3:4], axis=-1, keepdims=True)) + lam_init)
    q = q_ref[...]
    lo = lax.broadcasted_iota(jnp.int32, (1, LANES), 1) < 64
    zero = jnp.zeros_like(q)
    qq = jnp.concatenate([jnp.where(lo, q, zero), jnp.where(lo, zero, q)], axis=0)
    o2 = _softmax_pv_t(qq, k_refs, v_refs)
    o = o2[:, :tq] - lam * o2[:, tq:]
    ms = jnp.mean(o * o, axis=0, keepdims=True)
    o = (o * lax.rsqrt(ms + EPS)).T
    o_ref[...] = (o * gsub_ref[...] * (1.0 - lam_init)).astype(BF16)


def _attn_a(lv, gsub, aq, ak, av, q_row0, q_rows, srcs, n_batch, layer_idx):
    tq = min(TQ_A, q_rows)
    nq = q_rows // tq
    qb0 = q_row0 // tq
    lam_init = 0.8 - 0.6 * math.exp(-0.3 * layer_idx)
    in_specs = [
        pl.BlockSpec((4, A_DK), lambda b, h, i: (0, 0)),
        pl.BlockSpec((1, A_DV), lambda b, h, i: (0, 0)),
        pl.BlockSpec((tq, LANES), lambda b, h, i: (qb0 + b * nq + i, h)),
    ]
    kv_specs = [pl.BlockSpec((rows, LANES), lambda b, h, i, blk0=row0 // rows: (blk0 + b, h))
                for row0, rows in srcs]
    return pl.pallas_call(
        functools.partial(_attn_a_kernel, n_src=len(srcs), lam_init=lam_init),
        out_shape=jax.ShapeDtypeStruct((n_batch * q_rows, A_HEADS * A_DV), BF16),
        grid=(n_batch, A_HEADS, nq),
        in_specs=in_specs + kv_specs + kv_specs,
        out_specs=pl.BlockSpec((tq, LANES), lambda b, h, i: (b * nq + i, h)),
        compiler_params=_cparams(("arbitrary", "arbitrary", "arbitrary")),
        name="attn_a",
    )(lv, gsub, aq, *([ak] * len(srcs)), *([av] * len(srcs)))


def _attn_c_kernel(*refs, n_src):
    q_ref = refs[0]
    k_refs = refs[1:1 + n_src]
    v_refs = refs[1 + n_src:1 + 2 * n_src]
    o_ref = refs[1 + 2 * n_src]
    for hh in range(C_HEADS_PER_STEP):
        qk = slice(hh * C_HEAD_PAD, (hh + 1) * C_HEAD_PAD)
        dv = slice(hh * C_DV, (hh + 1) * C_DV)
        o = _softmax_pv_t(q_ref[:, qk], [k.at[:, qk] for k in k_refs], [v.at[:, dv] for v in v_refs])
        o_ref[:, dv] = o.T.astype(BF16)


def _attn_c(cq, ck, cv, q_row0, q_rows, srcs, n_batch):
    tq = min(TQ_C, q_rows)
    nq = q_rows // tq
    qb0 = q_row0 // tq
    hp = C_HEADS_PER_STEP
    in_specs = [pl.BlockSpec((tq, hp * C_HEAD_PAD), lambda b, h, i: (qb0 + b * nq + i, h))]
    k_specs = [pl.BlockSpec((rows, hp * C_HEAD_PAD), lambda b, h, i, blk0=row0 // rows: (blk0 + b, h))
               for row0, rows in srcs]
    v_specs = [pl.BlockSpec((rows, hp * C_DV), lambda b, h, i, blk0=row0 // rows: (blk0 + b, h))
               for row0, rows in srcs]
    return pl.pallas_call(
        functools.partial(_attn_c_kernel, n_src=len(srcs)),
        out_shape=jax.ShapeDtypeStruct((n_batch * q_rows, C_HEADS * C_DV), BF16),
        grid=(n_batch, C_HEADS // hp, nq),
        in_specs=in_specs + k_specs + v_specs,
        out_specs=pl.BlockSpec((tq, hp * C_DV), lambda b, h, i: (b * nq + i, h)),
        compiler_params=_cparams(("arbitrary", "arbitrary", "arbitrary")),
        name="attn_c",
    )(cq, *([ck] * len(srcs)), *([cv] * len(srcs)))


def _stack_heads(q):
    return jnp.concatenate([q[:, g * LANES:(g + 1) * LANES] for g in range(B_GROUP)], axis=0)


def _sink_row(sink_ref, kvh, cols):
    return jnp.concatenate(
        [jnp.full((1, cols), sink_ref[kvh * B_GROUP + g] * LOG2E, F32) for g in range(B_GROUP)], axis=1)


def _attn_b_kernel(sink_ref, q_ref, kp_ref, km_ref, kn_ref, vp_ref, vm_ref, vn_ref, kc_ref, vc_ref, o_ref):
    kvh = pl.program_id(1)
    qb = pl.program_id(2)
    nqb = pl.num_programs(2)
    n_blk = QB_B // BLOCK
    kband = jnp.concatenate([kp_ref[...], km_ref[...], kn_ref[...]], axis=0)
    vband = jnp.concatenate([vp_ref[...], vm_ref[...], vn_ref[...]], axis=0)
    kc, vc = kc_ref[...], vc_ref[...]
    sink = _sink_row(sink_ref, kvh, BLOCK)
    c = lax.broadcasted_iota(jnp.int32, (3 * BLOCK, B_GROUP * BLOCK), 0)
    r = lax.broadcasted_iota(jnp.int32, (3 * BLOCK, B_GROUP * BLOCK), 1) % BLOCK
    cr = c - r
    band_ok = (cr >= 0) & (cr <= BLOCK + WINDOW)
    for j in range(n_blk):
        q4 = _stack_heads(q_ref[j * BLOCK:(j + 1) * BLOCK, :])
        s_loc = _dot_nt(kband[j * BLOCK:(j + 3) * BLOCK], q4)
        valid = band_ok
        if j == 0:
            valid = valid & (c >= jnp.where(qb > 0, 0, BLOCK))
        if j == n_blk - 1:
            valid = valid & (c < jnp.where(qb < nqb - 1, 3 * BLOCK, 2 * BLOCK))
        s_loc = jnp.where(valid, s_loc, NEG_INF)
        s_ctx = _dot_nt(kc, q4)
        m = jnp.maximum(jnp.maximum(jnp.max(s_loc, axis=0, keepdims=True),
                                    jnp.max(s_ctx, axis=0, keepdims=True)), sink)
        e_loc, e_ctx = jnp.exp2(s_loc - m), jnp.exp2(s_ctx - m)
        l = (jnp.sum(e_loc, axis=0, keepdims=True) + jnp.sum(e_ctx, axis=0, keepdims=True)
             + jnp.exp2(sink - m))
        o = (_dot_tn(vband[j * BLOCK:(j + 3) * BLOCK], e_loc.astype(BF16))
             + _dot_tn(vc, e_ctx.astype(BF16))) * (1.0 / l)
        o = o.T
        for g in range(B_GROUP):
            o_ref[j * BLOCK:(j + 1) * BLOCK, g * LANES:(g + 1) * LANES] = (
                o[g * BLOCK:(g + 1) * BLOCK].astype(BF16))


def _attn_b(sink, bq, bk, bv, n_batch, n_lat, n_ctx):
    nqb = n_lat // QB_B
    per = QB_B // BLOCK
    blocks_per_batch = n_lat // BLOCK
    ctx_blk0 = n_batch * n_lat // n_ctx
    gw = B_GROUP * B_DH
    prev = lambda b, h, i: (b * blocks_per_batch + jnp.maximum(i * per - 1, 0), h)
    main = lambda b, h, i: (b * nqb + i, h)
    nxt = lambda b, h, i: (b * blocks_per_batch + jnp.minimum(i * per + per, blocks_per_batch - 1), h)
    ctx = lambda b, h, i: (ctx_blk0 + b, h)
    return pl.pallas_call(
        _attn_b_kernel,
        out_shape=jax.ShapeDtypeStruct((n_batch * n_lat, B_HEADS * B_DH), BF16),
        grid=(n_batch, B_KV_HEADS, nqb),
        in_specs=[
            pl.BlockSpec(memory_space=pltpu.SMEM),
            pl.BlockSpec((QB_B, gw), main),
            pl.BlockSpec((BLOCK, B_DH), prev), pl.BlockSpec((QB_B, B_DH), main), pl.BlockSpec((BLOCK, B_DH), nxt),
            pl.BlockSpec((BLOCK, B_DH), prev), pl.BlockSpec((QB_B, B_DH), main), pl.BlockSpec((BLOCK, B_DH), nxt),
            pl.BlockSpec((n_ctx, B_DH), ctx), pl.BlockSpec((n_ctx, B_DH), ctx),
        ],
        out_specs=pl.BlockSpec((QB_B, gw), main),
        compiler_params=_cparams(("arbitrary", "arbitrary", "arbitrary")),
        name="attn_b",
    )(sink, bq, bk, bk, bk, bv, bv, bv, bk, bv)


def _attn_b_ctx_kernel(sink_ref, q_ref, k_ref, v_ref, o_ref):
    kvh = pl.program_id(1)
    rows = q_ref.shape[0]
    q4 = _stack_heads(q_ref[...])
    sink = _sink_row(sink_ref, kvh, rows)
    s = _dot_nt(k_ref[...], q4)
    m = jnp.maximum(jnp.max(s, axis=0, keepdims=True), sink)
    e = jnp.exp2(s - m)
    l = jnp.sum(e, axis=0, keepdims=True) + jnp.exp2(sink - m)
    o = (_dot_tn(v_ref[...], e.astype(BF16)) * (1.0 / l)).T
    for g in range(B_GROUP):
        o_ref[:, g * LANES:(g + 1) * LANES] = o[g * rows:(g + 1) * rows].astype(BF16)


def _attn_b_ctx(sink, bq, bk, bv, n_batch, n_lat, n_ctx):
    gw = B_GROUP * B_DH
    ctx_blk0 = n_batch * n_lat // n_ctx
    return pl.pallas_call(
        _attn_b_ctx_kernel,
        out_shape=jax.ShapeDtypeStruct((n_batch * n_ctx, B_HEADS * B_DH), BF16),
        grid=(n_batch, B_KV_HEADS),
        in_specs=[
            pl.BlockSpec(memory_space=pltpu.SMEM),
            pl.BlockSpec((n_ctx, gw), lambda b, h: (ctx_blk0 + b, h)),
            pl.BlockSpec((n_ctx, B_DH), lambda b, h: (ctx_blk0 + b, h)),
            pl.BlockSpec((n_ctx, B_DH), lambda b, h: (ctx_blk0 + b, h)),
        ],
        out_specs=pl.BlockSpec((n_ctx, gw), lambda b, h: (b, h)),
        compiler_params=_cparams(("arbitrary", "arbitrary")),
        name="attn_b_ctx",
    )(sink, bq, bk, bv)


def _route(h2, rw_ref, rb_ref):
    tm = h2.shape[0]
    h_hi = h2.astype(BF16)
    h_lo = (h2 - h_hi.astype(F32)).astype(BF16)
    p = jnp.dot(jnp.concatenate([h_hi, h_lo], axis=0), rw_ref[...], preferred_element_type=F32)
    p = p[:tm] + p[tm:]
    logits = p[:, :N_EXPERTS] + p[:, N_EXPERTS:]
    scores = 1.0 / (1.0 + jnp.exp(-logits))
    sel = scores + rb_ref[...]
    lane_i = lax.broadcasted_iota(jnp.int32, sel.shape, 1)
    lane = lane_i.astype(F32)
    big = float(N_EXPERTS)

    def top2(mask):
        v = jnp.where(mask, sel, -jnp.inf)
        m1 = jnp.max(v, axis=-1, keepdims=True)
        i1 = jnp.min(jnp.where(v == m1, lane, big), axis=-1, keepdims=True)
        v2 = jnp.where(lane == i1, -jnp.inf, v)
        m2 = jnp.max(v2, axis=-1, keepdims=True)
        i2 = jnp.min(jnp.where(v2 == m2, lane, big), axis=-1, keepdims=True)
        return m1, i1, m2, i2

    best = None
    for g in range(N_GROUPS):
        m1, i1, m2, i2 = top2((lane_i >= g * EXPERTS_PER_GROUP) & (lane_i < (g + 1) * EXPERTS_PER_GROUP))
        gs = m1 + m2
        if best is None:
            best = (gs, i1, i2)
        else:
            take = gs > best[0]
            best = (jnp.where(take, gs, best[0]), jnp.where(take, i1, best[1]), jnp.where(take, i2, best[2]))
    _, e1, e2 = best
    w1 = jnp.sum(jnp.where(lane == e1, scores, 0.0), axis=-1, keepdims=True)
    w2 = jnp.sum(jnp.where(lane == e2, scores, 0.0), axis=-1, keepdims=True)
    tot = w1 + w2
    return e1, e2, w1 / tot, w2 / tot


def _outproj_kernel(oa1_ref, ob1_ref, oc1_ref, x1in_ref, oa2_ref, ob2_ref, oc2_ref, x2in_ref,
                    g1_ref, sh2_ref, sc2_ref, n2_ref, w_hbm, rw_ref, rb_ref,
                    x1_ref, h2_ref, idx_ref, wt_ref, cnt_ref, w_ref, w_stage, w_sem, cnt_acc,
                    *, n_a_tiles, layer):
    @pl.when(pl.program_id(0) == 0)
    def _():
        _load_weight_bf16(w_hbm, layer, w_stage, w_sem, w_ref)

    first = pl.program_id(0) < n_a_tiles
    oa = jnp.where(first, oa1_ref[...], oa2_ref[...])
    ob = jnp.where(first, ob1_ref[...], ob2_ref[...])
    oc = jnp.where(first, oc1_ref[...], oc2_ref[...])
    x = jnp.where(first, x1in_ref[...], x2in_ref[...])
    y = jnp.dot(jnp.concatenate([oa, ob, oc], axis=1), w_ref[...], preferred_element_type=F32)
    x1 = x + g1_ref[0] * y
    x1_ref[...] = x1
    ms = jnp.mean(x1 * x1, axis=-1, keepdims=True)
    h2 = x1 * lax.rsqrt(ms + EPS) * n2_ref[...]
    h2 = h2 * (1.0 + sc2_ref[0]) + sh2_ref[0]
    tm = h2.shape[0]
    for cc in range(ROW_CHUNKS):
        h2_ref[pl.ds(cc, tm, stride=ROW_CHUNKS), :] = h2[:, cc * LANES:(cc + 1) * LANES]
    e1, e2, w1, w2 = _route(h2, rw_ref, rb_ref)
    lane = lax.broadcasted_iota(jnp.int32, idx_ref.shape, 1)
    wt_ref[...] = jnp.where(lane == 0, w1, jnp.where(lane == 1, w2, 0.0))

    @pl.when(pl.program_id(0) == 0)
    def _():
        cnt_acc[...] = jnp.zeros_like(cnt_acc)

    lane_f = lane.astype(F32)
    sel1, sel2 = lane_f == e1, lane_f == e2
    picks = jnp.where(sel1, 1.0, 0.0) + jnp.where(sel2, 1.0, 0.0)
    t_row = lax.broadcasted_iota(jnp.int32, (tm, tm), 0)
    t_col = lax.broadcasted_iota(jnp.int32, (tm, tm), 1)
    earlier = jnp.where(t_col < t_row, 1.0, 0.0).astype(BF16)
    before = jnp.dot(earlier, picks.astype(BF16), preferred_element_type=F32) + cnt_acc[0:1, :]
    r1 = jnp.sum(jnp.where(sel1, before, 0.0), axis=-1, keepdims=True)
    r2 = jnp.sum(jnp.where(sel2, before, 0.0), axis=-1, keepdims=True)
    idx_ref[...] = jnp.where(lane == 0, e1, jnp.where(lane == 1, e2, jnp.where(
        lane == 2, r1, jnp.where(lane == 3, r2, 0.0)))).astype(jnp.int32)
    cnt_acc[...] = cnt_acc[...] + jnp.sum(picks, axis=0, keepdims=True)
    cnt_ref[...] = cnt_acc[...]


def _outproj(layer, set_a, set_b, n_tiles, n_a_tiles, mod48, mod_row, n2, w_out, router_w, router_b):
    d = set_a[3].shape[1]
    tm = TM_PREP
    m = n_tiles * tm
    row = lambda i: (i, 0)
    const = lambda i: (0, 0)
    first, second = _split_rows(n_a_tiles)
    mod_spec = lambda j: pl.BlockSpec((1, 1, d), lambda i: (mod_row(i) * 6 + j, 0, 0))
    in_specs = (
        [pl.BlockSpec((tm, a.shape[1]), first) for a in set_a]
        + [pl.BlockSpec((tm, a.shape[1]), second) for a in set_b]
        + [mod_spec(2), mod_spec(3), mod_spec(4),
           pl.BlockSpec((1, d), const),
           pl.BlockSpec(memory_space=pl.ANY),
           pl.BlockSpec((d, 2 * N_EXPERTS), const),
           pl.BlockSpec((1, N_EXPERTS), const)])
    return pl.pallas_call(
        functools.partial(_outproj_kernel, n_a_tiles=n_a_tiles, layer=layer),
        out_shape=[jax.ShapeDtypeStruct((m, d), F32), jax.ShapeDtypeStruct((m * ROW_CHUNKS, LANES), F32),
                   jax.ShapeDtypeStruct((m, LANES), jnp.int32), jax.ShapeDtypeStruct((m, LANES), F32),
                   jax.ShapeDtypeStruct((8, LANES), F32)],
        grid=(n_tiles,),
        in_specs=in_specs,
        out_specs=[pl.BlockSpec((tm, d), row), pl.BlockSpec((tm * ROW_CHUNKS, LANES), row),
                   pl.BlockSpec((tm, LANES), row), pl.BlockSpec((tm, LANES), row),
                   pl.BlockSpec((8, LANES), const)],
        scratch_shapes=[pltpu.VMEM((d, d), BF16),
                        pltpu.VMEM((2, W_STAGE_ROWS, d), F32),
                        pltpu.SemaphoreType.DMA((2,)),
                        pltpu.VMEM((8, LANES), F32)],
        compiler_params=_cparams(("arbitrary",)),
        name="outproj",
    )(*set_a, *set_b, mod48, mod48, mod48, n2, w_out, router_w, router_b)


def _experts_kernel(te_ref, nxt_ref, ws_ref, nv_ref, src_ref, h2_hbm, wg_hbm, wu_hbm, wd_hbm, o_ref,
                    xbuf, xsem, wg_f, wu_f, wd_f, wsem, wg_bf, wu_bf, wd_bf, *, layer):
    i = pl.program_id(0)
    n_valid = nv_ref[0]
    slot = i % X_SLOTS

    def issue_rows(tile):
        s = tile % X_SLOTS
        base = tile * TM_E
        for r in range(TM_E):
            row0 = pl.multiple_of(src_ref[base + r] * ROW_CHUNKS, ROW_CHUNKS)
            pltpu.make_async_copy(h2_hbm.at[pl.ds(row0, ROW_CHUNKS), :],
                                  xbuf.at[s, pl.ds(r * X_PITCH, ROW_CHUNKS), :], xsem.at[s]).start()

    def wait_rows(s):
        pltpu.make_async_copy(h2_hbm.at[pl.ds(0, TM_E * ROW_CHUNKS), :],
                              xbuf.at[s, pl.ds(0, TM_E * ROW_CHUNKS), :], xsem.at[s]).wait()

    def weight_copies(e, s):
        return (pltpu.make_async_copy(wg_hbm.at[layer, e], wg_f.at[s], wsem.at[s]),
                pltpu.make_async_copy(wu_hbm.at[layer, e], wu_f.at[s], wsem.at[s]),
                pltpu.make_async_copy(wd_hbm.at[layer, e], wd_f.at[s], wsem.at[s]))

    def start_ahead(n_ahead, ws):
        e = nxt_ref[(n_ahead - 1) * pl.num_programs(0) + i]

        @pl.when(e >= 0)
        def _():
            for cp in weight_copies(e, (ws + n_ahead) % W_SLOTS):
                cp.start(priority=WEIGHT_DMA_PRIORITY)

    @pl.when(i == 0)
    def _():
        for cp in weight_copies(te_ref[0], ws_ref[0]):
            cp.start()
        for n_ahead in range(1, W_SLOTS - 1):
            start_ahead(n_ahead, ws_ref[0])
        for t in range(X_SLOTS - 1):
            issue_rows(t)

    first_of_expert = jnp.logical_or(i == 0, te_ref[i] != te_ref[jnp.maximum(i - 1, 0)])

    @pl.when(jnp.logical_and(first_of_expert, i < n_valid))
    def _():
        ws = ws_ref[i]
        for cp in weight_copies(te_ref[i], ws):
            cp.wait()
        start_ahead(W_SLOTS - 1, ws)

        wg_bf[...] = wg_f[ws].astype(BF16)
        wu_bf[...] = wu_f[ws].astype(BF16)
        wd_bf[...] = wd_f[ws].astype(BF16)

    @pl.when(i < n_valid)
    def _():
        wait_rows(slot)
        x = jnp.concatenate([xbuf[slot, pl.ds(cc, TM_E, stride=X_PITCH), :] for cc in range(ROW_CHUNKS)],
                            axis=1).astype(BF16)
        issue_rows(i + X_SLOTS - 1)
        g = jnp.dot(x, wg_bf[...], preferred_element_type=F32)
        u = jnp.dot(x, wu_bf[...], preferred_element_type=F32)
        a = (_silu(g) * u).astype(BF16)
        y = jnp.dot(a, wd_bf[...], preferred_element_type=F32)
        for cc in range(ROW_CHUNKS):
            o_ref[pl.ds(cc, TM_E, stride=ROW_CHUNKS), :] = y[:, cc * LANES:(cc + 1) * LANES]

    @pl.when(i == n_valid - 1)
    def _():
        for t in range(1, X_SLOTS):
            wait_rows((i + t) % X_SLOTS)

    @pl.when(i >= n_valid)
    def _():
        o_ref[...] = jnp.zeros_like(o_ref)


def _experts(layer, tile_expert, next_expert, w_slot, n_valid, src, h2_all, w_gate, w_up, w_down):
    n_rows = src.shape[0] - (X_SLOTS - 1) * TM_E
    nt = n_rows // TM_E
    d, de = w_gate.shape[2], w_gate.shape[3]
    grid_spec = pltpu.PrefetchScalarGridSpec(
        num_scalar_prefetch=5,
        grid=(nt,),
        in_specs=[pl.BlockSpec(memory_space=pl.ANY)] * 4,
        out_specs=pl.BlockSpec((TM_E * ROW_CHUNKS, LANES), lambda i, *_: (i, 0)),
        scratch_shapes=[
            pltpu.VMEM((X_SLOTS, TM_E * X_PITCH, LANES), F32),
            pltpu.SemaphoreType.DMA((X_SLOTS,)),
            pltpu.VMEM((W_SLOTS, d, de), F32), pltpu.VMEM((W_SLOTS, d, de), F32),
            pltpu.VMEM((W_SLOTS, de, d), F32),
            pltpu.SemaphoreType.DMA((W_SLOTS,)),
            pltpu.VMEM((d, de), BF16), pltpu.VMEM((d, de), BF16), pltpu.VMEM((de, d), BF16),
        ],
    )
    return pl.pallas_call(
        functools.partial(_experts_kernel, layer=layer),
        out_shape=jax.ShapeDtypeStruct((n_rows * ROW_CHUNKS, LANES), F32),
        grid_spec=grid_spec,
        compiler_params=_cparams(("arbitrary",)),
        name="experts",
    )(tile_expert, next_expert, w_slot, n_valid, src, h2_all, w_gate, w_up, w_down)


def _combine_kernel(pos_ref, o_hbm, x1_ref, wt_ref, g2_ref, x2_ref, buf, sem):
    i = pl.program_id(0)
    nt = pl.num_programs(0)
    slot = i % 2

    def issue(tile, s):
        base = tile * (TM_C * TOP_K)
        for r in range(TM_C):
            for k in range(TOP_K):
                row0 = pl.multiple_of(pos_ref[base + TOP_K * r + k] * ROW_CHUNKS, ROW_CHUNKS)
                pltpu.make_async_copy(o_hbm.at[pl.ds(row0, ROW_CHUNKS), :],
                                      buf.at[s, k, pl.ds(r * X_PITCH, ROW_CHUNKS), :], sem.at[s]
                                      ).start(priority=k % DMA_QUEUES)

    @pl.when(i == 0)
    def _():
        issue(0, 0)

    @pl.when(i + 1 < nt)
    def _():
        issue(i + 1, 1 - slot)

    def expert_rows(k):
        return jnp.concatenate([buf[slot, k, pl.ds(cc, TM_C, stride=X_PITCH), :] for cc in range(ROW_CHUNKS)],
                               axis=1)

    for k in range(TOP_K):
        pltpu.make_async_copy(o_hbm.at[pl.ds(0, TM_C * ROW_CHUNKS), :],
                              buf.at[slot, k, pl.ds(0, TM_C * ROW_CHUNKS), :], sem.at[slot]).wait()
    wt = wt_ref[...]
    y = wt[:, 0:1] * expert_rows(0) + wt[:, 1:2] * expert_rows(1)
    x2_ref[...] = x1_ref[...] + g2_ref[0] * y


def _combine(pos, o_sorted, x1, wts, mod48, mod_row):
    m, d = x1.shape
    grid_spec = pltpu.PrefetchScalarGridSpec(
        num_scalar_prefetch=1,
        grid=(m // TM_C,),
        in_specs=[
            pl.BlockSpec(memory_space=pl.ANY),
            pl.BlockSpec((TM_C, d), lambda i, p: (i, 0)),
            pl.BlockSpec((TM_C, LANES), lambda i, p: (i, 0)),
            pl.BlockSpec((1, 1, d), lambda i, p: (mod_row(i) * 6 + 5, 0, 0)),
        ],
        out_specs=pl.BlockSpec((TM_C, d), lambda i, p: (i, 0)),
        scratch_shapes=[pltpu.VMEM((2, TOP_K, TM_C * X_PITCH, LANES), F32), pltpu.SemaphoreType.DMA((2,))],
    )
    return pl.pallas_call(
        _combine_kernel,
        out_shape=jax.ShapeDtypeStruct((m, d), F32),
        grid_spec=grid_spec,
        compiler_params=_cparams(("arbitrary",)),
        name="combine",
    )(pos, o_sorted, x1, wts, mod48)


def _rope_tables(n_lat, tm):
    f32 = np.float32
    t = np.arange(n_lat)
    r = (t // GRID_W).astype(f32)
    col = (t % GRID_W).astype(f32)

    def cos_sin(dim):
        nf = dim // 4
        inv = (f32(ROPE_THETA) ** (-np.arange(nf, dtype=f32) / f32(nf))).astype(f32)
        ang = np.concatenate([r[:, None] * inv, col[:, None] * inv], axis=-1).astype(f32)
        return np.cos(ang).astype(f32), np.sin(ang).astype(f32)

    c64, s64 = cos_sin(A_DK)
    c128, s128 = cos_sin(B_DH)
    z32 = np.zeros_like(s64)
    tabs = [
        np.concatenate([c64, c64, c64, c64], axis=-1),
        np.concatenate([-s64, z32, -s64, z32], axis=-1),
        np.concatenate([z32, s64, z32, s64], axis=-1),
        np.concatenate([c128, c128], axis=-1),
        np.concatenate([-s128, s128], axis=-1),
    ]
    ident = [np.ones((tm, LANES), f32), np.zeros((tm, LANES), f32), np.zeros((tm, LANES), f32),
             np.ones((tm, LANES), f32), np.zeros((tm, LANES), f32)]
    return [jnp.asarray(np.concatenate([a, b], axis=0)) for a, b in zip(tabs, ident)]


def _layer_params(l, a_qn, a_kn, b_qn, b_kn, c_qa_norm, c_kva_norm, c_wuq, c_wukv, c_qn, c_kn):
    z64 = jnp.zeros((C_ROPE,), F32)
    gains = jnp.stack([
        jnp.tile(a_qn[l], 2) * (A_DK ** -0.5 * LOG2E),
        jnp.tile(a_kn[l], 2),
        b_qn[l] * (B_DH ** -0.5 * LOG2E),
        b_kn[l],
        c_qn[l][:C_NOPE] * (C_DQK ** -0.5 * LOG2E),
        jnp.concatenate([c_qn[l][C_NOPE:] * (C_DQK ** -0.5 * LOG2E), z64]),
        c_kn[l][:C_NOPE],
        jnp.concatenate([c_kn[l][C_NOPE:], z64]),
    ])
    wq = c_wuq[l].reshape(C_Q_RANK, C_HEADS, C_DQK)
    wq = jnp.pad(wq, ((0, 0), (0, 0), (0, C_HEAD_PAD - C_DQK))).reshape(C_Q_RANK, C_HEADS * C_HEAD_PAD)
    return dict(gains=gains, gcq=c_qa_norm[l][None], gckv=c_kva_norm[l][None],
                wuq=wq.astype(BF16), wukv=c_wukv[l].astype(BF16))


def _sorted_rows(idx, rank, counts):
    t = idx.shape[0]
    n_pairs = t * TOP_K
    n_rows = ((n_pairs + N_EXPERTS * (TM_E - 1)) // TM_E) * TM_E
    nt = n_rows // TM_E
    flat_e = idx.reshape(-1)
    rank = rank.reshape(-1)
    padded = ((counts + TM_E - 1) // TM_E) * TM_E
    ends = jnp.cumsum(padded)
    pos = (ends - padded)[flat_e] + rank
    src = jnp.zeros((n_rows + (X_SLOTS - 1) * TM_E,), jnp.int32).at[pos].set(
        jnp.arange(n_pairs, dtype=jnp.int32) // TOP_K, unique_indices=True)
    tile_start = jnp.arange(nt, dtype=jnp.int32) * TM_E
    tile_expert = jnp.sum((ends[None, :] <= tile_start[:, None]).astype(jnp.int32), axis=1)
    last_used = jnp.sum((ends <= ends[-1] - 1).astype(jnp.int32))
    tile_expert = jnp.minimum(tile_expert, last_used)
    n_valid = (ends[-1:] // TM_E).astype(jnp.int32)
    used = counts > 0
    e_ids = jnp.arange(N_EXPERTS, dtype=jnp.int32)
    later = jnp.where(used[None, :] & (e_ids[None, :] > e_ids[:, None]), e_ids[None, :], N_EXPERTS)
    next_used = jnp.min(later, axis=1)
    next_pad = jnp.concatenate([next_used, jnp.full((1,), N_EXPERTS, jnp.int32)])
    ahead, hop = [], tile_expert
    for _ in range(W_SLOTS - 1):
        hop = next_pad[hop]
        ahead.append(jnp.where(hop == N_EXPERTS, -1, hop).astype(jnp.int32))
    w_slot = ((jnp.cumsum(used.astype(jnp.int32)) - 1) % W_SLOTS).astype(jnp.int32)
    return (pos.astype(jnp.int32), src, tile_expert.astype(jnp.int32), jnp.concatenate(ahead),
            w_slot[tile_expert], n_valid)


def kernel(x, c, ctx, c_ctx, ada_w, ada_b, norm1_g, norm2_g, w_in, w_out, a_qn, a_kn, a_lambda, a_subln,
           b_qn, b_kn, b_sink, c_qa_norm, c_kva_norm, c_wuq, c_wukv, c_qn, c_kn,
           router_w, router_bias, moe_w_gate, moe_w_up, moe_w_down):
    bsz, n_lat, d = x.shape
    n_ctx = ctx.shape[1]
    depth = ada_w.shape[0]
    t_lat, t_ctx = bsz * n_lat, bsz * n_ctx
    tm = TM_PREP
    lat_tiles = n_lat // tm
    n_lat_tiles, n_ctx_tiles = t_lat // tm, t_ctx // tm
    n_all_tiles = n_lat_tiles + n_ctx_tiles

    cond8 = jnp.concatenate([c, c_ctx[None], jnp.zeros((8 - bsz - 1, d), F32)], axis=0)
    mod = _ada_modulation(cond8, ada_w, ada_b)
    tables = _rope_tables(n_lat, tm)
    rb = router_bias[None]
    rw_hi = router_w.astype(BF16)
    rw_lo = (router_w - rw_hi.astype(F32)).astype(BF16)
    rw2 = jnp.concatenate([rw_hi, rw_lo], axis=1)

    mod_row = lambda i: jnp.minimum(i // lat_tiles, bsz)
    mod_row_c = lambda i: jnp.minimum(i // (n_lat // TM_C), bsz)
    rope_blk = lambda i: jnp.where(i < n_lat_tiles, i % lat_tiles, lat_tiles)

    w_in2d = w_in.reshape(depth * d, w_in.shape[2])
    w_out2d = w_out.reshape(depth * w_out.shape[1], d)
    xa, xb, n_a_tiles = x.reshape(t_lat, d), ctx.reshape(t_ctx, d), n_lat_tiles
    lat_src, ctx_src = (0, n_lat), (t_lat, n_ctx)
    for l in range(depth):
        last = l == depth - 1
        p = _layer_params(l, a_qn, a_kn, b_qn, b_kn, c_qa_norm, c_kva_norm, c_wuq, c_wukv, c_qn, c_kn)
        mod48 = mod[l].reshape(8 * 6, 1, d)
        aq, ak, av, bq, bk, bv, cq, ck, cv = _prep(
            l, xa, xb, n_all_tiles, n_a_tiles, mod48, mod_row, norm1_g[l][None], w_in2d, tables, rope_blk,
            p["gains"], p["gcq"], p["gckv"], p["wuq"], p["wukv"])

        lv, gsub, sink = a_lambda[l], a_subln[l][None], b_sink[l]
        o_a = _attn_a(lv, gsub, aq, ak, av, 0, n_lat, [lat_src, ctx_src], bsz, l)
        o_b = _attn_b(sink, bq, bk, bv, bsz, n_lat, n_ctx)
        o_c = _attn_c(cq, ck, cv, 0, n_lat, [lat_src, ctx_src], bsz)
        set_a = (o_a, o_b, o_c, xa)
        if not last:
            oc_a = _attn_a(lv, gsub, aq, ak, av, t_lat, n_ctx, [ctx_src], bsz, l)
            oc_b = _attn_b_ctx(sink, bq, bk, bv, bsz, n_lat, n_ctx)
            oc_c = _attn_c(cq, ck, cv, t_lat, n_ctx, [ctx_src], bsz)
            set_b, n_tok_tiles = (oc_a, oc_b, oc_c, xb), n_all_tiles
        else:
            set_b, n_tok_tiles = set_a, n_lat_tiles
        x1, h2, idx, wts, cnt = _outproj(l, set_a, set_b, n_tok_tiles, min(n_a_tiles, n_tok_tiles), mod48,
                                         mod_row, norm2_g[l][None], w_out2d, rw2, rb)

        pos, src, tile_expert, next_expert, w_slot, n_valid = _sorted_rows(
            idx[:, :TOP_K], idx[:, TOP_K:2 * TOP_K], cnt[0, :N_EXPERTS].astype(jnp.int32))
        o_sorted = _experts(l, tile_expert, next_expert, w_slot, n_valid, src, h2,
                            moe_w_gate, moe_w_up, moe_w_down)
        xa = _combine(pos, o_sorted, x1, wts, mod48, mod_row_c)
        xb, n_a_tiles = xa, n_all_tiles
    return xa.reshape(bsz, n_lat, d)
```

```python
import functools
import math

import jax
import jax.numpy as jnp
import numpy as np
from jax import lax
from jax.experimental import pallas as pl
from jax.experimental.pallas import tpu as pltpu

F32 = jnp.float32
BF16 = jnp.bfloat16

D_MODEL = 2048
GRID_W = 64
BLOCK = 128
WINDOW = 128
ROPE_THETA = 10000.0
EPS = 1e-6
NEG_INF = -1e30
LOG2E = math.log2(math.e)
A_HEADS, A_DK = 4, 64
A_DV = 2 * A_DK
B_HEADS, B_KV_HEADS, B_DH = 8, 2, 128
B_GROUP = B_HEADS // B_KV_HEADS
C_HEADS, C_Q_RANK, C_KV_RANK, C_NOPE, C_ROPE, C_DV = 4, 512, 256, 128, 64, 128
C_DQK = C_NOPE + C_ROPE
SPLIT_SIZES = (A_HEADS * 2 * A_DK, A_HEADS * 2 * A_DK, A_HEADS * A_DV,
               B_HEADS * B_DH, B_KV_HEADS * B_DH, B_KV_HEADS * B_DH,
               C_Q_RANK, C_KV_RANK, C_ROPE)
D_IN = sum(SPLIT_SIZES)
N_EXPERTS, N_GROUPS, TOP_K = 32, 4, 2
EXPERTS_PER_GROUP = N_EXPERTS // N_GROUPS
D_EXPERT = 512

LANES = 128
V7X_VMEM_LIMIT = 56 * 1024 * 1024

D_IN_PAD = ((D_IN + LANES - 1) // LANES) * LANES
C_HEAD_PAD = 2 * LANES
TM_PREP = 256
TQ_A = 512
TQ_C = 1024
C_HEADS_PER_STEP = 1
QB_B = 512
TM_E = 128
X_SLOTS = 3
ROW_CHUNKS = D_MODEL // LANES
X_PITCH = ROW_CHUNKS + 8
DMA_QUEUES = 2
WEIGHT_DMA_PRIORITY = 1
TM_C = 256
ADA_TN = 1024
W_STAGE_ROWS = 128

_OFF = [0]
for _s in SPLIT_SIZES:
    _OFF.append(_OFF[-1] + _s)
O_AQ, O_AK, O_AV, O_BQ, O_BK, O_BV, O_CQ, O_CKV, O_CKR, _ = _OFF


def _cparams(sem):
    return pltpu.CompilerParams(dimension_semantics=sem, vmem_limit_bytes=V7X_VMEM_LIMIT)


def _silu(v):
    return v * (1.0 / (1.0 + jnp.exp(-v)))


def _ada_kernel(cond_ref, w_ref, b_ref, o_ref):
    s = _silu(cond_ref[...]).astype(BF16)
    o_ref[0] = jnp.dot(s, w_ref[0].astype(BF16), preferred_element_type=F32) + b_ref[0]


def _ada_modulation(cond8, ada_w, ada_b):
    depth, d, n = ada_w.shape
    return pl.pallas_call(
        _ada_kernel,
        out_shape=jax.ShapeDtypeStruct((depth, 8, n), F32),
        grid=(depth, n // ADA_TN),
        in_specs=[
            pl.BlockSpec((8, d), lambda l, j: (0, 0)),
            pl.BlockSpec((1, d, ADA_TN), lambda l, j: (l, 0, j)),
            pl.BlockSpec((1, 1, ADA_TN), lambda l, j: (l, 0, j)),
        ],
        out_specs=pl.BlockSpec((1, 8, ADA_TN), lambda l, j: (l, 0, j)),
        compiler_params=_cparams(("arbitrary", "arbitrary")),
        name="ada_modulation",
    )(cond8, ada_w, ada_b.reshape(depth, 1, n))


def _rope64(v, c, sa, sb):
    return v * c + pltpu.roll(v, 96, 1) * sa + pltpu.roll(v, 32, 1) * sb


def _rope128(v, c, s):
    return v * c + pltpu.roll(v, 64, 1) * s


def _norm_seg128(v, g):
    ms = jnp.sum(v * v, axis=-1, keepdims=True) * (1.0 / 128)
    return v * lax.rsqrt(ms + EPS) * g


def _norm_seg64x2(v, g, lo):
    sq = v * v
    s_lo = jnp.sum(jnp.where(lo, sq, 0.0), axis=-1, keepdims=True)
    s_hi = jnp.sum(jnp.where(lo, 0.0, sq), axis=-1, keepdims=True)
    ms = jnp.where(lo, s_lo, s_hi) * (1.0 / 64)
    return v * lax.rsqrt(ms + EPS) * g


def _norm_low64(v, g):
    ms = jnp.sum(v * v, axis=-1, keepdims=True) * (1.0 / 64)
    return v * lax.rsqrt(ms + EPS) * g


def _load_weight_bf16(w_hbm, layer, stage, sem, w_bf):
    k, n = w_hbm.shape[1], w_hbm.shape[2]
    ch = stage.shape[1]

    def chunk_copy(c):
        return pltpu.make_async_copy(w_hbm.at[layer, pl.ds(c * ch, ch), :], stage.at[c % 2], sem.at[c % 2])

    n_pad = w_bf.shape[1]
    if n_pad > n:
        edge = (n // LANES) * LANES
        w_bf[:, edge:n_pad] = jnp.zeros((k, n_pad - edge), BF16)
    chunk_copy(0).start()
    for c in range(k // ch):
        if c + 1 < k // ch:
            chunk_copy(c + 1).start()
        chunk_copy(c).wait()
        w_bf[c * ch:(c + 1) * ch, 0:n] = stage[c % 2].astype(BF16)


def _prep_kernel(xa_ref, xb_ref, sh_ref, sc_ref, g1_ref, w_hbm, ca_ref, saa_ref, sab_ref, cb_ref, sb_ref,
                 gains_ref, gcq_ref, gckv_ref, wuq_ref, wukv_ref,
                 aq_ref, ak_ref, av_ref, bq_ref, bk_ref, bv_ref, cq_ref, ck_ref, cv_ref,
                 w_bf, w_stage, w_sem, *, n_a_tiles, layer):
    @pl.when(pl.program_id(0) == 0)
    def _():
        _load_weight_bf16(w_hbm, layer, w_stage, w_sem, w_bf)

    x = jnp.where(pl.program_id(0) < n_a_tiles, xa_ref[...], xb_ref[...])
    ms = jnp.mean(x * x, axis=-1, keepdims=True)
    h = x * lax.rsqrt(ms + EPS) * g1_ref[...]
    h = h * (1.0 + sc_ref[0]) + sh_ref[0]
    z = jnp.dot(h.astype(BF16), w_bf[...], preferred_element_type=F32)

    lane = lax.broadcasted_iota(jnp.int32, (1, LANES), 1)
    lo = lane < 64
    ca, saa, sab = ca_ref[...], saa_ref[...], sab_ref[...]
    cb, sb = cb_ref[...], sb_ref[...]
    g_aq, g_ak, g_bq, g_bk = gains_ref[0:1], gains_ref[1:2], gains_ref[2:3], gains_ref[3:4]
    g_cqn, g_cqr, g_ckn, g_ckr = gains_ref[4:5], gains_ref[5:6], gains_ref[6:7], gains_ref[7:8]

    def blk(off, j):
        return z[:, off + j * LANES: off + (j + 1) * LANES]

    for j in range(A_HEADS):
        sl = slice(j * LANES, (j + 1) * LANES)
        aq_ref[:, sl] = _rope64(_norm_seg64x2(blk(O_AQ, j), g_aq, lo), ca, saa, sab).astype(BF16)
        ak_ref[:, sl] = _rope64(_norm_seg64x2(blk(O_AK, j), g_ak, lo), ca, saa, sab).astype(BF16)
        av_ref[:, sl] = blk(O_AV, j).astype(BF16)
    for j in range(B_HEADS):
        sl = slice(j * LANES, (j + 1) * LANES)
        bq_ref[:, sl] = _rope128(_norm_seg128(blk(O_BQ, j), g_bq), cb, sb).astype(BF16)
    for j in range(B_KV_HEADS):
        sl = slice(j * LANES, (j + 1) * LANES)
        bk_ref[:, sl] = _rope128(_norm_seg128(blk(O_BK, j), g_bk), cb, sb).astype(BF16)
        bv_ref[:, sl] = blk(O_BV, j).astype(BF16)
    cq = z[:, O_CQ:O_CQ + C_Q_RANK]
    cqn = cq * lax.rsqrt(jnp.mean(cq * cq, axis=-1, keepdims=True) + EPS) * gcq_ref[...]
    q = jnp.dot(cqn.astype(BF16), wuq_ref[...], preferred_element_type=F32)
    ckv = z[:, O_CKV:O_CKV + C_KV_RANK]
    ckvn = ckv * lax.rsqrt(jnp.mean(ckv * ckv, axis=-1, keepdims=True) + EPS) * gckv_ref[...]
    kv = jnp.dot(ckvn.astype(BF16), wukv_ref[...], preferred_element_type=F32)
    krope = _rope64(_norm_low64(z[:, O_CKR:O_CKR + LANES], g_ckr), ca, saa, sab).astype(BF16)
    for hh in range(C_HEADS):
        b0 = hh * C_HEAD_PAD
        cq_ref[:, b0:b0 + LANES] = _norm_seg128(q[:, b0:b0 + LANES], g_cqn).astype(BF16)
        cq_ref[:, b0 + LANES:b0 + 2 * LANES] = _rope64(
            _norm_low64(q[:, b0 + LANES:b0 + 2 * LANES], g_cqr), ca, saa, sab).astype(BF16)
        ck_ref[:, b0:b0 + LANES] = _norm_seg128(kv[:, b0:b0 + LANES], g_ckn).astype(BF16)
        ck_ref[:, b0 + LANES:b0 + 2 * LANES] = krope
        cv_ref[:, hh * LANES:(hh + 1) * LANES] = kv[:, b0 + LANES:b0 + 2 * LANES].astype(BF16)


def _split_rows(n_a_tiles):
    first = lambda i: (jnp.minimum(i, n_a_tiles - 1), 0)
    second = lambda i: (jnp.maximum(i - n_a_tiles, 0), 0)
    return first, second


def _prep(layer, xa, xb, n_tiles, n_a_tiles, mod48, mod_row, g1, w_in, tables, rope_blk, gains, gcq, gckv,
          wuq_bf, wukv_bf):
    d = xa.shape[1]
    tm = TM_PREP
    m = n_tiles * tm
    row = lambda i: (i, 0)
    const = lambda i: (0, 0)
    first, second = _split_rows(n_a_tiles)
    tab_spec = pl.BlockSpec((tm, LANES), lambda i: (rope_blk(i), 0))
    widths = (512, 512, 512, 1024, 256, 256, C_HEADS * C_HEAD_PAD, C_HEADS * C_HEAD_PAD, 512)
    return pl.pallas_call(
        functools.partial(_prep_kernel, n_a_tiles=n_a_tiles, layer=layer),
        out_shape=[jax.ShapeDtypeStruct((m, w), BF16) for w in widths],
        grid=(n_tiles,),
        in_specs=[
            pl.BlockSpec((tm, d), first),
            pl.BlockSpec((tm, d), second),
            pl.BlockSpec((1, 1, d), lambda i: (mod_row(i) * 6 + 0, 0, 0)),
            pl.BlockSpec((1, 1, d), lambda i: (mod_row(i) * 6 + 1, 0, 0)),
            pl.BlockSpec((1, d), const),
            pl.BlockSpec(memory_space=pl.ANY),
            tab_spec, tab_spec, tab_spec, tab_spec, tab_spec,
            pl.BlockSpec((8, LANES), const),
            pl.BlockSpec((1, C_Q_RANK), const),
            pl.BlockSpec((1, C_KV_RANK), const),
            pl.BlockSpec((C_Q_RANK, C_HEADS * C_HEAD_PAD), const, pipeline_mode=pl.Buffered(1)),
            pl.BlockSpec((C_KV_RANK, C_HEADS * C_HEAD_PAD), const, pipeline_mode=pl.Buffered(1)),
        ],
        out_specs=[pl.BlockSpec((tm, w), row) for w in widths],
        scratch_shapes=[pltpu.VMEM((d, D_IN_PAD), BF16),
                        pltpu.VMEM((2, W_STAGE_ROWS, w_in.shape[2]), F32),
                        pltpu.SemaphoreType.DMA((2,))],
        compiler_params=_cparams(("arbitrary",)),
        name="prep",
    )(xa, xb, mod48, mod48, g1, w_in, *tables, gains, gcq, gckv, wuq_bf, wukv_bf)


def _dot_nt(a, b):
    return lax.dot_general(a, b, (((1,), (1,)), ((), ())), preferred_element_type=F32)


def _dot_tn(a, b):
    return lax.dot_general(a, b, (((0,), (0,)), ((), ())), preferred_element_type=F32)


def _softmax_pv_t(q, k_refs, v_refs):
    s = [_dot_nt(k[...], q) for k in k_refs]
    m = functools.reduce(jnp.maximum, [jnp.max(si, axis=0, keepdims=True) for si in s])
    e = [jnp.exp2(si - m) for si in s]
    l = functools.reduce(jnp.add, [jnp.sum(ei, axis=0, keepdims=True) for ei in e])
    o = functools.reduce(jnp.add, [_dot_tn(v[...], ei.astype(BF16)) for ei, v in zip(e, v_refs)])
    return o * (1.0 / l)


def _attn_a_kernel(*refs, n_src, lam_init):
    lv_ref, gsub_ref, q_ref = refs[0], refs[1], refs[2]
    k_refs = refs[3:3 + n_src]
    v_refs = refs[3 + n_src:3 + 2 * n_src]
    o_ref = refs[3 + 2 * n_src]
    tq = q_ref.shape[0]
    lv = lv_ref[...]
    lam = (jnp.exp(jnp.sum(lv[0:1] * lv[1:2], axis=-1, keepdims=True))
           - jnp.exp(jnp.sum(lv[2:3] * lv[3:4], axis=-1, keepdims=True)) + lam_init)
    q = q_ref[...]
    lo = lax.broadcasted_iota(jnp.int32, (1, LANES), 1) < 64
    zero = jnp.zeros_like(q)
    qq = jnp.concatenate([jnp.where(lo, q, zero), jnp.where(lo, zero, q)], axis=0)
    o2 = _softmax_pv_t(qq, k_refs, v_refs)
    o = o2[:, :tq] - lam * o2[:, tq:]
    ms = jnp.mean(o * o, axis=0, keepdims=True)
    o = (o * lax.rsqrt(ms + EPS)).T
    o_ref[...] = (o * gsub_ref[...] * (1.0 - lam_init)).astype(BF16)


def _attn_a(lv, gsub, aq, ak, av, q_row0, q_rows, srcs, n_batch, layer_idx):
    tq = min(TQ_A, q_rows)
    nq = q_rows // tq
    qb0 = q_row0 // tq
    lam_init = 0.8 - 0.6 * math.exp(-0.3 * layer_idx)
    in_specs = [
        pl.BlockSpec((4, A_DK), lambda b, h, i: (0, 0)),
        pl.BlockSpec((1, A_DV), lambda b, h, i: (0, 0)),
        pl.BlockSpec((tq, LANES), lambda b, h, i: (qb0 + b * nq + i, h)),
    ]
    kv_specs = [pl.BlockSpec((rows, LANES), lambda b, h, i, blk0=row0 // rows: (blk0 + b, h))
                for row0, rows in srcs]
    return pl.pallas_call(
        functools.partial(_attn_a_kernel, n_src=len(srcs), lam_init=lam_init),
        out_shape=jax.ShapeDtypeStruct((n_batch * q_rows, A_HEADS * A_DV), BF16),
        grid=(n_batch, A_HEADS, nq),
        in_specs=in_specs + kv_specs + kv_specs,
        out_specs=pl.BlockSpec((tq, LANES), lambda b, h, i: (b * nq + i, h)),
        compiler_params=_cparams(("arbitrary", "arbitrary", "arbitrary")),
        name="attn_a",
    )(lv, gsub, aq, *([ak] * len(srcs)), *([av] * len(srcs)))


def _attn_c_kernel(*refs, n_src):
    q_ref = refs[0]
    k_refs = refs[1:1 + n_src]
    v_refs = refs[1 + n_src:1 + 2 * n_src]
    o_ref = refs[1 + 2 * n_src]
    for hh in range(C_HEADS_PER_STEP):
        qk = slice(hh * C_HEAD_PAD, (hh + 1) * C_HEAD_PAD)
        dv = slice(hh * C_DV, (hh + 1) * C_DV)
        o = _softmax_pv_t(q_ref[:, qk], [k.at[:, qk] for k in k_refs], [v.at[:, dv] for v in v_refs])
        o_ref[:, dv] = o.T.astype(BF16)


def _attn_c(cq, ck, cv, q_row0, q_rows, srcs, n_batch):
    tq = min(TQ_C, q_rows)
    nq = q_rows // tq
    qb0 = q_row0 // tq
    hp = C_HEADS_PER_STEP
    in_specs = [pl.BlockSpec((tq, hp * C_HEAD_PAD), lambda b, h, i: (qb0 + b * nq + i, h))]
    k_specs = [pl.BlockSpec((rows, hp * C_HEAD_PAD), lambda b, h, i, blk0=row0 // rows: (blk0 + b, h))
               for row0, rows in srcs]
    v_specs = [pl.BlockSpec((rows, hp * C_DV), lambda b, h, i, blk0=row0 // rows: (blk0 + b, h))
               for row0, rows in srcs]
    return pl.pallas_call(
        functools.partial(_attn_c_kernel, n_src=len(srcs)),
        out_shape=jax.ShapeDtypeStruct((n_batch * q_rows, C_HEADS * C_DV), BF16),
        grid=(n_batch, C_HEADS // hp, nq),
        in_specs=in_specs + k_specs + v_specs,
        out_specs=pl.BlockSpec((tq, hp * C_DV), lambda b, h, i: (b * nq + i, h)),
        compiler_params=_cparams(("arbitrary", "arbitrary", "arbitrary")),
        name="attn_c",
    )(cq, *([ck] * len(srcs)), *([cv] * len(srcs)))


def _stack_heads(q):
    return jnp.concatenate([q[:, g * LANES:(g + 1) * LANES] for g in range(B_GROUP)], axis=0)


def _sink_row(sink_ref, kvh, cols):
    return jnp.concatenate(
        [jnp.full((1, cols), sink_ref[kvh * B_GROUP + g] * LOG2E, F32) for g in range(B_GROUP)], axis=1)


def _attn_b_kernel(sink_ref, q_ref, kp_ref, km_ref, kn_ref, vp_ref, vm_ref, vn_ref, kc_ref, vc_ref, o_ref):
    kvh = pl.program_id(1)
    qb = pl.program_id(2)
    nqb = pl.num_programs(2)
    n_blk = QB_B // BLOCK
    kband = jnp.concatenate([kp_ref[...], km_ref[...], kn_ref[...]], axis=0)
    vband = jnp.concatenate([vp_ref[...], vm_ref[...], vn_ref[...]], axis=0)
    kc, vc = kc_ref[...], vc_ref[...]
    sink = _sink_row(sink_ref, kvh, BLOCK)
    c = lax.broadcasted_iota(jnp.int32, (3 * BLOCK, B_GROUP * BLOCK), 0)
    r = lax.broadcasted_iota(jnp.int32, (3 * BLOCK, B_GROUP * BLOCK), 1) % BLOCK
    cr = c - r
    band_ok = (cr >= 0) & (cr <= BLOCK + WINDOW)
    for j in range(n_blk):
        q4 = _stack_heads(q_ref[j * BLOCK:(j + 1) * BLOCK, :])
        s_loc = _dot_nt(kband[j * BLOCK:(j + 3) * BLOCK], q4)
        valid = band_ok
        if j == 0:
            valid = valid & (c >= jnp.where(qb > 0, 0, BLOCK))
        if j == n_blk - 1:
            valid = valid & (c < jnp.where(qb < nqb - 1, 3 * BLOCK, 2 * BLOCK))
        s_loc = jnp.where(valid, s_loc, NEG_INF)
        s_ctx = _dot_nt(kc, q4)
        m = jnp.maximum(jnp.maximum(jnp.max(s_loc, axis=0, keepdims=True),
                                    jnp.max(s_ctx, axis=0, keepdims=True)), sink)
        e_loc, e_ctx = jnp.exp2(s_loc - m), jnp.exp2(s_ctx - m)
        l = (jnp.sum(e_loc, axis=0, keepdims=True) + jnp.sum(e_ctx, axis=0, keepdims=True)
             + jnp.exp2(sink - m))
        o = (_dot_tn(vband[j * BLOCK:(j + 3) * BLOCK], e_loc.astype(BF16))
             + _dot_tn(vc, e_ctx.astype(BF16))) * (1.0 / l)
        o = o.T
        for g in range(B_GROUP):
            o_ref[j * BLOCK:(j + 1) * BLOCK, g * LANES:(g + 1) * LANES] = (
                o[g * BLOCK:(g + 1) * BLOCK].astype(BF16))


def _attn_b(sink, bq, bk, bv, n_batch, n_lat, n_ctx):
    nqb = n_lat // QB_B
    per = QB_B // BLOCK
    blocks_per_batch = n_lat // BLOCK
    ctx_blk0 = n_batch * n_lat // n_ctx
    gw = B_GROUP * B_DH
    prev = lambda b, h, i: (b * blocks_per_batch + jnp.maximum(i * per - 1, 0), h)
    main = lambda b, h, i: (b * nqb + i, h)
    nxt = lambda b, h, i: (b * blocks_per_batch + jnp.minimum(i * per + per, blocks_per_batch - 1), h)
    ctx = lambda b, h, i: (ctx_blk0 + b, h)
    return pl.pallas_call(
        _attn_b_kernel,
        out_shape=jax.ShapeDtypeStruct((n_batch * n_lat, B_HEADS * B_DH), BF16),
        grid=(n_batch, B_KV_HEADS, nqb),
        in_specs=[
            pl.BlockSpec(memory_space=pltpu.SMEM),
            pl.BlockSpec((QB_B, gw), main),
            pl.BlockSpec((BLOCK, B_DH), prev), pl.BlockSpec((QB_B, B_DH), main), pl.BlockSpec((BLOCK, B_DH), nxt),
            pl.BlockSpec((BLOCK, B_DH), prev), pl.BlockSpec((QB_B, B_DH), main), pl.BlockSpec((BLOCK, B_DH), nxt),
            pl.BlockSpec((n_ctx, B_DH), ctx), pl.BlockSpec((n_ctx, B_DH), ctx),
        ],
        out_specs=pl.BlockSpec((QB_B, gw), main),
        compiler_params=_cparams(("arbitrary", "arbitrary", "arbitrary")),
        name="attn_b",
    )(sink, bq, bk, bk, bk, bv, bv, bv, bk, bv)


def _attn_b_ctx_kernel(sink_ref, q_ref, k_ref, v_ref, o_ref):
    kvh = pl.program_id(1)
    rows = q_ref.shape[0]
    q4 = _stack_heads(q_ref[...])
    sink = _sink_row(sink_ref, kvh, rows)
    s = _dot_nt(k_ref[...], q4)
    m = jnp.maximum(jnp.max(s, axis=0, keepdims=True), sink)
    e = jnp.exp2(s - m)
    l = jnp.sum(e, axis=0, keepdims=True) + jnp.exp2(sink - m)
    o = (_dot_tn(v_ref[...], e.astype(BF16)) * (1.0 / l)).T
    for g in range(B_GROUP):
        o_ref[:, g * LANES:(g + 1) * LANES] = o[g * rows:(g + 1) * rows].astype(BF16)


def _attn_b_ctx(sink, bq, bk, bv, n_batch, n_lat, n_ctx):
    gw = B_GROUP * B_DH
    ctx_blk0 = n_batch * n_lat // n_ctx
    return pl.pallas_call(
        _attn_b_ctx_kernel,
        out_shape=jax.ShapeDtypeStruct((n_batch * n_ctx, B_HEADS * B_DH), BF16),
        grid=(n_batch, B_KV_HEADS),
        in_specs=[
            pl.BlockSpec(memory_space=pltpu.SMEM),
            pl.BlockSpec((n_ctx, gw), lambda b, h: (ctx_blk0 + b, h)),
            pl.BlockSpec((n_ctx, B_DH), lambda b, h: (ctx_blk0 + b, h)),
            pl.BlockSpec((n_ctx, B_DH), lambda b, h: (ctx_blk0 + b, h)),
        ],
        out_specs=pl.BlockSpec((n_ctx, gw), lambda b, h: (b, h)),
        compiler_params=_cparams(("arbitrary", "arbitrary")),
        name="attn_b_ctx",
    )(sink, bq, bk, bv)


def _route(h2, rw_ref, rb_ref):
    tm = h2.shape[0]
    h_hi = h2.astype(BF16)
    h_lo = (h2 - h_hi.astype(F32)).astype(BF16)
    p = jnp.dot(jnp.concatenate([h_hi, h_lo], axis=0), rw_ref[...], preferred_element_type=F32)
    p = p[:tm] + p[tm:]
    logits = p[:, :N_EXPERTS] + p[:, N_EXPERTS:]
    scores = 1.0 / (1.0 + jnp.exp(-logits))
    sel = scores + rb_ref[...]
    lane_i = lax.broadcasted_iota(jnp.int32, sel.shape, 1)
    lane = lane_i.astype(F32)
    big = float(N_EXPERTS)

    def top2(mask):
        v = jnp.where(mask, sel, -jnp.inf)
        m1 = jnp.max(v, axis=-1, keepdims=True)
        i1 = jnp.min(jnp.where(v == m1, lane, big), axis=-1, keepdims=True)
        v2 = jnp.where(lane == i1, -jnp.inf, v)
        m2 = jnp.max(v2, axis=-1, keepdims=True)
        i2 = jnp.min(jnp.where(v2 == m2, lane, big), axis=-1, keepdims=True)
        return m1, i1, m2, i2

    best = None
    for g in range(N_GROUPS):
        m1, i1, m2, i2 = top2((lane_i >= g * EXPERTS_PER_GROUP) & (lane_i < (g + 1) * EXPERTS_PER_GROUP))
        gs = m1 + m2
        if best is None:
            best = (gs, i1, i2)
        else:
            take = gs > best[0]
            best = (jnp.where(take, gs, best[0]), jnp.where(take, i1, best[1]), jnp.where(take, i2, best[2]))
    _, e1, e2 = best
    w1 = jnp.sum(jnp.where(lane == e1, scores, 0.0), axis=-1, keepdims=True)
    w2 = jnp.sum(jnp.where(lane == e2, scores, 0.0), axis=-1, keepdims=True)
    tot = w1 + w2
    return e1, e2, w1 / tot, w2 / tot


def _outproj_kernel(oa1_ref, ob1_ref, oc1_ref, x1in_ref, oa2_ref, ob2_ref, oc2_ref, x2in_ref,
                    g1_ref, sh2_ref, sc2_ref, n2_ref, w_hbm, rw_ref, rb_ref,
                    x1_ref, h2_ref, idx_ref, wt_ref, cnt_ref, w_ref, w_stage, w_sem, cnt_acc,
                    *, n_a_tiles, layer):
    @pl.when(pl.program_id(0) == 0)
    def _():
        _load_weight_bf16(w_hbm, layer, w_stage, w_sem, w_ref)

    first = pl.program_id(0) < n_a_tiles
    oa = jnp.where(first, oa1_ref[...], oa2_ref[...])
    ob = jnp.where(first, ob1_ref[...], ob2_ref[...])
    oc = jnp.where(first, oc1_ref[...], oc2_ref[...])
    x = jnp.where(first, x1in_ref[...], x2in_ref[...])
    y = jnp.dot(jnp.concatenate([oa, ob, oc], axis=1), w_ref[...], preferred_element_type=F32)
    x1 = x + g1_ref[0] * y
    x1_ref[...] = x1
    ms = jnp.mean(x1 * x1, axis=-1, keepdims=True)
    h2 = x1 * lax.rsqrt(ms + EPS) * n2_ref[...]
    h2 = h2 * (1.0 + sc2_ref[0]) + sh2_ref[0]
    tm = h2.shape[0]
    for cc in range(ROW_CHUNKS):
        h2_ref[pl.ds(cc, tm, stride=ROW_CHUNKS), :] = h2[:, cc * LANES:(cc + 1) * LANES]
    e1, e2, w1, w2 = _route(h2, rw_ref, rb_ref)
    lane = lax.broadcasted_iota(jnp.int32, idx_ref.shape, 1)
    wt_ref[...] = jnp.where(lane == 0, w1, jnp.where(lane == 1, w2, 0.0))

    @pl.when(pl.program_id(0) == 0)
    def _():
        cnt_acc[...] = jnp.zeros_like(cnt_acc)

    lane_f = lane.astype(F32)
    sel1, sel2 = lane_f == e1, lane_f == e2
    picks = jnp.where(sel1, 1.0, 0.0) + jnp.where(sel2, 1.0, 0.0)
    t_row = lax.broadcasted_iota(jnp.int32, (tm, tm), 0)
    t_col = lax.broadcasted_iota(jnp.int32, (tm, tm), 1)
    earlier = jnp.where(t_col < t_row, 1.0, 0.0).astype(BF16)
    before = jnp.dot(earlier, picks.astype(BF16), preferred_element_type=F32) + cnt_acc[0:1, :]
    r1 = jnp.sum(jnp.where(sel1, before, 0.0), axis=-1, keepdims=True)
    r2 = jnp.sum(jnp.where(sel2, before, 0.0), axis=-1, keepdims=True)
    idx_ref[...] = jnp.where(lane == 0, e1, jnp.where(lane == 1, e2, jnp.where(
        lane == 2, r1, jnp.where(lane == 3, r2, 0.0)))).astype(jnp.int32)
    cnt_acc[...] = cnt_acc[...] + jnp.sum(picks, axis=0, keepdims=True)
    cnt_ref[...] = cnt_acc[...]


def _outproj(layer, set_a, set_b, n_tiles, n_a_tiles, mod48, mod_row, n2, w_out, router_w, router_b):
    d = set_a[3].shape[1]
    tm = TM_PREP
    m = n_tiles * tm
    row = lambda i: (i, 0)
    const = lambda i: (0, 0)
    first, second = _split_rows(n_a_tiles)
    mod_spec = lambda j: pl.BlockSpec((1, 1, d), lambda i: (mod_row(i) * 6 + j, 0, 0))
    in_specs = (
        [pl.BlockSpec((tm, a.shape[1]), first) for a in set_a]
        + [pl.BlockSpec((tm, a.shape[1]), second) for a in set_b]
        + [mod_spec(2), mod_spec(3), mod_spec(4),
           pl.BlockSpec((1, d), const),
           pl.BlockSpec(memory_space=pl.ANY),
           pl.BlockSpec((d, 2 * N_EXPERTS), const),
           pl.BlockSpec((1, N_EXPERTS), const)])
    return pl.pallas_call(
        functools.partial(_outproj_kernel, n_a_tiles=n_a_tiles, layer=layer),
        out_shape=[jax.ShapeDtypeStruct((m, d), F32), jax.ShapeDtypeStruct((m * ROW_CHUNKS, LANES), F32),
                   jax.ShapeDtypeStruct((m, LANES), jnp.int32), jax.ShapeDtypeStruct((m, LANES), F32),
                   jax.ShapeDtypeStruct((8, LANES), F32)],
        grid=(n_tiles,),
        in_specs=in_specs,
        out_specs=[pl.BlockSpec((tm, d), row), pl.BlockSpec((tm * ROW_CHUNKS, LANES), row),
                   pl.BlockSpec((tm, LANES), row), pl.BlockSpec((tm, LANES), row),
                   pl.BlockSpec((8, LANES), const)],
        scratch_shapes=[pltpu.VMEM((d, d), BF16),
                        pltpu.VMEM((2, W_STAGE_ROWS, d), F32),
                        pltpu.SemaphoreType.DMA((2,)),
                        pltpu.VMEM((8, LANES), F32)],
        compiler_params=_cparams(("arbitrary",)),
        name="outproj",
    )(*set_a, *set_b, mod48, mod48, mod48, n2, w_out, router_w, router_b)


def _experts_kernel(te_ref, nxt_ref, ws_ref, nv_ref, src_ref, h2_hbm, wg_hbm, wu_hbm, wd_hbm, o_ref,
                    xbuf, xsem, wg_f, wu_f, wd_f, wsem, wg_bf, wu_bf, wd_bf, *, layer):
    i = pl.program_id(0)
    n_valid = nv_ref[0]
    slot = i % X_SLOTS

    def issue_rows(tile):
        s = tile % X_SLOTS
        base = tile * TM_E
        for r in range(TM_E):
            row0 = pl.multiple_of(src_ref[base + r] * ROW_CHUNKS, ROW_CHUNKS)
            pltpu.make_async_copy(h2_hbm.at[pl.ds(row0, ROW_CHUNKS), :],
                                  xbuf.at[s, pl.ds(r * X_PITCH, ROW_CHUNKS), :], xsem.at[s]).start()

    def wait_rows(s):
        pltpu.make_async_copy(h2_hbm.at[pl.ds(0, TM_E * ROW_CHUNKS), :],
                              xbuf.at[s, pl.ds(0, TM_E * ROW_CHUNKS), :], xsem.at[s]).wait()

    def weight_copies(e, s):
        return (pltpu.make_async_copy(wg_hbm.at[layer, e], wg_f.at[s], wsem.at[s]),
                pltpu.make_async_copy(wu_hbm.at[layer, e], wu_f.at[s], wsem.at[s]),
                pltpu.make_async_copy(wd_hbm.at[layer, e], wd_f.at[s], wsem.at[s]))

    @pl.when(i == 0)
    def _():
        for cp in weight_copies(te_ref[0], ws_ref[0]):
            cp.start()
        for t in range(X_SLOTS - 1):
            issue_rows(t)

    first_of_expert = jnp.logical_or(i == 0, te_ref[i] != te_ref[jnp.maximum(i - 1, 0)])

    @pl.when(jnp.logical_and(first_of_expert, i < n_valid))
    def _():
        ws = ws_ref[i]
        for cp in weight_copies(te_ref[i], ws):
            cp.wait()

        @pl.when(nxt_ref[i] >= 0)
        def _():
            for cp in weight_copies(nxt_ref[i], 1 - ws):
                cp.start(priority=WEIGHT_DMA_PRIORITY)

        wg_bf[...] = wg_f[ws].astype(BF16)
        wu_bf[...] = wu_f[ws].astype(BF16)
        wd_bf[...] = wd_f[ws].astype(BF16)

    @pl.when(i < n_valid)
    def _():
        wait_rows(slot)
        x = jnp.concatenate([xbuf[slot, pl.ds(cc, TM_E, stride=X_PITCH), :] for cc in range(ROW_CHUNKS)],
                            axis=1).astype(BF16)
        issue_rows(i + X_SLOTS - 1)
        g = jnp.dot(x, wg_bf[...], preferred_element_type=F32)
        u = jnp.dot(x, wu_bf[...], preferred_element_type=F32)
        a = (_silu(g) * u).astype(BF16)
        y = jnp.dot(a, wd_bf[...], preferred_element_type=F32)
        for cc in range(ROW_CHUNKS):
            o_ref[pl.ds(cc, TM_E, stride=ROW_CHUNKS), :] = y[:, cc * LANES:(cc + 1) * LANES]

    @pl.when(i == n_valid - 1)
    def _():
        for t in range(1, X_SLOTS):
            wait_rows((i + t) % X_SLOTS)

    @pl.when(i >= n_valid)
    def _():
        o_ref[...] = jnp.zeros_like(o_ref)


def _experts(layer, tile_expert, next_expert, w_slot, n_valid, src, h2_all, w_gate, w_up, w_down):
    n_rows = src.shape[0] - (X_SLOTS - 1) * TM_E
    nt = n_rows // TM_E
    d, de = w_gate.shape[2], w_gate.shape[3]
    grid_spec = pltpu.PrefetchScalarGridSpec(
        num_scalar_prefetch=5,
        grid=(nt,),
        in_specs=[pl.BlockSpec(memory_space=pl.ANY)] * 4,
        out_specs=pl.BlockSpec((TM_E * ROW_CHUNKS, LANES), lambda i, *_: (i, 0)),
        scratch_shapes=[
            pltpu.VMEM((X_SLOTS, TM_E * X_PITCH, LANES), F32),
            pltpu.SemaphoreType.DMA((X_SLOTS,)),
            pltpu.VMEM((2, d, de), F32), pltpu.VMEM((2, d, de), F32), pltpu.VMEM((2, de, d), F32),
            pltpu.SemaphoreType.DMA((2,)),
            pltpu.VMEM((d, de), BF16), pltpu.VMEM((d, de), BF16), pltpu.VMEM((de, d), BF16),
        ],
    )
    return pl.pallas_call(
        functools.partial(_experts_kernel, layer=layer),
        out_shape=jax.ShapeDtypeStruct((n_rows * ROW_CHUNKS, LANES), F32),
        grid_spec=grid_spec,
        compiler_params=_cparams(("arbitrary",)),
        name="experts",
    )(tile_expert, next_expert, w_slot, n_valid, src, h2_all, w_gate, w_up, w_down)


def _combine_kernel(pos_ref, o_hbm, x1_ref, wt_ref, g2_ref, x2_ref, buf, sem):
    i = pl.program_id(0)
    nt = pl.num_programs(0)
    slot = i % 2

    def issue(tile, s):
        base = tile * (TM_C * TOP_K)
        for r in range(TM_C):
            for k in range(TOP_K):
                row0 = pl.multiple_of(pos_ref[base + TOP_K * r + k] * ROW_CHUNKS, ROW_CHUNKS)
                pltpu.make_async_copy(o_hbm.at[pl.ds(row0, ROW_CHUNKS), :],
                                      buf.at[s, k, pl.ds(r * X_PITCH, ROW_CHUNKS), :], sem.at[s]
                                      ).start(priority=k % DMA_QUEUES)

    def wait(s):
        for k in range(TOP_K):
            pltpu.make_async_copy(o_hbm.at[pl.ds(0, TM_C * ROW_CHUNKS), :],
                                  buf.at[s, k, pl.ds(0, TM_C * ROW_CHUNKS), :], sem.at[s]).wait()

    @pl.when(i == 0)
    def _():
        issue(0, 0)

    def expert_rows(k):
        return jnp.concatenate([buf[slot, k, pl.ds(cc, TM_C, stride=X_PITCH), :] for cc in range(ROW_CHUNKS)],
                               axis=1)

    wait(slot)
    wt = wt_ref[...]
    x2_ref[...] = x1_ref[...] + g2_ref[0] * (wt[:, 0:1] * expert_rows(0) + wt[:, 1:2] * expert_rows(1))
    issue(i + 1, 1 - slot)

    @pl.when(i == nt - 1)
    def _():
        wait(1 - slot)


def _combine(pos, o_sorted, x1, wts, mod48, mod_row):
    m, d = x1.shape
    pos = jnp.concatenate([pos, jnp.zeros((TM_C * TOP_K,), jnp.int32)])
    grid_spec = pltpu.PrefetchScalarGridSpec(
        num_scalar_prefetch=1,
        grid=(m // TM_C,),
        in_specs=[
            pl.BlockSpec(memory_space=pl.ANY),
            pl.BlockSpec((TM_C, d), lambda i, p: (i, 0)),
            pl.BlockSpec((TM_C, LANES), lambda i, p: (i, 0)),
            pl.BlockSpec((1, 1, d), lambda i, p: (mod_row(i) * 6 + 5, 0, 0)),
        ],
        out_specs=pl.BlockSpec((TM_C, d), lambda i, p: (i, 0)),
        scratch_shapes=[pltpu.VMEM((2, TOP_K, TM_C * X_PITCH, LANES), F32), pltpu.SemaphoreType.DMA((2,))],
    )
    return pl.pallas_call(
        _combine_kernel,
        out_shape=jax.ShapeDtypeStruct((m, d), F32),
        grid_spec=grid_spec,
        compiler_params=_cparams(("arbitrary",)),
        name="combine",
    )(pos, o_sorted, x1, wts, mod48)


def _rope_tables(n_lat, tm):
    f32 = np.float32
    t = np.arange(n_lat)
    r = (t // GRID_W).astype(f32)
    col = (t % GRID_W).astype(f32)

    def cos_sin(dim):
        nf = dim // 4
        inv = (f32(ROPE_THETA) ** (-np.arange(nf, dtype=f32) / f32(nf))).astype(f32)
        ang = np.concatenate([r[:, None] * inv, col[:, None] * inv], axis=-1).astype(f32)
        return np.cos(ang).astype(f32), np.sin(ang).astype(f32)

    c64, s64 = cos_sin(A_DK)
    c128, s128 = cos_sin(B_DH)
    z32 = np.zeros_like(s64)
    tabs = [
        np.concatenate([c64, c64, c64, c64], axis=-1),
        np.concatenate([-s64, z32, -s64, z32], axis=-1),
        np.concatenate([z32, s64, z32, s64], axis=-1),
        np.concatenate([c128, c128], axis=-1),
        np.concatenate([-s128, s128], axis=-1),
    ]
    ident = [np.ones((tm, LANES), f32), np.zeros((tm, LANES), f32), np.zeros((tm, LANES), f32),
             np.ones((tm, LANES), f32), np.zeros((tm, LANES), f32)]
    return [jnp.asarray(np.concatenate([a, b], axis=0)) for a, b in zip(tabs, ident)]


def _layer_params(l, a_qn, a_kn, b_qn, b_kn, c_qa_norm, c_kva_norm, c_wuq, c_wukv, c_qn, c_kn):
    z64 = jnp.zeros((C_ROPE,), F32)
    gains = jnp.stack([
        jnp.tile(a_qn[l], 2) * (A_DK ** -0.5 * LOG2E),
        jnp.tile(a_kn[l], 2),
        b_qn[l] * (B_DH ** -0.5 * LOG2E),
        b_kn[l],
        c_qn[l][:C_NOPE] * (C_DQK ** -0.5 * LOG2E),
        jnp.concatenate([c_qn[l][C_NOPE:] * (C_DQK ** -0.5 * LOG2E), z64]),
        c_kn[l][:C_NOPE],
        jnp.concatenate([c_kn[l][C_NOPE:], z64]),
    ])
    wq = c_wuq[l].reshape(C_Q_RANK, C_HEADS, C_DQK)
    wq = jnp.pad(wq, ((0, 0), (0, 0), (0, C_HEAD_PAD - C_DQK))).reshape(C_Q_RANK, C_HEADS * C_HEAD_PAD)
    return dict(gains=gains, gcq=c_qa_norm[l][None], gckv=c_kva_norm[l][None],
                wuq=wq.astype(BF16), wukv=c_wukv[l].astype(BF16))


def _sorted_rows(idx, rank, counts):
    t = idx.shape[0]
    n_pairs = t * TOP_K
    n_rows = ((n_pairs + N_EXPERTS * (TM_E - 1)) // TM_E) * TM_E
    nt = n_rows // TM_E
    flat_e = idx.reshape(-1)
    rank = rank.reshape(-1)
    padded = ((counts + TM_E - 1) // TM_E) * TM_E
    ends = jnp.cumsum(padded)
    pos = (ends - padded)[flat_e] + rank
    src = jnp.zeros((n_rows + (X_SLOTS - 1) * TM_E,), jnp.int32).at[pos].set(
        jnp.arange(n_pairs, dtype=jnp.int32) // TOP_K, unique_indices=True)
    tile_start = jnp.arange(nt, dtype=jnp.int32) * TM_E
    tile_expert = jnp.sum((ends[None, :] <= tile_start[:, None]).astype(jnp.int32), axis=1)
    last_used = jnp.sum((ends <= ends[-1] - 1).astype(jnp.int32))
    tile_expert = jnp.minimum(tile_expert, last_used)
    n_valid = (ends[-1:] // TM_E).astype(jnp.int32)
    used = counts > 0
    e_ids = jnp.arange(N_EXPERTS, dtype=jnp.int32)
    later = jnp.where(used[None, :] & (e_ids[None, :] > e_ids[:, None]), e_ids[None, :], N_EXPERTS)
    next_used = jnp.min(later, axis=1)
    next_used = jnp.where(next_used == N_EXPERTS, -1, next_used).astype(jnp.int32)
    parity = ((jnp.cumsum(used.astype(jnp.int32)) - 1) % 2).astype(jnp.int32)
    return (pos.astype(jnp.int32), src, tile_expert.astype(jnp.int32), next_used[tile_expert],
            parity[tile_expert], n_valid)


def kernel(x, c, ctx, c_ctx, ada_w, ada_b, norm1_g, norm2_g, w_in, w_out, a_qn, a_kn, a_lambda, a_subln,
           b_qn, b_kn, b_sink, c_qa_norm, c_kva_norm, c_wuq, c_wukv, c_qn, c_kn,
           router_w, router_bias, moe_w_gate, moe_w_up, moe_w_down):
    bsz, n_lat, d = x.shape
    n_ctx = ctx.shape[1]
    depth = ada_w.shape[0]
    t_lat, t_ctx = bsz * n_lat, bsz * n_ctx
    tm = TM_PREP
    lat_tiles = n_lat // tm
    n_lat_tiles, n_ctx_tiles = t_lat // tm, t_ctx // tm
    n_all_tiles = n_lat_tiles + n_ctx_tiles

    cond8 = jnp.concatenate([c, c_ctx[None], jnp.zeros((8 - bsz - 1, d), F32)], axis=0)
    mod = _ada_modulation(cond8, ada_w, ada_b)
    tables = _rope_tables(n_lat, tm)
    rb = router_bias[None]
    rw_hi = router_w.astype(BF16)
    rw_lo = (router_w - rw_hi.astype(F32)).astype(BF16)
    rw2 = jnp.concatenate([rw_hi, rw_lo], axis=1)

    mod_row = lambda i: jnp.minimum(i // lat_tiles, bsz)
    mod_row_c = lambda i: jnp.minimum(i // (n_lat // TM_C), bsz)
    rope_blk = lambda i: jnp.where(i < n_lat_tiles, i % lat_tiles, lat_tiles)

    xa, xb, n_a_tiles = x.reshape(t_lat, d), ctx.reshape(t_ctx, d), n_lat_tiles
    lat_src, ctx_src = (0, n_lat), (t_lat, n_ctx)
    for l in range(depth):
        last = l == depth - 1
        p = _layer_params(l, a_qn, a_kn, b_qn, b_kn, c_qa_norm, c_kva_norm, c_wuq, c_wukv, c_qn, c_kn)
        mod48 = mod[l].reshape(8 * 6, 1, d)
        aq, ak, av, bq, bk, bv, cq, ck, cv = _prep(
            l, xa, xb, n_all_tiles, n_a_tiles, mod48, mod_row, norm1_g[l][None], w_in, tables, rope_blk,
            p["gains"], p["gcq"], p["gckv"], p["wuq"], p["wukv"])

        lv, gsub, sink = a_lambda[l], a_subln[l][None], b_sink[l]
        o_a = _attn_a(lv, gsub, aq, ak, av, 0, n_lat, [lat_src, ctx_src], bsz, l)
        o_b = _attn_b(sink, bq, bk, bv, bsz, n_lat, n_ctx)
        o_c = _attn_c(cq, ck, cv, 0, n_lat, [lat_src, ctx_src], bsz)
        set_a = (o_a, o_b, o_c, xa)
        if not last:
            oc_a = _attn_a(lv, gsub, aq, ak, av, t_lat, n_ctx, [ctx_src], bsz, l)
            oc_b = _attn_b_ctx(sink, bq, bk, bv, bsz, n_lat, n_ctx)
            oc_c = _attn_c(cq, ck, cv, t_lat, n_ctx, [ctx_src], bsz)
            set_b, n_tok_tiles = (oc_a, oc_b, oc_c, xb), n_all_tiles
        else:
            set_b, n_tok_tiles = set_a, n_lat_tiles
        x1, h2, idx, wts, cnt = _outproj(l, set_a, set_b, n_tok_tiles, min(n_a_tiles, n_tok_tiles), mod48,
                                         mod_row, norm2_g[l][None], w_out, rw2, rb)

        pos, src, tile_expert, next_expert, w_slot, n_valid = _sorted_rows(
            idx[:, :TOP_K], idx[:, TOP_K:2 * TOP_K], cnt[0, :N_EXPERTS].astype(jnp.int32))
        o_sorted = _experts(l, tile_expert, next_expert, w_slot, n_valid, src, h2,
                            moe_w_gate, moe_w_up, moe_w_down)
        xa = _combine(pos, o_sorted, x1, wts, mod48, mod_row_c)
        xb, n_a_tiles = xa, n_all_tiles
    return xa.reshape(bsz, n_lat, d)
```

```python
import functools
import math

import jax
import jax.numpy as jnp
import numpy as np
from jax import lax
from jax.experimental import pallas as pl
from jax.experimental.pallas import tpu as pltpu

F32 = jnp.float32
BF16 = jnp.bfloat16

D_MODEL = 2048
GRID_W = 64
BLOCK = 128
WINDOW = 128
ROPE_THETA = 10000.0
EPS = 1e-6
NEG_INF = -1e30
LOG2E = math.log2(math.e)
A_HEADS, A_DK = 4, 64
A_DV = 2 * A_DK
B_HEADS, B_KV_HEADS, B_DH = 8, 2, 128
B_GROUP = B_HEADS // B_KV_HEADS
C_HEADS, C_Q_RANK, C_KV_RANK, C_NOPE, C_ROPE, C_DV = 4, 512, 256, 128, 64, 128
C_DQK = C_NOPE + C_ROPE
SPLIT_SIZES = (A_HEADS * 2 * A_DK, A_HEADS * 2 * A_DK, A_HEADS * A_DV,
               B_HEADS * B_DH, B_KV_HEADS * B_DH, B_KV_HEADS * B_DH,
               C_Q_RANK, C_KV_RANK, C_ROPE)
D_IN = sum(SPLIT_SIZES)
N_EXPERTS, N_GROUPS, TOP_K = 32, 4, 2
EXPERTS_PER_GROUP = N_EXPERTS // N_GROUPS
D_EXPERT = 512

LANES = 128
V7X_VMEM_LIMIT = 56 * 1024 * 1024

D_IN_PAD = ((D_IN + LANES - 1) // LANES) * LANES
C_HEAD_PAD = 2 * LANES
TM_PREP = 256
TQ_A = 512
TQ_C = 1024
C_HEADS_PER_STEP = 1
QB_B = 512
TM_E = 128
X_SLOTS = 3
ROW_CHUNKS = D_MODEL // LANES
X_PITCH = ROW_CHUNKS + 8
DMA_QUEUES = 2
WEIGHT_DMA_PRIORITY = 1
TM_C = 128
C_SLOTS = 3
ADA_TN = 1024
W_STAGE_ROWS = 128

_OFF = [0]
for _s in SPLIT_SIZES:
    _OFF.append(_OFF[-1] + _s)
O_AQ, O_AK, O_AV, O_BQ, O_BK, O_BV, O_CQ, O_CKV, O_CKR, _ = _OFF


def _cparams(sem):
    return pltpu.CompilerParams(dimension_semantics=sem, vmem_limit_bytes=V7X_VMEM_LIMIT)


def _silu(v):
    return v * (1.0 / (1.0 + jnp.exp(-v)))


def _ada_kernel(cond_ref, w_ref, b_ref, o_ref):
    s = _silu(cond_ref[...]).astype(BF16)
    o_ref[0] = jnp.dot(s, w_ref[0].astype(BF16), preferred_element_type=F32) + b_ref[0]


def _ada_modulation(cond8, ada_w, ada_b):
    depth, d, n = ada_w.shape
    return pl.pallas_call(
        _ada_kernel,
        out_shape=jax.ShapeDtypeStruct((depth, 8, n), F32),
        grid=(depth, n // ADA_TN),
        in_specs=[
            pl.BlockSpec((8, d), lambda l, j: (0, 0)),
            pl.BlockSpec((1, d, ADA_TN), lambda l, j: (l, 0, j)),
            pl.BlockSpec((1, 1, ADA_TN), lambda l, j: (l, 0, j)),
        ],
        out_specs=pl.BlockSpec((1, 8, ADA_TN), lambda l, j: (l, 0, j)),
        compiler_params=_cparams(("arbitrary", "arbitrary")),
        name="ada_modulation",
    )(cond8, ada_w, ada_b.reshape(depth, 1, n))


def _rope64(v, c, sa, sb):
    return v * c + pltpu.roll(v, 96, 1) * sa + pltpu.roll(v, 32, 1) * sb


def _rope128(v, c, s):
    return v * c + pltpu.roll(v, 64, 1) * s


def _norm_seg128(v, g):
    ms = jnp.sum(v * v, axis=-1, keepdims=True) * (1.0 / 128)
    return v * lax.rsqrt(ms + EPS) * g


def _norm_seg64x2(v, g, lo):
    sq = v * v
    s_lo = jnp.sum(jnp.where(lo, sq, 0.0), axis=-1, keepdims=True)
    s_hi = jnp.sum(jnp.where(lo, 0.0, sq), axis=-1, keepdims=True)
    ms = jnp.where(lo, s_lo, s_hi) * (1.0 / 64)
    return v * lax.rsqrt(ms + EPS) * g


def _norm_low64(v, g):
    ms = jnp.sum(v * v, axis=-1, keepdims=True) * (1.0 / 64)
    return v * lax.rsqrt(ms + EPS) * g


def _load_weight_bf16(w_hbm, layer, stage, sem, w_bf):
    k, n = w_hbm.shape[1], w_hbm.shape[2]
    ch = stage.shape[1]

    def chunk_copy(c):
        return pltpu.make_async_copy(w_hbm.at[layer, pl.ds(c * ch, ch), :], stage.at[c % 2], sem.at[c % 2])

    n_pad = w_bf.shape[1]
    if n_pad > n:
        edge = (n // LANES) * LANES
        w_bf[:, edge:n_pad] = jnp.zeros((k, n_pad - edge), BF16)
    chunk_copy(0).start()
    for c in range(k // ch):
        if c + 1 < k // ch:
            chunk_copy(c + 1).start()
        chunk_copy(c).wait()
        w_bf[c * ch:(c + 1) * ch, 0:n] = stage[c % 2].astype(BF16)


def _prep_kernel(xa_ref, xb_ref, sh_ref, sc_ref, g1_ref, w_hbm, ca_ref, saa_ref, sab_ref, cb_ref, sb_ref,
                 gains_ref, gcq_ref, gckv_ref, wuq_ref, wukv_ref,
                 aq_ref, ak_ref, av_ref, bq_ref, bk_ref, bv_ref, cq_ref, ck_ref, cv_ref,
                 w_bf, w_stage, w_sem, *, n_a_tiles, layer):
    @pl.when(pl.program_id(0) == 0)
    def _():
        _load_weight_bf16(w_hbm, layer, w_stage, w_sem, w_bf)

    x = jnp.where(pl.program_id(0) < n_a_tiles, xa_ref[...], xb_ref[...])
    ms = jnp.mean(x * x, axis=-1, keepdims=True)
    h = x * lax.rsqrt(ms + EPS) * g1_ref[...]
    h = h * (1.0 + sc_ref[0]) + sh_ref[0]
    z = jnp.dot(h.astype(BF16), w_bf[...], preferred_element_type=F32)

    lane = lax.broadcasted_iota(jnp.int32, (1, LANES), 1)
    lo = lane < 64
    ca, saa, sab = ca_ref[...], saa_ref[...], sab_ref[...]
    cb, sb = cb_ref[...], sb_ref[...]
    g_aq, g_ak, g_bq, g_bk = gains_ref[0:1], gains_ref[1:2], gains_ref[2:3], gains_ref[3:4]
    g_cqn, g_cqr, g_ckn, g_ckr = gains_ref[4:5], gains_ref[5:6], gains_ref[6:7], gains_ref[7:8]

    def blk(off, j):
        return z[:, off + j * LANES: off + (j + 1) * LANES]

    for j in range(A_HEADS):
        sl = slice(j * LANES, (j + 1) * LANES)
        aq_ref[:, sl] = _rope64(_norm_seg64x2(blk(O_AQ, j), g_aq, lo), ca, saa, sab).astype(BF16)
        ak_ref[:, sl] = _rope64(_norm_seg64x2(blk(O_AK, j), g_ak, lo), ca, saa, sab).astype(BF16)
        av_ref[:, sl] = blk(O_AV, j).astype(BF16)
    for j in range(B_HEADS):
        sl = slice(j * LANES, (j + 1) * LANES)
        bq_ref[:, sl] = _rope128(_norm_seg128(blk(O_BQ, j), g_bq), cb, sb).astype(BF16)
    for j in range(B_KV_HEADS):
        sl = slice(j * LANES, (j + 1) * LANES)
        bk_ref[:, sl] = _rope128(_norm_seg128(blk(O_BK, j), g_bk), cb, sb).astype(BF16)
        bv_ref[:, sl] = blk(O_BV, j).astype(BF16)
    cq = z[:, O_CQ:O_CQ + C_Q_RANK]
    cqn = cq * lax.rsqrt(jnp.mean(cq * cq, axis=-1, keepdims=True) + EPS) * gcq_ref[...]
    q = jnp.dot(cqn.astype(BF16), wuq_ref[...], preferred_element_type=F32)
    ckv = z[:, O_CKV:O_CKV + C_KV_RANK]
    ckvn = ckv * lax.rsqrt(jnp.mean(ckv * ckv, axis=-1, keepdims=True) + EPS) * gckv_ref[...]
    kv = jnp.dot(ckvn.astype(BF16), wukv_ref[...], preferred_element_type=F32)
    krope = _rope64(_norm_low64(z[:, O_CKR:O_CKR + LANES], g_ckr), ca, saa, sab).astype(BF16)
    for hh in range(C_HEADS):
        b0 = hh * C_HEAD_PAD
        cq_ref[:, b0:b0 + LANES] = _norm_seg128(q[:, b0:b0 + LANES], g_cqn).astype(BF16)
        cq_ref[:, b0 + LANES:b0 + 2 * LANES] = _rope64(
            _norm_low64(q[:, b0 + LANES:b0 + 2 * LANES], g_cqr), ca, saa, sab).astype(BF16)
        ck_ref[:, b0:b0 + LANES] = _norm_seg128(kv[:, b0:b0 + LANES], g_ckn).astype(BF16)
        ck_ref[:, b0 + LANES:b0 + 2 * LANES] = krope
        cv_ref[:, hh * LANES:(hh + 1) * LANES] = kv[:, b0 + LANES:b0 + 2 * LANES].astype(BF16)


def _split_rows(n_a_tiles):
    first = lambda i: (jnp.minimum(i, n_a_tiles - 1), 0)
    second = lambda i: (jnp.maximum(i - n_a_tiles, 0), 0)
    return first, second


def _prep(layer, xa, xb, n_tiles, n_a_tiles, mod48, mod_row, g1, w_in, tables, rope_blk, gains, gcq, gckv,
          wuq_bf, wukv_bf):
    d = xa.shape[1]
    tm = TM_PREP
    m = n_tiles * tm
    row = lambda i: (i, 0)
    const = lambda i: (0, 0)
    first, second = _split_rows(n_a_tiles)
    tab_spec = pl.BlockSpec((tm, LANES), lambda i: (rope_blk(i), 0))
    widths = (512, 512, 512, 1024, 256, 256, C_HEADS * C_HEAD_PAD, C_HEADS * C_HEAD_PAD, 512)
    return pl.pallas_call(
        functools.partial(_prep_kernel, n_a_tiles=n_a_tiles, layer=layer),
        out_shape=[jax.ShapeDtypeStruct((m, w), BF16) for w in widths],
        grid=(n_tiles,),
        in_specs=[
            pl.BlockSpec((tm, d), first),
            pl.BlockSpec((tm, d), second),
            pl.BlockSpec((1, 1, d), lambda i: (mod_row(i) * 6 + 0, 0, 0)),
            pl.BlockSpec((1, 1, d), lambda i: (mod_row(i) * 6 + 1, 0, 0)),
            pl.BlockSpec((1, d), const),
            pl.BlockSpec(memory_space=pl.ANY),
            tab_spec, tab_spec, tab_spec, tab_spec, tab_spec,
            pl.BlockSpec((8, LANES), const),
            pl.BlockSpec((1, C_Q_RANK), const),
            pl.BlockSpec((1, C_KV_RANK), const),
            pl.BlockSpec((C_Q_RANK, C_HEADS * C_HEAD_PAD), const, pipeline_mode=pl.Buffered(1)),
            pl.BlockSpec((C_KV_RANK, C_HEADS * C_HEAD_PAD), const, pipeline_mode=pl.Buffered(1)),
        ],
        out_specs=[pl.BlockSpec((tm, w), row) for w in widths],
        scratch_shapes=[pltpu.VMEM((d, D_IN_PAD), BF16),
                        pltpu.VMEM((2, W_STAGE_ROWS, w_in.shape[2]), F32),
                        pltpu.SemaphoreType.DMA((2,))],
        compiler_params=_cparams(("arbitrary",)),
        name="prep",
    )(xa, xb, mod48, mod48, g1, w_in, *tables, gains, gcq, gckv, wuq_bf, wukv_bf)


def _dot_nt(a, b):
    return lax.dot_general(a, b, (((1,), (1,)), ((), ())), preferred_element_type=F32)


def _dot_tn(a, b):
    return lax.dot_general(a, b, (((0,), (0,)), ((), ())), preferred_element_type=F32)


def _softmax_pv_t(q, k_refs, v_refs):
    s = [_dot_nt(k[...], q) for k in k_refs]
    m = functools.reduce(jnp.maximum, [jnp.max(si, axis=0, keepdims=True) for si in s])
    e = [jnp.exp2(si - m) for si in s]
    l = functools.reduce(jnp.add, [jnp.sum(ei, axis=0, keepdims=True) for ei in e])
    o = functools.reduce(jnp.add, [_dot_tn(v[...], ei.astype(BF16)) for ei, v in zip(e, v_refs)])
    return o * (1.0 / l)


def _attn_a_kernel(*refs, n_src, lam_init):
    lv_ref, gsub_ref, q_ref = refs[0], refs[1], refs[2]
    k_refs = refs[3:3 + n_src]
    v_refs = refs[3 + n_src:3 + 2 * n_src]
    o_ref = refs[3 + 2 * n_src]
    tq = q_ref.shape[0]
    lv = lv_ref[...]
    lam = (jnp.exp(jnp.sum(lv[0:1] * lv[1:2], axis=-1, keepdims=True))
           - jnp.exp(jnp.sum(lv[2:3] * lv[3:4], axis=-1, keepdims=True)) + lam_init)
    q = q_ref[...]
    lo = lax.broadcasted_iota(jnp.int32, (1, LANES), 1) < 64
    zero = jnp.zeros_like(q)
    qq = jnp.concatenate([jnp.where(lo, q, zero), jnp.where(lo, zero, q)], axis=0)
    o2 = _softmax_pv_t(qq, k_refs, v_refs)
    o = o2[:, :tq] - lam * o2[:, tq:]
    ms = jnp.mean(o * o, axis=0, keepdims=True)
    o = (o * lax.rsqrt(ms + EPS)).T
    o_ref[...] = (o * gsub_ref[...] * (1.0 - lam_init)).astype(BF16)


def _attn_a(lv, gsub, aq, ak, av, q_row0, q_rows, srcs, n_batch, layer_idx):
    tq = min(TQ_A, q_rows)
    nq = q_rows // tq
    qb0 = q_row0 // tq
    lam_init = 0.8 - 0.6 * math.exp(-0.3 * layer_idx)
    in_specs = [
        pl.BlockSpec((4, A_DK), lambda b, h, i: (0, 0)),
        pl.BlockSpec((1, A_DV), lambda b, h, i: (0, 0)),
        pl.BlockSpec((tq, LANES), lambda b, h, i: (qb0 + b * nq + i, h)),
    ]
    kv_specs = [pl.BlockSpec((rows, LANES), lambda b, h, i, blk0=row0 // rows: (blk0 + b, h))
                for row0, rows in srcs]
    return pl.pallas_call(
        functools.partial(_attn_a_kernel, n_src=len(srcs), lam_init=lam_init),
        out_shape=jax.ShapeDtypeStruct((n_batch * q_rows, A_HEADS * A_DV), BF16),
        grid=(n_batch, A_HEADS, nq),
        in_specs=in_specs + kv_specs + kv_specs,
        out_specs=pl.BlockSpec((tq, LANES), lambda b, h, i: (b * nq + i, h)),
        compiler_params=_cparams(("arbitrary", "arbitrary", "arbitrary")),
        name="attn_a",
    )(lv, gsub, aq, *([ak] * len(srcs)), *([av] * len(srcs)))


def _attn_c_kernel(*refs, n_src):
    q_ref = refs[0]
    k_refs = refs[1:1 + n_src]
    v_refs = refs[1 + n_src:1 + 2 * n_src]
    o_ref = refs[1 + 2 * n_src]
    for hh in range(C_HEADS_PER_STEP):
        qk = slice(hh * C_HEAD_PAD, (hh + 1) * C_HEAD_PAD)
        dv = slice(hh * C_DV, (hh + 1) * C_DV)
        o = _softmax_pv_t(q_ref[:, qk], [k.at[:, qk] for k in k_refs], [v.at[:, dv] for v in v_refs])
        o_ref[:, dv] = o.T.astype(BF16)


def _attn_c(cq, ck, cv, q_row0, q_rows, srcs, n_batch):
    tq = min(TQ_C, q_rows)
    nq = q_rows // tq
    qb0 = q_row0 // tq
    hp = C_HEADS_PER_STEP
    in_specs = [pl.BlockSpec((tq, hp * C_HEAD_PAD), lambda b, h, i: (qb0 + b * nq + i, h))]
    k_specs = [pl.BlockSpec((rows, hp * C_HEAD_PAD), lambda b, h, i, blk0=row0 // rows: (blk0 + b, h))
               for row0, rows in srcs]
    v_specs = [pl.BlockSpec((rows, hp * C_DV), lambda b, h, i, blk0=row0 // rows: (blk0 + b, h))
               for row0, rows in srcs]
    return pl.pallas_call(
        functools.partial(_attn_c_kernel, n_src=len(srcs)),
        out_shape=jax.ShapeDtypeStruct((n_batch * q_rows, C_HEADS * C_DV), BF16),
        grid=(n_batch, C_HEADS // hp, nq),
        in_specs=in_specs + k_specs + v_specs,
        out_specs=pl.BlockSpec((tq, hp * C_DV), lambda b, h, i: (b * nq + i, h)),
        compiler_params=_cparams(("arbitrary", "arbitrary", "arbitrary")),
        name="attn_c",
    )(cq, *([ck] * len(srcs)), *([cv] * len(srcs)))


def _stack_heads(q):
    return jnp.concatenate([q[:, g * LANES:(g + 1) * LANES] for g in range(B_GROUP)], axis=0)


def _sink_row(sink_ref, kvh, cols):
    return jnp.concatenate(
        [jnp.full((1, cols), sink_ref[kvh * B_GROUP + g] * LOG2E, F32) for g in range(B_GROUP)], axis=1)


def _attn_b_kernel(sink_ref, q_ref, kp_ref, km_ref, kn_ref, vp_ref, vm_ref, vn_ref, kc_ref, vc_ref, o_ref):
    kvh = pl.program_id(1)
    qb = pl.program_id(2)
    nqb = pl.num_programs(2)
    n_blk = QB_B // BLOCK
    kband = jnp.concatenate([kp_ref[...], km_ref[...], kn_ref[...]], axis=0)
    vband = jnp.concatenate([vp_ref[...], vm_ref[...], vn_ref[...]], axis=0)
    kc, vc = kc_ref[...], vc_ref[...]
    sink = _sink_row(sink_ref, kvh, BLOCK)
    c = lax.broadcasted_iota(jnp.int32, (3 * BLOCK, B_GROUP * BLOCK), 0)
    r = lax.broadcasted_iota(jnp.int32, (3 * BLOCK, B_GROUP * BLOCK), 1) % BLOCK
    cr = c - r
    band_ok = (cr >= 0) & (cr <= BLOCK + WINDOW)
    for j in range(n_blk):
        q4 = _stack_heads(q_ref[j * BLOCK:(j + 1) * BLOCK, :])
        s_loc = _dot_nt(kband[j * BLOCK:(j + 3) * BLOCK], q4)
        valid = band_ok
        if j == 0:
            valid = valid & (c >= jnp.where(qb > 0, 0, BLOCK))
        if j == n_blk - 1:
            valid = valid & (c < jnp.where(qb < nqb - 1, 3 * BLOCK, 2 * BLOCK))
        s_loc = jnp.where(valid, s_loc, NEG_INF)
        s_ctx = _dot_nt(kc, q4)
        m = jnp.maximum(jnp.maximum(jnp.max(s_loc, axis=0, keepdims=True),
                                    jnp.max(s_ctx, axis=0, keepdims=True)), sink)
        e_loc, e_ctx = jnp.exp2(s_loc - m), jnp.exp2(s_ctx - m)
        l = (jnp.sum(e_loc, axis=0, keepdims=True) + jnp.sum(e_ctx, axis=0, keepdims=True)
             + jnp.exp2(sink - m))
        o = (_dot_tn(vband[j * BLOCK:(j + 3) * BLOCK], e_loc.astype(BF16))
             + _dot_tn(vc, e_ctx.astype(BF16))) * (1.0 / l)
        o = o.T
        for g in range(B_GROUP):
            o_ref[j * BLOCK:(j + 1) * BLOCK, g * LANES:(g + 1) * LANES] = (
                o[g * BLOCK:(g + 1) * BLOCK].astype(BF16))


def _attn_b(sink, bq, bk, bv, n_batch, n_lat, n_ctx):
    nqb = n_lat // QB_B
    per = QB_B // BLOCK
    blocks_per_batch = n_lat // BLOCK
    ctx_blk0 = n_batch * n_lat // n_ctx
    gw = B_GROUP * B_DH
    prev = lambda b, h, i: (b * blocks_per_batch + jnp.maximum(i * per - 1, 0), h)
    main = lambda b, h, i: (b * nqb + i, h)
    nxt = lambda b, h, i: (b * blocks_per_batch + jnp.minimum(i * per + per, blocks_per_batch - 1), h)
    ctx = lambda b, h, i: (ctx_blk0 + b, h)
    return pl.pallas_call(
        _attn_b_kernel,
        out_shape=jax.ShapeDtypeStruct((n_batch * n_lat, B_HEADS * B_DH), BF16),
        grid=(n_batch, B_KV_HEADS, nqb),
        in_specs=[
            pl.BlockSpec(memory_space=pltpu.SMEM),
            pl.BlockSpec((QB_B, gw), main),
            pl.BlockSpec((BLOCK, B_DH), prev), pl.BlockSpec((QB_B, B_DH), main), pl.BlockSpec((BLOCK, B_DH), nxt),
            pl.BlockSpec((BLOCK, B_DH), prev), pl.BlockSpec((QB_B, B_DH), main), pl.BlockSpec((BLOCK, B_DH), nxt),
            pl.BlockSpec((n_ctx, B_DH), ctx), pl.BlockSpec((n_ctx, B_DH), ctx),
        ],
        out_specs=pl.BlockSpec((QB_B, gw), main),
        compiler_params=_cparams(("arbitrary", "arbitrary", "arbitrary")),
        name="attn_b",
    )(sink, bq, bk, bk, bk, bv, bv, bv, bk, bv)


def _attn_b_ctx_kernel(sink_ref, q_ref, k_ref, v_ref, o_ref):
    kvh = pl.program_id(1)
    rows = q_ref.shape[0]
    q4 = _stack_heads(q_ref[...])
    sink = _sink_row(sink_ref, kvh, rows)
    s = _dot_nt(k_ref[...], q4)
    m = jnp.maximum(jnp.max(s, axis=0, keepdims=True), sink)
    e = jnp.exp2(s - m)
    l = jnp.sum(e, axis=0, keepdims=True) + jnp.exp2(sink - m)
    o = (_dot_tn(v_ref[...], e.astype(BF16)) * (1.0 / l)).T
    for g in range(B_GROUP):
        o_ref[:, g * LANES:(g + 1) * LANES] = o[g * rows:(g + 1) * rows].astype(BF16)


def _attn_b_ctx(sink, bq, bk, bv, n_batch, n_lat, n_ctx):
    gw = B_GROUP * B_DH
    ctx_blk0 = n_batch * n_lat // n_ctx
    return pl.pallas_call(
        _attn_b_ctx_kernel,
        out_shape=jax.ShapeDtypeStruct((n_batch * n_ctx, B_HEADS * B_DH), BF16),
        grid=(n_batch, B_KV_HEADS),
        in_specs=[
            pl.BlockSpec(memory_space=pltpu.SMEM),
            pl.BlockSpec((n_ctx, gw), lambda b, h: (ctx_blk0 + b, h)),
            pl.BlockSpec((n_ctx, B_DH), lambda b, h: (ctx_blk0 + b, h)),
            pl.BlockSpec((n_ctx, B_DH), lambda b, h: (ctx_blk0 + b, h)),
        ],
        out_specs=pl.BlockSpec((n_ctx, gw), lambda b, h: (b, h)),
        compiler_params=_cparams(("arbitrary", "arbitrary")),
        name="attn_b_ctx",
    )(sink, bq, bk, bv)


def _route(h2, rw_ref, rb_ref):
    tm = h2.shape[0]
    h_hi = h2.astype(BF16)
    h_lo = (h2 - h_hi.astype(F32)).astype(BF16)
    p = jnp.dot(jnp.concatenate([h_hi, h_lo], axis=0), rw_ref[...], preferred_element_type=F32)
    p = p[:tm] + p[tm:]
    logits = p[:, :N_EXPERTS] + p[:, N_EXPERTS:]
    scores = 1.0 / (1.0 + jnp.exp(-logits))
    sel = scores + rb_ref[...]
    lane_i = lax.broadcasted_iota(jnp.int32, sel.shape, 1)
    lane = lane_i.astype(F32)
    big = float(N_EXPERTS)

    def top2(mask):
        v = jnp.where(mask, sel, -jnp.inf)
        m1 = jnp.max(v, axis=-1, keepdims=True)
        i1 = jnp.min(jnp.where(v == m1, lane, big), axis=-1, keepdims=True)
        v2 = jnp.where(lane == i1, -jnp.inf, v)
        m2 = jnp.max(v2, axis=-1, keepdims=True)
        i2 = jnp.min(jnp.where(v2 == m2, lane, big), axis=-1, keepdims=True)
        return m1, i1, m2, i2

    best = None
    for g in range(N_GROUPS):
        m1, i1, m2, i2 = top2((lane_i >= g * EXPERTS_PER_GROUP) & (lane_i < (g + 1) * EXPERTS_PER_GROUP))
        gs = m1 + m2
        if best is None:
            best = (gs, i1, i2)
        else:
            take = gs > best[0]
            best = (jnp.where(take, gs, best[0]), jnp.where(take, i1, best[1]), jnp.where(take, i2, best[2]))
    _, e1, e2 = best
    w1 = jnp.sum(jnp.where(lane == e1, scores, 0.0), axis=-1, keepdims=True)
    w2 = jnp.sum(jnp.where(lane == e2, scores, 0.0), axis=-1, keepdims=True)
    tot = w1 + w2
    return e1, e2, w1 / tot, w2 / tot


def _outproj_kernel(oa1_ref, ob1_ref, oc1_ref, x1in_ref, oa2_ref, ob2_ref, oc2_ref, x2in_ref,
                    g1_ref, sh2_ref, sc2_ref, n2_ref, w_hbm, rw_ref, rb_ref,
                    x1_ref, h2_ref, idx_ref, wt_ref, cnt_ref, w_ref, w_stage, w_sem, cnt_acc,
                    *, n_a_tiles, layer):
    @pl.when(pl.program_id(0) == 0)
    def _():
        _load_weight_bf16(w_hbm, layer, w_stage, w_sem, w_ref)

    first = pl.program_id(0) < n_a_tiles
    oa = jnp.where(first, oa1_ref[...], oa2_ref[...])
    ob = jnp.where(first, ob1_ref[...], ob2_ref[...])
    oc = jnp.where(first, oc1_ref[...], oc2_ref[...])
    x = jnp.where(first, x1in_ref[...], x2in_ref[...])
    y = jnp.dot(jnp.concatenate([oa, ob, oc], axis=1), w_ref[...], preferred_element_type=F32)
    x1 = x + g1_ref[0] * y
    x1_ref[...] = x1
    ms = jnp.mean(x1 * x1, axis=-1, keepdims=True)
    h2 = x1 * lax.rsqrt(ms + EPS) * n2_ref[...]
    h2 = h2 * (1.0 + sc2_ref[0]) + sh2_ref[0]
    tm = h2.shape[0]
    for cc in range(ROW_CHUNKS):
        h2_ref[pl.ds(cc, tm, stride=ROW_CHUNKS), :] = h2[:, cc * LANES:(cc + 1) * LANES]
    e1, e2, w1, w2 = _route(h2, rw_ref, rb_ref)
    lane = lax.broadcasted_iota(jnp.int32, idx_ref.shape, 1)
    wt_ref[...] = jnp.where(lane == 0, w1, jnp.where(lane == 1, w2, 0.0))

    @pl.when(pl.program_id(0) == 0)
    def _():
        cnt_acc[...] = jnp.zeros_like(cnt_acc)

    lane_f = lane.astype(F32)
    sel1, sel2 = lane_f == e1, lane_f == e2
    picks = jnp.where(sel1, 1.0, 0.0) + jnp.where(sel2, 1.0, 0.0)
    t_row = lax.broadcasted_iota(jnp.int32, (tm, tm), 0)
    t_col = lax.broadcasted_iota(jnp.int32, (tm, tm), 1)
    earlier = jnp.where(t_col < t_row, 1.0, 0.0).astype(BF16)
    before = jnp.dot(earlier, picks.astype(BF16), preferred_element_type=F32) + cnt_acc[0:1, :]
    r1 = jnp.sum(jnp.where(sel1, before, 0.0), axis=-1, keepdims=True)
    r2 = jnp.sum(jnp.where(sel2, before, 0.0), axis=-1, keepdims=True)
    idx_ref[...] = jnp.where(lane == 0, e1, jnp.where(lane == 1, e2, jnp.where(
        lane == 2, r1, jnp.where(lane == 3, r2, 0.0)))).astype(jnp.int32)
    cnt_acc[...] = cnt_acc[...] + jnp.sum(picks, axis=0, keepdims=True)
    cnt_ref[...] = cnt_acc[...]


def _outproj(layer, set_a, set_b, n_tiles, n_a_tiles, mod48, mod_row, n2, w_out, router_w, router_b):
    d = set_a[3].shape[1]
    tm = TM_PREP
    m = n_tiles * tm
    row = lambda i: (i, 0)
    const = lambda i: (0, 0)
    first, second = _split_rows(n_a_tiles)
    mod_spec = lambda j: pl.BlockSpec((1, 1, d), lambda i: (mod_row(i) * 6 + j, 0, 0))
    in_specs = (
        [pl.BlockSpec((tm, a.shape[1]), first) for a in set_a]
        + [pl.BlockSpec((tm, a.shape[1]), second) for a in set_b]
        + [mod_spec(2), mod_spec(3), mod_spec(4),
           pl.BlockSpec((1, d), const),
           pl.BlockSpec(memory_space=pl.ANY),
           pl.BlockSpec((d, 2 * N_EXPERTS), const),
           pl.BlockSpec((1, N_EXPERTS), const)])
    return pl.pallas_call(
        functools.partial(_outproj_kernel, n_a_tiles=n_a_tiles, layer=layer),
        out_shape=[jax.ShapeDtypeStruct((m, d), F32), jax.ShapeDtypeStruct((m * ROW_CHUNKS, LANES), F32),
                   jax.ShapeDtypeStruct((m, LANES), jnp.int32), jax.ShapeDtypeStruct((m, LANES), F32),
                   jax.ShapeDtypeStruct((8, LANES), F32)],
        grid=(n_tiles,),
        in_specs=in_specs,
        out_specs=[pl.BlockSpec((tm, d), row), pl.BlockSpec((tm * ROW_CHUNKS, LANES), row),
                   pl.BlockSpec((tm, LANES), row), pl.BlockSpec((tm, LANES), row),
                   pl.BlockSpec((8, LANES), const)],
        scratch_shapes=[pltpu.VMEM((d, d), BF16),
                        pltpu.VMEM((2, W_STAGE_ROWS, d), F32),
                        pltpu.SemaphoreType.DMA((2,)),
                        pltpu.VMEM((8, LANES), F32)],
        compiler_params=_cparams(("arbitrary",)),
        name="outproj",
    )(*set_a, *set_b, mod48, mod48, mod48, n2, w_out, router_w, router_b)


def _experts_kernel(te_ref, nxt_ref, ws_ref, nv_ref, src_ref, h2_hbm, wg_hbm, wu_hbm, wd_hbm, o_ref,
                    xbuf, xsem, wg_f, wu_f, wd_f, wsem, wg_bf, wu_bf, wd_bf, *, layer):
    i = pl.program_id(0)
    n_valid = nv_ref[0]
    slot = i % X_SLOTS

    def issue_rows(tile):
        s = tile % X_SLOTS
        base = tile * TM_E
        for r in range(TM_E):
            row0 = pl.multiple_of(src_ref[base + r] * ROW_CHUNKS, ROW_CHUNKS)
            pltpu.make_async_copy(h2_hbm.at[pl.ds(row0, ROW_CHUNKS), :],
                                  xbuf.at[s, pl.ds(r * X_PITCH, ROW_CHUNKS), :], xsem.at[s]).start()

    def wait_rows(s):
        pltpu.make_async_copy(h2_hbm.at[pl.ds(0, TM_E * ROW_CHUNKS), :],
                              xbuf.at[s, pl.ds(0, TM_E * ROW_CHUNKS), :], xsem.at[s]).wait()

    def weight_copies(e, s):
        return (pltpu.make_async_copy(wg_hbm.at[layer, e], wg_f.at[s], wsem.at[s]),
                pltpu.make_async_copy(wu_hbm.at[layer, e], wu_f.at[s], wsem.at[s]),
                pltpu.make_async_copy(wd_hbm.at[layer, e], wd_f.at[s], wsem.at[s]))

    @pl.when(i == 0)
    def _():
        for cp in weight_copies(te_ref[0], ws_ref[0]):
            cp.start()
        for t in range(X_SLOTS - 1):
            issue_rows(t)

    first_of_expert = jnp.logical_or(i == 0, te_ref[i] != te_ref[jnp.maximum(i - 1, 0)])

    @pl.when(jnp.logical_and(first_of_expert, i < n_valid))
    def _():
        ws = ws_ref[i]
        for cp in weight_copies(te_ref[i], ws):
            cp.wait()

        @pl.when(nxt_ref[i] >= 0)
        def _():
            for cp in weight_copies(nxt_ref[i], 1 - ws):
                cp.start(priority=WEIGHT_DMA_PRIORITY)

        wg_bf[...] = wg_f[ws].astype(BF16)
        wu_bf[...] = wu_f[ws].astype(BF16)
        wd_bf[...] = wd_f[ws].astype(BF16)

    @pl.when(i < n_valid)
    def _():
        wait_rows(slot)
        x = jnp.concatenate([xbuf[slot, pl.ds(cc, TM_E, stride=X_PITCH), :] for cc in range(ROW_CHUNKS)],
                            axis=1).astype(BF16)
        issue_rows(i + X_SLOTS - 1)
        g = jnp.dot(x, wg_bf[...], preferred_element_type=F32)
        u = jnp.dot(x, wu_bf[...], preferred_element_type=F32)
        a = (_silu(g) * u).astype(BF16)
        y = jnp.dot(a, wd_bf[...], preferred_element_type=F32)
        for cc in range(ROW_CHUNKS):
            o_ref[pl.ds(cc, TM_E, stride=ROW_CHUNKS), :] = y[:, cc * LANES:(cc + 1) * LANES]

    @pl.when(i == n_valid - 1)
    def _():
        for t in range(1, X_SLOTS):
            wait_rows((i + t) % X_SLOTS)

    @pl.when(i >= n_valid)
    def _():
        o_ref[...] = jnp.zeros_like(o_ref)


def _experts(layer, tile_expert, next_expert, w_slot, n_valid, src, h2_all, w_gate, w_up, w_down):
    n_rows = src.shape[0] - (X_SLOTS - 1) * TM_E
    nt = n_rows // TM_E
    d, de = w_gate.shape[2], w_gate.shape[3]
    grid_spec = pltpu.PrefetchScalarGridSpec(
        num_scalar_prefetch=5,
        grid=(nt,),
        in_specs=[pl.BlockSpec(memory_space=pl.ANY)] * 4,
        out_specs=pl.BlockSpec((TM_E * ROW_CHUNKS, LANES), lambda i, *_: (i, 0)),
        scratch_shapes=[
            pltpu.VMEM((X_SLOTS, TM_E * X_PITCH, LANES), F32),
            pltpu.SemaphoreType.DMA((X_SLOTS,)),
            pltpu.VMEM((2, d, de), F32), pltpu.VMEM((2, d, de), F32), pltpu.VMEM((2, de, d), F32),
            pltpu.SemaphoreType.DMA((2,)),
            pltpu.VMEM((d, de), BF16), pltpu.VMEM((d, de), BF16), pltpu.VMEM((de, d), BF16),
        ],
    )
    return pl.pallas_call(
        functools.partial(_experts_kernel, layer=layer),
        out_shape=jax.ShapeDtypeStruct((n_rows * ROW_CHUNKS, LANES), F32),
        grid_spec=grid_spec,
        compiler_params=_cparams(("arbitrary",)),
        name="experts",
    )(tile_expert, next_expert, w_slot, n_valid, src, h2_all, w_gate, w_up, w_down)


def _combine_kernel(pos_ref, o_hbm, x1_ref, wt_ref, g2_ref, x2_ref, buf, sem):
    i = pl.program_id(0)
    nt = pl.num_programs(0)
    slot = i % C_SLOTS

    def issue(tile):
        s = tile % C_SLOTS
        base = tile * (TM_C * TOP_K)
        for r in range(TM_C):
            for k in range(TOP_K):
                row0 = pl.multiple_of(pos_ref[base + TOP_K * r + k] * ROW_CHUNKS, ROW_CHUNKS)
                pltpu.make_async_copy(o_hbm.at[pl.ds(row0, ROW_CHUNKS), :],
                                      buf.at[s, k, pl.ds(r * X_PITCH, ROW_CHUNKS), :], sem.at[s]
                                      ).start(priority=k % DMA_QUEUES)

    def wait(s):
        for k in range(TOP_K):
            pltpu.make_async_copy(o_hbm.at[pl.ds(0, TM_C * ROW_CHUNKS), :],
                                  buf.at[s, k, pl.ds(0, TM_C * ROW_CHUNKS), :], sem.at[s]).wait()

    @pl.when(i == 0)
    def _():
        for t in range(C_SLOTS - 1):
            issue(t)

    def expert_rows(k):
        return jnp.concatenate([buf[slot, k, pl.ds(cc, TM_C, stride=X_PITCH), :] for cc in range(ROW_CHUNKS)],
                               axis=1)

    wait(slot)
    wt = wt_ref[...]
    x2_ref[...] = x1_ref[...] + g2_ref[0] * (wt[:, 0:1] * expert_rows(0) + wt[:, 1:2] * expert_rows(1))
    issue(i + C_SLOTS - 1)

    @pl.when(i == nt - 1)
    def _():
        for t in range(1, C_SLOTS):
            wait((i + t) % C_SLOTS)


def _combine(pos, o_sorted, x1, wts, mod48, mod_row):
    m, d = x1.shape
    pos = jnp.concatenate([pos, jnp.zeros(((C_SLOTS - 1) * TM_C * TOP_K,), jnp.int32)])
    grid_spec = pltpu.PrefetchScalarGridSpec(
        num_scalar_prefetch=1,
        grid=(m // TM_C,),
        in_specs=[
            pl.BlockSpec(memory_space=pl.ANY),
            pl.BlockSpec((TM_C, d), lambda i, p: (i, 0)),
            pl.BlockSpec((TM_C, LANES), lambda i, p: (i, 0)),
            pl.BlockSpec((1, 1, d), lambda i, p: (mod_row(i) * 6 + 5, 0, 0)),
        ],
        out_specs=pl.BlockSpec((TM_C, d), lambda i, p: (i, 0)),
        scratch_shapes=[pltpu.VMEM((C_SLOTS, TOP_K, TM_C * X_PITCH, LANES), F32),
                        pltpu.SemaphoreType.DMA((C_SLOTS,))],
    )
    return pl.pallas_call(
        _combine_kernel,
        out_shape=jax.ShapeDtypeStruct((m, d), F32),
        grid_spec=grid_spec,
        compiler_params=_cparams(("arbitrary",)),
        name="combine",
    )(pos, o_sorted, x1, wts, mod48)


def _rope_tables(n_lat, tm):
    f32 = np.float32
    t = np.arange(n_lat)
    r = (t // GRID_W).astype(f32)
    col = (t % GRID_W).astype(f32)

    def cos_sin(dim):
        nf = dim // 4
        inv = (f32(ROPE_THETA) ** (-np.arange(nf, dtype=f32) / f32(nf))).astype(f32)
        ang = np.concatenate([r[:, None] * inv, col[:, None] * inv], axis=-1).astype(f32)
        return np.cos(ang).astype(f32), np.sin(ang).astype(f32)

    c64, s64 = cos_sin(A_DK)
    c128, s128 = cos_sin(B_DH)
    z32 = np.zeros_like(s64)
    tabs = [
        np.concatenate([c64, c64, c64, c64], axis=-1),
        np.concatenate([-s64, z32, -s64, z32], axis=-1),
        np.concatenate([z32, s64, z32, s64], axis=-1),
        np.concatenate([c128, c128], axis=-1),
        np.concatenate([-s128, s128], axis=-1),
    ]
    ident = [np.ones((tm, LANES), f32), np.zeros((tm, LANES), f32), np.zeros((tm, LANES), f32),
             np.ones((tm, LANES), f32), np.zeros((tm, LANES), f32)]
    return [jnp.asarray(np.concatenate([a, b], axis=0)) for a, b in zip(tabs, ident)]


def _layer_params(l, a_qn, a_kn, b_qn, b_kn, c_qa_norm, c_kva_norm, c_wuq, c_wukv, c_qn, c_kn):
    z64 = jnp.zeros((C_ROPE,), F32)
    gains = jnp.stack([
        jnp.tile(a_qn[l], 2) * (A_DK ** -0.5 * LOG2E),
        jnp.tile(a_kn[l], 2),
        b_qn[l] * (B_DH ** -0.5 * LOG2E),
        b_kn[l],
        c_qn[l][:C_NOPE] * (C_DQK ** -0.5 * LOG2E),
        jnp.concatenate([c_qn[l][C_NOPE:] * (C_DQK ** -0.5 * LOG2E), z64]),
        c_kn[l][:C_NOPE],
        jnp.concatenate([c_kn[l][C_NOPE:], z64]),
    ])
    wq = c_wuq[l].reshape(C_Q_RANK, C_HEADS, C_DQK)
    wq = jnp.pad(wq, ((0, 0), (0, 0), (0, C_HEAD_PAD - C_DQK))).reshape(C_Q_RANK, C_HEADS * C_HEAD_PAD)
    return dict(gains=gains, gcq=c_qa_norm[l][None], gckv=c_kva_norm[l][None],
                wuq=wq.astype(BF16), wukv=c_wukv[l].astype(BF16))


def _sorted_rows(idx, rank, counts):
    t = idx.shape[0]
    n_pairs = t * TOP_K
    n_rows = ((n_pairs + N_EXPERTS * (TM_E - 1)) // TM_E) * TM_E
    nt = n_rows // TM_E
    flat_e = idx.reshape(-1)
    rank = rank.reshape(-1)
    padded = ((counts + TM_E - 1) // TM_E) * TM_E
    ends = jnp.cumsum(padded)
    pos = (ends - padded)[flat_e] + rank
    src = jnp.zeros((n_rows + (X_SLOTS - 1) * TM_E,), jnp.int32).at[pos].set(
        jnp.arange(n_pairs, dtype=jnp.int32) // TOP_K, unique_indices=True)
    tile_start = jnp.arange(nt, dtype=jnp.int32) * TM_E
    tile_expert = jnp.sum((ends[None, :] <= tile_start[:, None]).astype(jnp.int32), axis=1)
    last_used = jnp.sum((ends <= ends[-1] - 1).astype(jnp.int32))
    tile_expert = jnp.minimum(tile_expert, last_used)
    n_valid = (ends[-1:] // TM_E).astype(jnp.int32)
    used = counts > 0
    e_ids = jnp.arange(N_EXPERTS, dtype=jnp.int32)
    later = jnp.where(used[None, :] & (e_ids[None, :] > e_ids[:, None]), e_ids[None, :], N_EXPERTS)
    next_used = jnp.min(later, axis=1)
    next_used = jnp.where(next_used == N_EXPERTS, -1, next_used).astype(jnp.int32)
    parity = ((jnp.cumsum(used.astype(jnp.int32)) - 1) % 2).astype(jnp.int32)
    return (pos.astype(jnp.int32), src, tile_expert.astype(jnp.int32), next_used[tile_expert],
            parity[tile_expert], n_valid)


def kernel(x, c, ctx, c_ctx, ada_w, ada_b, norm1_g, norm2_g, w_in, w_out, a_qn, a_kn, a_lambda, a_subln,
           b_qn, b_kn, b_sink, c_qa_norm, c_kva_norm, c_wuq, c_wukv, c_qn, c_kn,
           router_w, router_bias, moe_w_gate, moe_w_up, moe_w_down):
    bsz, n_lat, d = x.shape
    n_ctx = ctx.shape[1]
    depth = ada_w.shape[0]
    t_lat, t_ctx = bsz * n_lat, bsz * n_ctx
    tm = TM_PREP
    lat_tiles = n_lat // tm
    n_lat_tiles, n_ctx_tiles = t_lat // tm, t_ctx // tm
    n_all_tiles = n_lat_tiles + n_ctx_tiles

    cond8 = jnp.concatenate([c, c_ctx[None], jnp.zeros((8 - bsz - 1, d), F32)], axis=0)
    mod = _ada_modulation(cond8, ada_w, ada_b)
    tables = _rope_tables(n_lat, tm)
    rb = router_bias[None]
    rw_hi = router_w.astype(BF16)
    rw_lo = (router_w - rw_hi.astype(F32)).astype(BF16)
    rw2 = jnp.concatenate([rw_hi, rw_lo], axis=1)

    mod_row = lambda i: jnp.minimum(i // lat_tiles, bsz)
    mod_row_c = lambda i: jnp.minimum(i // (n_lat // TM_C), bsz)
    rope_blk = lambda i: jnp.where(i < n_lat_tiles, i % lat_tiles, lat_tiles)

    xa, xb, n_a_tiles = x.reshape(t_lat, d), ctx.reshape(t_ctx, d), n_lat_tiles
    lat_src, ctx_src = (0, n_lat), (t_lat, n_ctx)
    for l in range(depth):
        last = l == depth - 1
        p = _layer_params(l, a_qn, a_kn, b_qn, b_kn, c_qa_norm, c_kva_norm, c_wuq, c_wukv, c_qn, c_kn)
        mod48 = mod[l].reshape(8 * 6, 1, d)
        aq, ak, av, bq, bk, bv, cq, ck, cv = _prep(
            l, xa, xb, n_all_tiles, n_a_tiles, mod48, mod_row, norm1_g[l][None], w_in, tables, rope_blk,
            p["gains"], p["gcq"], p["gckv"], p["wuq"], p["wukv"])

        lv, gsub, sink = a_lambda[l], a_subln[l][None], b_sink[l]
        o_a = _attn_a(lv, gsub, aq, ak, av, 0, n_lat, [lat_src, ctx_src], bsz, l)
        o_b = _attn_b(sink, bq, bk, bv, bsz, n_lat, n_ctx)
        o_c = _attn_c(cq, ck, cv, 0, n_lat, [lat_src, ctx_src], bsz)
        set_a = (o_a, o_b, o_c, xa)
        if not last:
            oc_a = _attn_a(lv, gsub, aq, ak, av, t_lat, n_ctx, [ctx_src], bsz, l)
            oc_b = _attn_b_ctx(sink, bq, bk, bv, bsz, n_lat, n_ctx)
            oc_c = _attn_c(cq, ck, cv, t_lat, n_ctx, [ctx_src], bsz)
            set_b, n_tok_tiles = (oc_a, oc_b, oc_c, xb), n_all_tiles
        else:
            set_b, n_tok_tiles = set_a, n_lat_tiles
        x1, h2, idx, wts, cnt = _outproj(l, set_a, set_b, n_tok_tiles, min(n_a_tiles, n_tok_tiles), mod48,
                                         mod_row, norm2_g[l][None], w_out, rw2, rb)

        pos, src, tile_expert, next_expert, w_slot, n_valid = _sorted_rows(
            idx[:, :TOP_K], idx[:, TOP_K:2 * TOP_K], cnt[0, :N_EXPERTS].astype(jnp.int32))
        o_sorted = _experts(l, tile_expert, next_expert, w_slot, n_valid, src, h2,
                            moe_w_gate, moe_w_up, moe_w_down)
        xa = _combine(pos, o_sorted, x1, wts, mod48, mod_row_c)
        xb, n_a_tiles = xa, n_all_tiles
    return xa.reshape(bsz, n_lat, d)
```

```python
import functools
import math

import jax
import jax.numpy as jnp
import numpy as np
from jax import lax
from jax.experimental import pallas as pl
from jax.experimental.pallas import tpu as pltpu

F32 = jnp.float32
BF16 = jnp.bfloat16

D_MODEL = 2048
GRID_W = 64
BLOCK = 128
WINDOW = 128
ROPE_THETA = 10000.0
EPS = 1e-6
NEG_INF = -1e30
LOG2E = math.log2(math.e)
A_HEADS, A_DK = 4, 64
A_DV = 2 * A_DK
B_HEADS, B_KV_HEADS, B_DH = 8, 2, 128
B_GROUP = B_HEADS // B_KV_HEADS
C_HEADS, C_Q_RANK, C_KV_RANK, C_NOPE, C_ROPE, C_DV = 4, 512, 256, 128, 64, 128
C_DQK = C_NOPE + C_ROPE
SPLIT_SIZES = (A_HEADS * 2 * A_DK, A_HEADS * 2 * A_DK, A_HEADS * A_DV,
               B_HEADS * B_DH, B_KV_HEADS * B_DH, B_KV_HEADS * B_DH,
               C_Q_RANK, C_KV_RANK, C_ROPE)
D_IN = sum(SPLIT_SIZES)
N_EXPERTS, N_GROUPS, TOP_K = 32, 4, 2
EXPERTS_PER_GROUP = N_EXPERTS // N_GROUPS
D_EXPERT = 512

LANES = 128
V7X_VMEM_LIMIT = 56 * 1024 * 1024

D_IN_PAD = ((D_IN + LANES - 1) // LANES) * LANES
C_HEAD_PAD = 2 * LANES
TM_PREP = 256
TQ_A = 1024
TQ_C = 2048
C_HEADS_PER_STEP = 1
QB_B = 512
TM_E = 128
X_SLOTS = 3
ROW_CHUNKS = D_MODEL // LANES
X_PITCH = ROW_CHUNKS + 8
DMA_QUEUES = 2
WEIGHT_DMA_PRIORITY = 1
TM_C = 128
ADA_TN = 1024
W_STAGE_ROWS = 128

_OFF = [0]
for _s in SPLIT_SIZES:
    _OFF.append(_OFF[-1] + _s)
O_AQ, O_AK, O_AV, O_BQ, O_BK, O_BV, O_CQ, O_CKV, O_CKR, _ = _OFF


def _cparams(sem):
    return pltpu.CompilerParams(dimension_semantics=sem, vmem_limit_bytes=V7X_VMEM_LIMIT)


def _silu(v):
    return v * (1.0 / (1.0 + jnp.exp(-v)))


def _ada_kernel(cond_ref, w_ref, b_ref, o_ref):
    s = _silu(cond_ref[...]).astype(BF16)
    o_ref[0] = jnp.dot(s, w_ref[0].astype(BF16), preferred_element_type=F32) + b_ref[0]


def _ada_modulation(cond8, ada_w, ada_b):
    depth, d, n = ada_w.shape
    return pl.pallas_call(
        _ada_kernel,
        out_shape=jax.ShapeDtypeStruct((depth, 8, n), F32),
        grid=(depth, n // ADA_TN),
        in_specs=[
            pl.BlockSpec((8, d), lambda l, j: (0, 0)),
            pl.BlockSpec((1, d, ADA_TN), lambda l, j: (l, 0, j)),
            pl.BlockSpec((1, 1, ADA_TN), lambda l, j: (l, 0, j)),
        ],
        out_specs=pl.BlockSpec((1, 8, ADA_TN), lambda l, j: (l, 0, j)),
        compiler_params=_cparams(("arbitrary", "arbitrary")),
        name="ada_modulation",
    )(cond8, ada_w, ada_b.reshape(depth, 1, n))


def _rope64(v, c, sa, sb):
    return v * c + pltpu.roll(v, 96, 1) * sa + pltpu.roll(v, 32, 1) * sb


def _rope128(v, c, s):
    return v * c + pltpu.roll(v, 64, 1) * s


def _norm_seg128(v, g):
    ms = jnp.sum(v * v, axis=-1, keepdims=True) * (1.0 / 128)
    return v * lax.rsqrt(ms + EPS) * g


def _norm_seg64x2(v, g, lo):
    sq = v * v
    s_lo = jnp.sum(jnp.where(lo, sq, 0.0), axis=-1, keepdims=True)
    s_hi = jnp.sum(jnp.where(lo, 0.0, sq), axis=-1, keepdims=True)
    ms = jnp.where(lo, s_lo, s_hi) * (1.0 / 64)
    return v * lax.rsqrt(ms + EPS) * g


def _norm_low64(v, g):
    ms = jnp.sum(v * v, axis=-1, keepdims=True) * (1.0 / 64)
    return v * lax.rsqrt(ms + EPS) * g


def _load_weight_bf16(w_hbm, layer, stage, sem, w_bf):
    k, n = w_hbm.shape[1], w_hbm.shape[2]
    ch = stage.shape[1]

    def chunk_copy(c):
        return pltpu.make_async_copy(w_hbm.at[layer, pl.ds(c * ch, ch), :], stage.at[c % 2], sem.at[c % 2])

    n_pad = w_bf.shape[1]
    if n_pad > n:
        edge = (n // LANES) * LANES
        w_bf[:, edge:n_pad] = jnp.zeros((k, n_pad - edge), BF16)
    chunk_copy(0).start()
    for c in range(k // ch):
        if c + 1 < k // ch:
            chunk_copy(c + 1).start()
        chunk_copy(c).wait()
        w_bf[c * ch:(c + 1) * ch, 0:n] = stage[c % 2].astype(BF16)


def _prep_kernel(xa_ref, xb_ref, sh_ref, sc_ref, g1_ref, w_hbm, ca_ref, saa_ref, sab_ref, cb_ref, sb_ref,
                 gains_ref, gcq_ref, gckv_ref, wuq_ref, wukv_ref,
                 aq_ref, ak_ref, av_ref, bq_ref, bk_ref, bv_ref, cq_ref, ck_ref, cv_ref,
                 w_bf, w_stage, w_sem, *, n_a_tiles, layer):
    @pl.when(pl.program_id(0) == 0)
    def _():
        _load_weight_bf16(w_hbm, layer, w_stage, w_sem, w_bf)

    x = jnp.where(pl.program_id(0) < n_a_tiles, xa_ref[...], xb_ref[...])
    ms = jnp.mean(x * x, axis=-1, keepdims=True)
    h = x * lax.rsqrt(ms + EPS) * g1_ref[...]
    h = h * (1.0 + sc_ref[0]) + sh_ref[0]
    z = jnp.dot(h.astype(BF16), w_bf[...], preferred_element_type=F32)

    lane = lax.broadcasted_iota(jnp.int32, (1, LANES), 1)
    lo = lane < 64
    ca, saa, sab = ca_ref[...], saa_ref[...], sab_ref[...]
    cb, sb = cb_ref[...], sb_ref[...]
    g_aq, g_ak, g_bq, g_bk = gains_ref[0:1], gains_ref[1:2], gains_ref[2:3], gains_ref[3:4]
    g_cqn, g_cqr, g_ckn, g_ckr = gains_ref[4:5], gains_ref[5:6], gains_ref[6:7], gains_ref[7:8]

    def blk(off, j):
        return z[:, off + j * LANES: off + (j + 1) * LANES]

    for j in range(A_HEADS):
        sl = slice(j * LANES, (j + 1) * LANES)
        aq_ref[:, sl] = _rope64(_norm_seg64x2(blk(O_AQ, j), g_aq, lo), ca, saa, sab).astype(BF16)
        ak_ref[:, sl] = _rope64(_norm_seg64x2(blk(O_AK, j), g_ak, lo), ca, saa, sab).astype(BF16)
        av_ref[:, sl] = blk(O_AV, j).astype(BF16)
    for j in range(B_HEADS):
        sl = slice(j * LANES, (j + 1) * LANES)
        bq_ref[:, sl] = _rope128(_norm_seg128(blk(O_BQ, j), g_bq), cb, sb).astype(BF16)
    for j in range(B_KV_HEADS):
        sl = slice(j * LANES, (j + 1) * LANES)
        bk_ref[:, sl] = _rope128(_norm_seg128(blk(O_BK, j), g_bk), cb, sb).astype(BF16)
        bv_ref[:, sl] = blk(O_BV, j).astype(BF16)
    cq = z[:, O_CQ:O_CQ + C_Q_RANK]
    cqn = cq * lax.rsqrt(jnp.mean(cq * cq, axis=-1, keepdims=True) + EPS) * gcq_ref[...]
    q = jnp.dot(cqn.astype(BF16), wuq_ref[...], preferred_element_type=F32)
    ckv = z[:, O_CKV:O_CKV + C_KV_RANK]
    ckvn = ckv * lax.rsqrt(jnp.mean(ckv * ckv, axis=-1, keepdims=True) + EPS) * gckv_ref[...]
    kv = jnp.dot(ckvn.astype(BF16), wukv_ref[...], preferred_element_type=F32)
    krope = _rope64(_norm_low64(z[:, O_CKR:O_CKR + LANES], g_ckr), ca, saa, sab).astype(BF16)
    for hh in range(C_HEADS):
        b0 = hh * C_HEAD_PAD
        cq_ref[:, b0:b0 + LANES] = _norm_seg128(q[:, b0:b0 + LANES], g_cqn).astype(BF16)
        cq_ref[:, b0 + LANES:b0 + 2 * LANES] = _rope64(
            _norm_low64(q[:, b0 + LANES:b0 + 2 * LANES], g_cqr), ca, saa, sab).astype(BF16)
        ck_ref[:, b0:b0 + LANES] = _norm_seg128(kv[:, b0:b0 + LANES], g_ckn).astype(BF16)
        ck_ref[:, b0 + LANES:b0 + 2 * LANES] = krope
        cv_ref[:, hh * LANES:(hh + 1) * LANES] = kv[:, b0 + LANES:b0 + 2 * LANES].astype(BF16)


def _split_rows(n_a_tiles):
    first = lambda i: (jnp.minimum(i, n_a_tiles - 1), 0)
    second = lambda i: (jnp.maximum(i - n_a_tiles, 0), 0)
    return first, second


def _prep(layer, xa, xb, n_tiles, n_a_tiles, mod48, mod_row, g1, w_in, tables, rope_blk, gains, gcq, gckv,
          wuq_bf, wukv_bf):
    d = xa.shape[1]
    tm = TM_PREP
    m = n_tiles * tm
    row = lambda i: (i, 0)
    const = lambda i: (0, 0)
    first, second = _split_rows(n_a_tiles)
    tab_spec = pl.BlockSpec((tm, LANES), lambda i: (rope_blk(i), 0))
    widths = (512, 512, 512, 1024, 256, 256, C_HEADS * C_HEAD_PAD, C_HEADS * C_HEAD_PAD, 512)
    return pl.pallas_call(
        functools.partial(_prep_kernel, n_a_tiles=n_a_tiles, layer=layer),
        out_shape=[jax.ShapeDtypeStruct((m, w), BF16) for w in widths],
        grid=(n_tiles,),
        in_specs=[
            pl.BlockSpec((tm, d), first),
            pl.BlockSpec((tm, d), second),
            pl.BlockSpec((1, 1, d), lambda i: (mod_row(i) * 6 + 0, 0, 0)),
            pl.BlockSpec((1, 1, d), lambda i: (mod_row(i) * 6 + 1, 0, 0)),
            pl.BlockSpec((1, d), const),
            pl.BlockSpec(memory_space=pl.ANY),
            tab_spec, tab_spec, tab_spec, tab_spec, tab_spec,
            pl.BlockSpec((8, LANES), const),
            pl.BlockSpec((1, C_Q_RANK), const),
            pl.BlockSpec((1, C_KV_RANK), const),
            pl.BlockSpec((C_Q_RANK, C_HEADS * C_HEAD_PAD), const, pipeline_mode=pl.Buffered(1)),
            pl.BlockSpec((C_KV_RANK, C_HEADS * C_HEAD_PAD), const, pipeline_mode=pl.Buffered(1)),
        ],
        out_specs=[pl.BlockSpec((tm, w), row) for w in widths],
        scratch_shapes=[pltpu.VMEM((d, D_IN_PAD), BF16),
                        pltpu.VMEM((2, W_STAGE_ROWS, w_in.shape[2]), F32),
                        pltpu.SemaphoreType.DMA((2,))],
        compiler_params=_cparams(("arbitrary",)),
        name="prep",
    )(xa, xb, mod48, mod48, g1, w_in, *tables, gains, gcq, gckv, wuq_bf, wukv_bf)


def _dot_nt(a, b):
    return lax.dot_general(a, b, (((1,), (1,)), ((), ())), preferred_element_type=F32)


def _dot_tn(a, b):
    return lax.dot_general(a, b, (((0,), (0,)), ((), ())), preferred_element_type=F32)


def _softmax_pv_t(q, k_refs, v_refs):
    s = [_dot_nt(k[...], q) for k in k_refs]
    m = functools.reduce(jnp.maximum, [jnp.max(si, axis=0, keepdims=True) for si in s])
    e = [jnp.exp2(si - m) for si in s]
    l = functools.reduce(jnp.add, [jnp.sum(ei, axis=0, keepdims=True) for ei in e])
    o = functools.reduce(jnp.add, [_dot_tn(v[...], ei.astype(BF16)) for ei, v in zip(e, v_refs)])
    return o * (1.0 / l)


def _attn_a_kernel(*refs, n_src, lam_init):
    lv_ref, gsub_ref, q_ref = refs[0], refs[1], refs[2]
    k_refs = refs[3:3 + n_src]
    v_refs = refs[3 + n_src:3 + 2 * n_src]
    o_ref = refs[3 + 2 * n_src]
    tq = q_ref.shape[0]
    lv = lv_ref[...]
    lam = (jnp.exp(jnp.sum(lv[0:1] * lv[1:2], axis=-1, keepdims=True))
           - jnp.exp(jnp.sum(lv[2:3] * lv[3:4], axis=-1, keepdims=True)) + lam_init)
    q = q_ref[...]
    lo = lax.broadcasted_iota(jnp.int32, (1, LANES), 1) < 64
    zero = jnp.zeros_like(q)
    qq = jnp.concatenate([jnp.where(lo, q, zero), jnp.where(lo, zero, q)], axis=0)
    o2 = _softmax_pv_t(qq, k_refs, v_refs)
    o = o2[:, :tq] - lam * o2[:, tq:]
    ms = jnp.mean(o * o, axis=0, keepdims=True)
    o = (o * lax.rsqrt(ms + EPS)).T
    o_ref[...] = (o * gsub_ref[...] * (1.0 - lam_init)).astype(BF16)


def _attn_a(lv, gsub, aq, ak, av, q_row0, q_rows, srcs, n_batch, layer_idx):
    tq = min(TQ_A, q_rows)
    nq = q_rows // tq
    qb0 = q_row0 // tq
    lam_init = 0.8 - 0.6 * math.exp(-0.3 * layer_idx)
    in_specs = [
        pl.BlockSpec((4, A_DK), lambda b, h, i: (0, 0)),
        pl.BlockSpec((1, A_DV), lambda b, h, i: (0, 0)),
        pl.BlockSpec((tq, LANES), lambda b, h, i: (qb0 + b * nq + i, h)),
    ]
    kv_specs = [pl.BlockSpec((rows, LANES), lambda b, h, i, blk0=row0 // rows: (blk0 + b, h))
                for row0, rows in srcs]
    return pl.pallas_call(
        functools.partial(_attn_a_kernel, n_src=len(srcs), lam_init=lam_init),
        out_shape=jax.ShapeDtypeStruct((n_batch * q_rows, A_HEADS * A_DV), BF16),
        grid=(n_batch, A_HEADS, nq),
        in_specs=in_specs + kv_specs + kv_specs,
        out_specs=pl.BlockSpec((tq, LANES), lambda b, h, i: (b * nq + i, h)),
        compiler_params=_cparams(("arbitrary", "arbitrary", "arbitrary")),
        name="attn_a",
    )(lv, gsub, aq, *([ak] * len(srcs)), *([av] * len(srcs)))


def _attn_c_kernel(*refs, n_src):
    q_ref = refs[0]
    k_refs = refs[1:1 + n_src]
    v_refs = refs[1 + n_src:1 + 2 * n_src]
    o_ref = refs[1 + 2 * n_src]
    for hh in range(C_HEADS_PER_STEP):
        qk = slice(hh * C_HEAD_PAD, (hh + 1) * C_HEAD_PAD)
        dv = slice(hh * C_DV, (hh + 1) * C_DV)
        o = _softmax_pv_t(q_ref[:, qk], [k.at[:, qk] for k in k_refs], [v.at[:, dv] for v in v_refs])
        o_ref[:, dv] = o.T.astype(BF16)


def _attn_c(cq, ck, cv, q_row0, q_rows, srcs, n_batch):
    tq = min(TQ_C, q_rows)
    nq = q_rows // tq
    qb0 = q_row0 // tq
    hp = C_HEADS_PER_STEP
    in_specs = [pl.BlockSpec((tq, hp * C_HEAD_PAD), lambda b, h, i: (qb0 + b * nq + i, h))]
    k_specs = [pl.BlockSpec((rows, hp * C_HEAD_PAD), lambda b, h, i, blk0=row0 // rows: (blk0 + b, h))
               for row0, rows in srcs]
    v_specs = [pl.BlockSpec((rows, hp * C_DV), lambda b, h, i, blk0=row0 // rows: (blk0 + b, h))
               for row0, rows in srcs]
    return pl.pallas_call(
        functools.partial(_attn_c_kernel, n_src=len(srcs)),
        out_shape=jax.ShapeDtypeStruct((n_batch * q_rows, C_HEADS * C_DV), BF16),
        grid=(n_batch, C_HEADS // hp, nq),
        in_specs=in_specs + k_specs + v_specs,
        out_specs=pl.BlockSpec((tq, hp * C_DV), lambda b, h, i: (b * nq + i, h)),
        compiler_params=_cparams(("arbitrary", "arbitrary", "arbitrary")),
        name="attn_c",
    )(cq, *([ck] * len(srcs)), *([cv] * len(srcs)))


def _stack_heads(q):
    return jnp.concatenate([q[:, g * LANES:(g + 1) * LANES] for g in range(B_GROUP)], axis=0)


def _sink_row(sink_ref, kvh, cols):
    return jnp.concatenate(
        [jnp.full((1, cols), sink_ref[kvh * B_GROUP + g] * LOG2E, F32) for g in range(B_GROUP)], axis=1)


def _attn_b_kernel(sink_ref, q_ref, kp_ref, km_ref, kn_ref, vp_ref, vm_ref, vn_ref, kc_ref, vc_ref, o_ref):
    kvh = pl.program_id(1)
    qb = pl.program_id(2)
    nqb = pl.num_programs(2)
    n_blk = QB_B // BLOCK
    kband = jnp.concatenate([kp_ref[...], km_ref[...], kn_ref[...]], axis=0)
    vband = jnp.concatenate([vp_ref[...], vm_ref[...], vn_ref[...]], axis=0)
    kc, vc = kc_ref[...], vc_ref[...]
    sink = _sink_row(sink_ref, kvh, BLOCK)
    c = lax.broadcasted_iota(jnp.int32, (3 * BLOCK, B_GROUP * BLOCK), 0)
    r = lax.broadcasted_iota(jnp.int32, (3 * BLOCK, B_GROUP * BLOCK), 1) % BLOCK
    cr = c - r
    band_ok = (cr >= 0) & (cr <= BLOCK + WINDOW)
    for j in range(n_blk):
        q4 = _stack_heads(q_ref[j * BLOCK:(j + 1) * BLOCK, :])
        s_loc = _dot_nt(kband[j * BLOCK:(j + 3) * BLOCK], q4)
        valid = band_ok
        if j == 0:
            valid = valid & (c >= jnp.where(qb > 0, 0, BLOCK))
        if j == n_blk - 1:
            valid = valid & (c < jnp.where(qb < nqb - 1, 3 * BLOCK, 2 * BLOCK))
        s_loc = jnp.where(valid, s_loc, NEG_INF)
        s_ctx = _dot_nt(kc, q4)
        m = jnp.maximum(jnp.maximum(jnp.max(s_loc, axis=0, keepdims=True),
                                    jnp.max(s_ctx, axis=0, keepdims=True)), sink)
        e_loc, e_ctx = jnp.exp2(s_loc - m), jnp.exp2(s_ctx - m)
        l = (jnp.sum(e_loc, axis=0, keepdims=True) + jnp.sum(e_ctx, axis=0, keepdims=True)
             + jnp.exp2(sink - m))
        o = (_dot_tn(vband[j * BLOCK:(j + 3) * BLOCK], e_loc.astype(BF16))
             + _dot_tn(vc, e_ctx.astype(BF16))) * (1.0 / l)
        o = o.T
        for g in range(B_GROUP):
            o_ref[j * BLOCK:(j + 1) * BLOCK, g * LANES:(g + 1) * LANES] = (
                o[g * BLOCK:(g + 1) * BLOCK].astype(BF16))


def _attn_b(sink, bq, bk, bv, n_batch, n_lat, n_ctx):
    nqb = n_lat // QB_B
    per = QB_B // BLOCK
    blocks_per_batch = n_lat // BLOCK
    ctx_blk0 = n_batch * n_lat // n_ctx
    gw = B_GROUP * B_DH
    prev = lambda b, h, i: (b * blocks_per_batch + jnp.maximum(i * per - 1, 0), h)
    main = lambda b, h, i: (b * nqb + i, h)
    nxt = lambda b, h, i: (b * blocks_per_batch + jnp.minimum(i * per + per, blocks_per_batch - 1), h)
    ctx = lambda b, h, i: (ctx_blk0 + b, h)
    return pl.pallas_call(
        _attn_b_kernel,
        out_shape=jax.ShapeDtypeStruct((n_batch * n_lat, B_HEADS * B_DH), BF16),
        grid=(n_batch, B_KV_HEADS, nqb),
        in_specs=[
            pl.BlockSpec(memory_space=pltpu.SMEM),
            pl.BlockSpec((QB_B, gw), main),
            pl.BlockSpec((BLOCK, B_DH), prev), pl.BlockSpec((QB_B, B_DH), main), pl.BlockSpec((BLOCK, B_DH), nxt),
            pl.BlockSpec((BLOCK, B_DH), prev), pl.BlockSpec((QB_B, B_DH), main), pl.BlockSpec((BLOCK, B_DH), nxt),
            pl.BlockSpec((n_ctx, B_DH), ctx), pl.BlockSpec((n_ctx, B_DH), ctx),
        ],
        out_specs=pl.BlockSpec((QB_B, gw), main),
        compiler_params=_cparams(("arbitrary", "arbitrary", "arbitrary")),
        name="attn_b",
    )(sink, bq, bk, bk, bk, bv, bv, bv, bk, bv)


def _attn_b_ctx_kernel(sink_ref, q_ref, k_ref, v_ref, o_ref):
    kvh = pl.program_id(1)
    rows = q_ref.shape[0]
    q4 = _stack_heads(q_ref[...])
    sink = _sink_row(sink_ref, kvh, rows)
    s = _dot_nt(k_ref[...], q4)
    m = jnp.maximum(jnp.max(s, axis=0, keepdims=True), sink)
    e = jnp.exp2(s - m)
    l = jnp.sum(e, axis=0, keepdims=True) + jnp.exp2(sink - m)
    o = (_dot_tn(v_ref[...], e.astype(BF16)) * (1.0 / l)).T
    for g in range(B_GROUP):
        o_ref[:, g * LANES:(g + 1) * LANES] = o[g * rows:(g + 1) * rows].astype(BF16)


def _attn_b_ctx(sink, bq, bk, bv, n_batch, n_lat, n_ctx):
    gw = B_GROUP * B_DH
    ctx_blk0 = n_batch * n_lat // n_ctx
    return pl.pallas_call(
        _attn_b_ctx_kernel,
        out_shape=jax.ShapeDtypeStruct((n_batch * n_ctx, B_HEADS * B_DH), BF16),
        grid=(n_batch, B_KV_HEADS),
        in_specs=[
            pl.BlockSpec(memory_space=pltpu.SMEM),
            pl.BlockSpec((n_ctx, gw), lambda b, h: (ctx_blk0 + b, h)),
            pl.BlockSpec((n_ctx, B_DH), lambda b, h: (ctx_blk0 + b, h)),
            pl.BlockSpec((n_ctx, B_DH), lambda b, h: (ctx_blk0 + b, h)),
        ],
        out_specs=pl.BlockSpec((n_ctx, gw), lambda b, h: (b, h)),
        compiler_params=_cparams(("arbitrary", "arbitrary")),
        name="attn_b_ctx",
    )(sink, bq, bk, bv)


def _route(h2, rw_ref, rb_ref):
    tm = h2.shape[0]
    h_hi = h2.astype(BF16)
    h_lo = (h2 - h_hi.astype(F32)).astype(BF16)
    p = jnp.dot(jnp.concatenate([h_hi, h_lo], axis=0), rw_ref[...], preferred_element_type=F32)
    p = p[:tm] + p[tm:]
    logits = p[:, :N_EXPERTS] + p[:, N_EXPERTS:]
    scores = 1.0 / (1.0 + jnp.exp(-logits))
    sel = scores + rb_ref[...]
    lane_i = lax.broadcasted_iota(jnp.int32, sel.shape, 1)
    lane = lane_i.astype(F32)
    big = float(N_EXPERTS)

    def top2(mask):
        v = jnp.where(mask, sel, -jnp.inf)
        m1 = jnp.max(v, axis=-1, keepdims=True)
        i1 = jnp.min(jnp.where(v == m1, lane, big), axis=-1, keepdims=True)
        v2 = jnp.where(lane == i1, -jnp.inf, v)
        m2 = jnp.max(v2, axis=-1, keepdims=True)
        i2 = jnp.min(jnp.where(v2 == m2, lane, big), axis=-1, keepdims=True)
        return m1, i1, m2, i2

    best = None
    for g in range(N_GROUPS):
        m1, i1, m2, i2 = top2((lane_i >= g * EXPERTS_PER_GROUP) & (lane_i < (g + 1) * EXPERTS_PER_GROUP))
        gs = m1 + m2
        if best is None:
            best = (gs, i1, i2)
        else:
            take = gs > best[0]
            best = (jnp.where(take, gs, best[0]), jnp.where(take, i1, best[1]), jnp.where(take, i2, best[2]))
    _, e1, e2 = best
    w1 = jnp.sum(jnp.where(lane == e1, scores, 0.0), axis=-1, keepdims=True)
    w2 = jnp.sum(jnp.where(lane == e2, scores, 0.0), axis=-1, keepdims=True)
    tot = w1 + w2
    return e1, e2, w1 / tot, w2 / tot


def _outproj_kernel(oa1_ref, ob1_ref, oc1_ref, x1in_ref, oa2_ref, ob2_ref, oc2_ref, x2in_ref,
                    g1_ref, sh2_ref, sc2_ref, n2_ref, w_hbm, rw_ref, rb_ref,
                    x1_ref, h2_ref, idx_ref, wt_ref, cnt_ref, w_ref, w_stage, w_sem, cnt_acc,
                    *, n_a_tiles, layer):
    @pl.when(pl.program_id(0) == 0)
    def _():
        _load_weight_bf16(w_hbm, layer, w_stage, w_sem, w_ref)

    first = pl.program_id(0) < n_a_tiles
    oa = jnp.where(first, oa1_ref[...], oa2_ref[...])
    ob = jnp.where(first, ob1_ref[...], ob2_ref[...])
    oc = jnp.where(first, oc1_ref[...], oc2_ref[...])
    x = jnp.where(first, x1in_ref[...], x2in_ref[...])
    y = jnp.dot(jnp.concatenate([oa, ob, oc], axis=1), w_ref[...], preferred_element_type=F32)
    x1 = x + g1_ref[0] * y
    x1_ref[...] = x1
    ms = jnp.mean(x1 * x1, axis=-1, keepdims=True)
    h2 = x1 * lax.rsqrt(ms + EPS) * n2_ref[...]
    h2 = h2 * (1.0 + sc2_ref[0]) + sh2_ref[0]
    tm = h2.shape[0]
    for cc in range(ROW_CHUNKS):
        h2_ref[pl.ds(cc, tm, stride=ROW_CHUNKS), :] = h2[:, cc * LANES:(cc + 1) * LANES]
    e1, e2, w1, w2 = _route(h2, rw_ref, rb_ref)
    lane = lax.broadcasted_iota(jnp.int32, idx_ref.shape, 1)
    wt_ref[...] = jnp.where(lane == 0, w1, jnp.where(lane == 1, w2, 0.0))

    @pl.when(pl.program_id(0) == 0)
    def _():
        cnt_acc[...] = jnp.zeros_like(cnt_acc)

    lane_f = lane.astype(F32)
    sel1, sel2 = lane_f == e1, lane_f == e2
    picks = jnp.where(sel1, 1.0, 0.0) + jnp.where(sel2, 1.0, 0.0)
    t_row = lax.broadcasted_iota(jnp.int32, (tm, tm), 0)
    t_col = lax.broadcasted_iota(jnp.int32, (tm, tm), 1)
    earlier = jnp.where(t_col < t_row, 1.0, 0.0).astype(BF16)
    before = jnp.dot(earlier, picks.astype(BF16), preferred_element_type=F32) + cnt_acc[0:1, :]
    r1 = jnp.sum(jnp.where(sel1, before, 0.0), axis=-1, keepdims=True)
    r2 = jnp.sum(jnp.where(sel2, before, 0.0), axis=-1, keepdims=True)
    idx_ref[...] = jnp.where(lane == 0, e1, jnp.where(lane == 1, e2, jnp.where(
        lane == 2, r1, jnp.where(lane == 3, r2, 0.0)))).astype(jnp.int32)
    cnt_acc[...] = cnt_acc[...] + jnp.sum(picks, axis=0, keepdims=True)
    cnt_ref[...] = cnt_acc[...]


def _outproj(layer, set_a, set_b, n_tiles, n_a_tiles, mod48, mod_row, n2, w_out, router_w, router_b):
    d = set_a[3].shape[1]
    tm = TM_PREP
    m = n_tiles * tm
    row = lambda i: (i, 0)
    const = lambda i: (0, 0)
    first, second = _split_rows(n_a_tiles)
    mod_spec = lambda j: pl.BlockSpec((1, 1, d), lambda i: (mod_row(i) * 6 + j, 0, 0))
    in_specs = (
        [pl.BlockSpec((tm, a.shape[1]), first) for a in set_a]
        + [pl.BlockSpec((tm, a.shape[1]), second) for a in set_b]
        + [mod_spec(2), mod_spec(3), mod_spec(4),
           pl.BlockSpec((1, d), const),
           pl.BlockSpec(memory_space=pl.ANY),
           pl.BlockSpec((d, 2 * N_EXPERTS), const),
           pl.BlockSpec((1, N_EXPERTS), const)])
    return pl.pallas_call(
        functools.partial(_outproj_kernel, n_a_tiles=n_a_tiles, layer=layer),
        out_shape=[jax.ShapeDtypeStruct((m, d), F32), jax.ShapeDtypeStruct((m * ROW_CHUNKS, LANES), F32),
                   jax.ShapeDtypeStruct((m, LANES), jnp.int32), jax.ShapeDtypeStruct((m, LANES), F32),
                   jax.ShapeDtypeStruct((8, LANES), F32)],
        grid=(n_tiles,),
        in_specs=in_specs,
        out_specs=[pl.BlockSpec((tm, d), row), pl.BlockSpec((tm * ROW_CHUNKS, LANES), row),
                   pl.BlockSpec((tm, LANES), row), pl.BlockSpec((tm, LANES), row),
                   pl.BlockSpec((8, LANES), const)],
        scratch_shapes=[pltpu.VMEM((d, d), BF16),
                        pltpu.VMEM((2, W_STAGE_ROWS, d), F32),
                        pltpu.SemaphoreType.DMA((2,)),
                        pltpu.VMEM((8, LANES), F32)],
        compiler_params=_cparams(("arbitrary",)),
        name="outproj",
    )(*set_a, *set_b, mod48, mod48, mod48, n2, w_out, router_w, router_b)


def _experts_kernel(te_ref, nxt_ref, ws_ref, nv_ref, src_ref, h2_hbm, wg_hbm, wu_hbm, wd_hbm, o_ref,
                    xbuf, xsem, wg_f, wu_f, wd_f, wsem, wg_bf, wu_bf, wd_bf, *, layer):
    i = pl.program_id(0)
    n_valid = nv_ref[0]
    slot = i % X_SLOTS

    def issue_rows(tile):
        s = tile % X_SLOTS
        base = tile * TM_E
        for r in range(TM_E):
            row0 = pl.multiple_of(src_ref[base + r] * ROW_CHUNKS, ROW_CHUNKS)
            pltpu.make_async_copy(h2_hbm.at[pl.ds(row0, ROW_CHUNKS), :],
                                  xbuf.at[s, pl.ds(r * X_PITCH, ROW_CHUNKS), :], xsem.at[s]).start()

    def wait_rows(s):
        pltpu.make_async_copy(h2_hbm.at[pl.ds(0, TM_E * ROW_CHUNKS), :],
                              xbuf.at[s, pl.ds(0, TM_E * ROW_CHUNKS), :], xsem.at[s]).wait()

    def weight_copies(e, s):
        return (pltpu.make_async_copy(wg_hbm.at[layer, e], wg_f.at[s], wsem.at[s]),
                pltpu.make_async_copy(wu_hbm.at[layer, e], wu_f.at[s], wsem.at[s]),
                pltpu.make_async_copy(wd_hbm.at[layer, e], wd_f.at[s], wsem.at[s]))

    @pl.when(i == 0)
    def _():
        for cp in weight_copies(te_ref[0], ws_ref[0]):
            cp.start()
        for t in range(X_SLOTS - 1):
            issue_rows(t)

    first_of_expert = jnp.logical_or(i == 0, te_ref[i] != te_ref[jnp.maximum(i - 1, 0)])

    @pl.when(jnp.logical_and(first_of_expert, i < n_valid))
    def _():
        ws = ws_ref[i]
        for cp in weight_copies(te_ref[i], ws):
            cp.wait()

        @pl.when(nxt_ref[i] >= 0)
        def _():
            for cp in weight_copies(nxt_ref[i], 1 - ws):
                cp.start(priority=WEIGHT_DMA_PRIORITY)

        wg_bf[...] = wg_f[ws].astype(BF16)
        wu_bf[...] = wu_f[ws].astype(BF16)
        wd_bf[...] = wd_f[ws].astype(BF16)

    @pl.when(i < n_valid)
    def _():
        wait_rows(slot)
        x = jnp.concatenate([xbuf[slot, pl.ds(cc, TM_E, stride=X_PITCH), :] for cc in range(ROW_CHUNKS)],
                            axis=1).astype(BF16)
        issue_rows(i + X_SLOTS - 1)
        g = jnp.dot(x, wg_bf[...], preferred_element_type=F32)
        u = jnp.dot(x, wu_bf[...], preferred_element_type=F32)
        a = (_silu(g) * u).astype(BF16)
        y = jnp.dot(a, wd_bf[...], preferred_element_type=F32)
        for cc in range(ROW_CHUNKS):
            o_ref[pl.ds(cc, TM_E, stride=ROW_CHUNKS), :] = y[:, cc * LANES:(cc + 1) * LANES]

    @pl.when(i == n_valid - 1)
    def _():
        for t in range(1, X_SLOTS):
            wait_rows((i + t) % X_SLOTS)

    @pl.when(i >= n_valid)
    def _():
        o_ref[...] = jnp.zeros_like(o_ref)


def _experts(layer, tile_expert, next_expert, w_slot, n_valid, src, h2_all, w_gate, w_up, w_down):
    n_rows = src.shape[0] - (X_SLOTS - 1) * TM_E
    nt = n_rows // TM_E
    d, de = w_gate.shape[2], w_gate.shape[3]
    grid_spec = pltpu.PrefetchScalarGridSpec(
        num_scalar_prefetch=5,
        grid=(nt,),
        in_specs=[pl.BlockSpec(memory_space=pl.ANY)] * 4,
        out_specs=pl.BlockSpec((TM_E * ROW_CHUNKS, LANES), lambda i, *_: (i, 0)),
        scratch_shapes=[
            pltpu.VMEM((X_SLOTS, TM_E * X_PITCH, LANES), F32),
            pltpu.SemaphoreType.DMA((X_SLOTS,)),
            pltpu.VMEM((2, d, de), F32), pltpu.VMEM((2, d, de), F32), pltpu.VMEM((2, de, d), F32),
            pltpu.SemaphoreType.DMA((2,)),
            pltpu.VMEM((d, de), BF16), pltpu.VMEM((d, de), BF16), pltpu.VMEM((de, d), BF16),
        ],
    )
    return pl.pallas_call(
        functools.partial(_experts_kernel, layer=layer),
        out_shape=jax.ShapeDtypeStruct((n_rows * ROW_CHUNKS, LANES), F32),
        grid_spec=grid_spec,
        compiler_params=_cparams(("arbitrary",)),
        name="experts",
    )(tile_expert, next_expert, w_slot, n_valid, src, h2_all, w_gate, w_up, w_down)


def _combine_kernel(pos_ref, o_hbm, x1_ref, wt_ref, g2_ref, x2_ref, buf, sem):
    i = pl.program_id(0)
    nt = pl.num_programs(0)
    slot = i % 2

    def issue(tile, s):
        base = tile * (TM_C * TOP_K)
        for r in range(TM_C):
            for k in range(TOP_K):
                row0 = pl.multiple_of(pos_ref[base + TOP_K * r + k] * ROW_CHUNKS, ROW_CHUNKS)
                pltpu.make_async_copy(o_hbm.at[pl.ds(row0, ROW_CHUNKS), :],
                                      buf.at[s, k, pl.ds(r * X_PITCH, ROW_CHUNKS), :], sem.at[s]
                                      ).start(priority=k % DMA_QUEUES)

    @pl.when(i == 0)
    def _():
        issue(0, 0)

    @pl.when(i + 1 < nt)
    def _():
        issue(i + 1, 1 - slot)

    def expert_rows(k):
        return jnp.concatenate([buf[slot, k, pl.ds(cc, TM_C, stride=X_PITCH), :] for cc in range(ROW_CHUNKS)],
                               axis=1)

    for k in range(TOP_K):
        pltpu.make_async_copy(o_hbm.at[pl.ds(0, TM_C * ROW_CHUNKS), :],
                              buf.at[slot, k, pl.ds(0, TM_C * ROW_CHUNKS), :], sem.at[slot]).wait()
    wt = wt_ref[...]
    y = wt[:, 0:1] * expert_rows(0) + wt[:, 1:2] * expert_rows(1)
    x2_ref[...] = x1_ref[...] + g2_ref[0] * y


def _combine(pos, o_sorted, x1, wts, mod48, mod_row):
    m, d = x1.shape
    grid_spec = pltpu.PrefetchScalarGridSpec(
        num_scalar_prefetch=1,
        grid=(m // TM_C,),
        in_specs=[
            pl.BlockSpec(memory_space=pl.ANY),
            pl.BlockSpec((TM_C, d), lambda i, p: (i, 0)),
            pl.BlockSpec((TM_C, LANES), lambda i, p: (i, 0)),
            pl.BlockSpec((1, 1, d), lambda i, p: (mod_row(i) * 6 + 5, 0, 0)),
        ],
        out_specs=pl.BlockSpec((TM_C, d), lambda i, p: (i, 0)),
        scratch_shapes=[pltpu.VMEM((2, TOP_K, TM_C * X_PITCH, LANES), F32), pltpu.SemaphoreType.DMA((2,))],
    )
    return pl.pallas_call(
        _combine_kernel,
        out_shape=jax.ShapeDtypeStruct((m, d), F32),
        grid_spec=grid_spec,
        compiler_params=_cparams(("arbitrary",)),
        name="combine",
    )(pos, o_sorted, x1, wts, mod48)


def _rope_tables(n_lat, tm):
    f32 = np.float32
    t = np.arange(n_lat)
    r = (t // GRID_W).astype(f32)
    col = (t % GRID_W).astype(f32)

    def cos_sin(dim):
        nf = dim // 4
        inv = (f32(ROPE_THETA) ** (-np.arange(nf, dtype=f32) / f32(nf))).astype(f32)
        ang = np.concatenate([r[:, None] * inv, col[:, None] * inv], axis=-1).astype(f32)
        return np.cos(ang).astype(f32), np.sin(ang).astype(f32)

    c64, s64 = cos_sin(A_DK)
    c128, s128 = cos_sin(B_DH)
    z32 = np.zeros_like(s64)
    tabs = [
        np.concatenate([c64, c64, c64, c64], axis=-1),
        np.concatenate([-s64, z32, -s64, z32], axis=-1),
        np.concatenate([z32, s64, z32, s64], axis=-1),
        np.concatenate([c128, c128], axis=-1),
        np.concatenate([-s128, s128], axis=-1),
    ]
    ident = [np.ones((tm, LANES), f32), np.zeros((tm, LANES), f32), np.zeros((tm, LANES), f32),
             np.ones((tm, LANES), f32), np.zeros((tm, LANES), f32)]
    return [jnp.asarray(np.concatenate([a, b], axis=0)) for a, b in zip(tabs, ident)]


def _layer_params(l, a_qn, a_kn, b_qn, b_kn, c_qa_norm, c_kva_norm, c_wuq, c_wukv, c_qn, c_kn):
    z64 = jnp.zeros((C_ROPE,), F32)
    gains = jnp.stack([
        jnp.tile(a_qn[l], 2) * (A_DK ** -0.5 * LOG2E),
        jnp.tile(a_kn[l], 2),
        b_qn[l] * (B_DH ** -0.5 * LOG2E),
        b_kn[l],
        c_qn[l][:C_NOPE] * (C_DQK ** -0.5 * LOG2E),
        jnp.concatenate([c_qn[l][C_NOPE:] * (C_DQK ** -0.5 * LOG2E), z64]),
        c_kn[l][:C_NOPE],
        jnp.concatenate([c_kn[l][C_NOPE:], z64]),
    ])
    wq = c_wuq[l].reshape(C_Q_RANK, C_HEADS, C_DQK)
    wq = jnp.pad(wq, ((0, 0), (0, 0), (0, C_HEAD_PAD - C_DQK))).reshape(C_Q_RANK, C_HEADS * C_HEAD_PAD)
    return dict(gains=gains, gcq=c_qa_norm[l][None], gckv=c_kva_norm[l][None],
                wuq=wq.astype(BF16), wukv=c_wukv[l].astype(BF16))


def _sorted_rows(idx, rank, counts):
    t = idx.shape[0]
    n_pairs = t * TOP_K
    n_rows = ((n_pairs + N_EXPERTS * (TM_E - 1)) // TM_E) * TM_E
    nt = n_rows // TM_E
    flat_e = idx.reshape(-1)
    rank = rank.reshape(-1)
    padded = ((counts + TM_E - 1) // TM_E) * TM_E
    ends = jnp.cumsum(padded)
    pos = (ends - padded)[flat_e] + rank
    src = jnp.zeros((n_rows + (X_SLOTS - 1) * TM_E,), jnp.int32).at[pos].set(
        jnp.arange(n_pairs, dtype=jnp.int32) // TOP_K, unique_indices=True)
    tile_start = jnp.arange(nt, dtype=jnp.int32) * TM_E
    tile_expert = jnp.sum((ends[None, :] <= tile_start[:, None]).astype(jnp.int32), axis=1)
    last_used = jnp.sum((ends <= ends[-1] - 1).astype(jnp.int32))
    tile_expert = jnp.minimum(tile_expert, last_used)
    n_valid = (ends[-1:] // TM_E).astype(jnp.int32)
    used = counts > 0
    e_ids = jnp.arange(N_EXPERTS, dtype=jnp.int32)
    later = jnp.where(used[None, :] & (e_ids[None, :] > e_ids[:, None]), e_ids[None, :], N_EXPERTS)
    next_used = jnp.min(later, axis=1)
    next_used = jnp.where(next_used == N_EXPERTS, -1, next_used).astype(jnp.int32)
    parity = ((jnp.cumsum(used.astype(jnp.int32)) - 1) % 2).astype(jnp.int32)
    return (pos.astype(jnp.int32), src, tile_expert.astype(jnp.int32), next_used[tile_expert],
            parity[tile_expert], n_valid)


def kernel(x, c, ctx, c_ctx, ada_w, ada_b, norm1_g, norm2_g, w_in, w_out, a_qn, a_kn, a_lambda, a_subln,
           b_qn, b_kn, b_sink, c_qa_norm, c_kva_norm, c_wuq, c_wukv, c_qn, c_kn,
           router_w, router_bias, moe_w_gate, moe_w_up, moe_w_down):
    bsz, n_lat, d = x.shape
    n_ctx = ctx.shape[1]
    depth = ada_w.shape[0]
    t_lat, t_ctx = bsz * n_lat, bsz * n_ctx
    tm = TM_PREP
    lat_tiles = n_lat // tm
    n_lat_tiles, n_ctx_tiles = t_lat // tm, t_ctx // tm
    n_all_tiles = n_lat_tiles + n_ctx_tiles

    cond8 = jnp.concatenate([c, c_ctx[None], jnp.zeros((8 - bsz - 1, d), F32)], axis=0)
    mod = _ada_modulation(cond8, ada_w, ada_b)
    tables = _rope_tables(n_lat, tm)
    rb = router_bias[None]
    rw_hi = router_w.astype(BF16)
    rw_lo = (router_w - rw_hi.astype(F32)).astype(BF16)
    rw2 = jnp.concatenate([rw_hi, rw_lo], axis=1)

    mod_row = lambda i: jnp.minimum(i // lat_tiles, bsz)
    mod_row_c = lambda i: jnp.minimum(i // (n_lat // TM_C), bsz)
    rope_blk = lambda i: jnp.where(i < n_lat_tiles, i % lat_tiles, lat_tiles)

    xa, xb, n_a_tiles = x.reshape(t_lat, d), ctx.reshape(t_ctx, d), n_lat_tiles
    lat_src, ctx_src = (0, n_lat), (t_lat, n_ctx)
    for l in range(depth):
        last = l == depth - 1
        p = _layer_params(l, a_qn, a_kn, b_qn, b_kn, c_qa_norm, c_kva_norm, c_wuq, c_wukv, c_qn, c_kn)
        mod48 = mod[l].reshape(8 * 6, 1, d)
        aq, ak, av, bq, bk, bv, cq, ck, cv = _prep(
            l, xa, xb, n_all_tiles, n_a_tiles, mod48, mod_row, norm1_g[l][None], w_in, tables, rope_blk,
            p["gains"], p["gcq"], p["gckv"], p["wuq"], p["wukv"])

        lv, gsub, sink = a_lambda[l], a_subln[l][None], b_sink[l]
        o_a = _attn_a(lv, gsub, aq, ak, av, 0, n_lat, [lat_src, ctx_src], bsz, l)
        o_b = _attn_b(sink, bq, bk, bv, bsz, n_lat, n_ctx)
        o_c = _attn_c(cq, ck, cv, 0, n_lat, [lat_src, ctx_src], bsz)
        set_a = (o_a, o_b, o_c, xa)
        if not last:
            oc_a = _attn_a(lv, gsub, aq, ak, av, t_lat, n_ctx, [ctx_src], bsz, l)
            oc_b = _attn_b_ctx(sink, bq, bk, bv, bsz, n_lat, n_ctx)
            oc_c = _attn_c(cq, ck, cv, t_lat, n_ctx, [ctx_src], bsz)
            set_b, n_tok_tiles = (oc_a, oc_b, oc_c, xb), n_all_tiles
        else:
            set_b, n_tok_tiles = set_a, n_lat_tiles
        x1, h2, idx, wts, cnt = _outproj(l, set_a, set_b, n_tok_tiles, min(n_a_tiles, n_tok_tiles), mod48,
                                         mod_row, norm2_g[l][None], w_out, rw2, rb)

        pos, src, tile_expert, next_expert, w_slot, n_valid = _sorted_rows(
            idx[:, :TOP_K], idx[:, TOP_K:2 * TOP_K], cnt[0, :N_EXPERTS].astype(jnp.int32))
        o_sorted = _experts(l, tile_expert, next_expert, w_slot, n_valid, src, h2,
                            moe_w_gate, moe_w_up, moe_w_down)
        xa = _combine(pos, o_sorted, x1, wts, mod48, mod_row_c)
        xb, n_a_tiles = xa, n_all_tiles
    return xa.reshape(bsz, n_lat, d)
```

```python
import functools
import math

import jax
import jax.numpy as jnp
import numpy as np
from jax import lax
from jax.experimental import pallas as pl
from jax.experimental.pallas import tpu as pltpu

F32 = jnp.float32
BF16 = jnp.bfloat16

D_MODEL = 2048
GRID_W = 64
BLOCK = 128
WINDOW = 128
ROPE_THETA = 10000.0
EPS = 1e-6
NEG_INF = -1e30
LOG2E = math.log2(math.e)
A_HEADS, A_DK = 4, 64
A_DV = 2 * A_DK
B_HEADS, B_KV_HEADS, B_DH = 8, 2, 128
B_GROUP = B_HEADS // B_KV_HEADS
C_HEADS, C_Q_RANK, C_KV_RANK, C_NOPE, C_ROPE, C_DV = 4, 512, 256, 128, 64, 128
C_DQK = C_NOPE + C_ROPE
SPLIT_SIZES = (A_HEADS * 2 * A_DK, A_HEADS * 2 * A_DK, A_HEADS * A_DV,
               B_HEADS * B_DH, B_KV_HEADS * B_DH, B_KV_HEADS * B_DH,
               C_Q_RANK, C_KV_RANK, C_ROPE)
D_IN = sum(SPLIT_SIZES)
N_EXPERTS, N_GROUPS, TOP_K = 32, 4, 2
EXPERTS_PER_GROUP = N_EXPERTS // N_GROUPS
D_EXPERT = 512

LANES = 128
V7X_VMEM_LIMIT = 56 * 1024 * 1024

D_IN_PAD = ((D_IN + LANES - 1) // LANES) * LANES
C_HEAD_PAD = 2 * LANES
TM_PREP = 256
TQ_A = 1024
TQ_C = 2048
C_HEADS_PER_STEP = 1
QB_B = 1024
TM_E = 128
X_SLOTS = 3
ROW_CHUNKS = D_MODEL // LANES
X_PITCH = ROW_CHUNKS + 8
DMA_QUEUES = 2
WEIGHT_DMA_PRIORITY = 1
TM_C = 128
ADA_TN = 1024
W_STAGE_ROWS = 128

_OFF = [0]
for _s in SPLIT_SIZES:
    _OFF.append(_OFF[-1] + _s)
O_AQ, O_AK, O_AV, O_BQ, O_BK, O_BV, O_CQ, O_CKV, O_CKR, _ = _OFF


def _cparams(sem):
    return pltpu.CompilerParams(dimension_semantics=sem, vmem_limit_bytes=V7X_VMEM_LIMIT)


def _silu(v):
    return v * (1.0 / (1.0 + jnp.exp(-v)))


def _ada_kernel(cond_ref, w_ref, b_ref, o_ref):
    s = _silu(cond_ref[...]).astype(BF16)
    o_ref[0] = jnp.dot(s, w_ref[0].astype(BF16), preferred_element_type=F32) + b_ref[0]


def _ada_modulation(cond8, ada_w, ada_b):
    depth, d, n = ada_w.shape
    return pl.pallas_call(
        _ada_kernel,
        out_shape=jax.ShapeDtypeStruct((depth, 8, n), F32),
        grid=(depth, n // ADA_TN),
        in_specs=[
            pl.BlockSpec((8, d), lambda l, j: (0, 0)),
            pl.BlockSpec((1, d, ADA_TN), lambda l, j: (l, 0, j)),
            pl.BlockSpec((1, 1, ADA_TN), lambda l, j: (l, 0, j)),
        ],
        out_specs=pl.BlockSpec((1, 8, ADA_TN), lambda l, j: (l, 0, j)),
        compiler_params=_cparams(("arbitrary", "arbitrary")),
        name="ada_modulation",
    )(cond8, ada_w, ada_b.reshape(depth, 1, n))


def _rope64(v, c, sa, sb):
    return v * c + pltpu.roll(v, 96, 1) * sa + pltpu.roll(v, 32, 1) * sb


def _rope128(v, c, s):
    return v * c + pltpu.roll(v, 64, 1) * s


def _norm_seg128(v, g):
    ms = jnp.sum(v * v, axis=-1, keepdims=True) * (1.0 / 128)
    return v * lax.rsqrt(ms + EPS) * g


def _norm_seg64x2(v, g, lo):
    sq = v * v
    s_lo = jnp.sum(jnp.where(lo, sq, 0.0), axis=-1, keepdims=True)
    s_hi = jnp.sum(jnp.where(lo, 0.0, sq), axis=-1, keepdims=True)
    ms = jnp.where(lo, s_lo, s_hi) * (1.0 / 64)
    return v * lax.rsqrt(ms + EPS) * g


def _norm_low64(v, g):
    ms = jnp.sum(v * v, axis=-1, keepdims=True) * (1.0 / 64)
    return v * lax.rsqrt(ms + EPS) * g


def _load_weight_bf16(w_hbm, layer, stage, sem, w_bf):
    k, n = w_hbm.shape[1], w_hbm.shape[2]
    ch = stage.shape[1]

    def chunk_copy(c):
        return pltpu.make_async_copy(w_hbm.at[layer, pl.ds(c * ch, ch), :], stage.at[c % 2], sem.at[c % 2])

    n_pad = w_bf.shape[1]
    if n_pad > n:
        edge = (n // LANES) * LANES
        w_bf[:, edge:n_pad] = jnp.zeros((k, n_pad - edge), BF16)
    chunk_copy(0).start()
    for c in range(k // ch):
        if c + 1 < k // ch:
            chunk_copy(c + 1).start()
        chunk_copy(c).wait()
        w_bf[c * ch:(c + 1) * ch, 0:n] = stage[c % 2].astype(BF16)


def _prep_kernel(xa_ref, xb_ref, sh_ref, sc_ref, g1_ref, w_hbm, ca_ref, saa_ref, sab_ref, cb_ref, sb_ref,
                 gains_ref, gcq_ref, gckv_ref, wuq_ref, wukv_ref,
                 aq_ref, ak_ref, av_ref, bq_ref, bk_ref, bv_ref, cq_ref, ck_ref, cv_ref,
                 w_bf, w_stage, w_sem, *, n_a_tiles, layer):
    @pl.when(pl.program_id(0) == 0)
    def _():
        _load_weight_bf16(w_hbm, layer, w_stage, w_sem, w_bf)

    x = jnp.where(pl.program_id(0) < n_a_tiles, xa_ref[...], xb_ref[...])
    ms = jnp.mean(x * x, axis=-1, keepdims=True)
    h = x * lax.rsqrt(ms + EPS) * g1_ref[...]
    h = h * (1.0 + sc_ref[0]) + sh_ref[0]
    z = jnp.dot(h.astype(BF16), w_bf[...], preferred_element_type=F32)

    lane = lax.broadcasted_iota(jnp.int32, (1, LANES), 1)
    lo = lane < 64
    ca, saa, sab = ca_ref[...], saa_ref[...], sab_ref[...]
    cb, sb = cb_ref[...], sb_ref[...]
    g_aq, g_ak, g_bq, g_bk = gains_ref[0:1], gains_ref[1:2], gains_ref[2:3], gains_ref[3:4]
    g_cqn, g_cqr, g_ckn, g_ckr = gains_ref[4:5], gains_ref[5:6], gains_ref[6:7], gains_ref[7:8]

    def blk(off, j):
        return z[:, off + j * LANES: off + (j + 1) * LANES]

    for j in range(A_HEADS):
        sl = slice(j * LANES, (j + 1) * LANES)
        aq_ref[:, sl] = _rope64(_norm_seg64x2(blk(O_AQ, j), g_aq, lo), ca, saa, sab).astype(BF16)
        ak_ref[:, sl] = _rope64(_norm_seg64x2(blk(O_AK, j), g_ak, lo), ca, saa, sab).astype(BF16)
        av_ref[:, sl] = blk(O_AV, j).astype(BF16)
    for j in range(B_HEADS):
        sl = slice(j * LANES, (j + 1) * LANES)
        bq_ref[:, sl] = _rope128(_norm_seg128(blk(O_BQ, j), g_bq), cb, sb).astype(BF16)
    for j in range(B_KV_HEADS):
        sl = slice(j * LANES, (j + 1) * LANES)
        bk_ref[:, sl] = _rope128(_norm_seg128(blk(O_BK, j), g_bk), cb, sb).astype(BF16)
        bv_ref[:, sl] = blk(O_BV, j).astype(BF16)
    cq = z[:, O_CQ:O_CQ + C_Q_RANK]
    cqn = cq * lax.rsqrt(jnp.mean(cq * cq, axis=-1, keepdims=True) + EPS) * gcq_ref[...]
    q = jnp.dot(cqn.astype(BF16), wuq_ref[...], preferred_element_type=F32)
    ckv = z[:, O_CKV:O_CKV + C_KV_RANK]
    ckvn = ckv * lax.rsqrt(jnp.mean(ckv * ckv, axis=-1, keepdims=True) + EPS) * gckv_ref[...]
    kv = jnp.dot(ckvn.astype(BF16), wukv_ref[...], preferred_element_type=F32)
    krope = _rope64(_norm_low64(z[:, O_CKR:O_CKR + LANES], g_ckr), ca, saa, sab).astype(BF16)
    for hh in range(C_HEADS):
        b0 = hh * C_HEAD_PAD
        cq_ref[:, b0:b0 + LANES] = _norm_seg128(q[:, b0:b0 + LANES], g_cqn).astype(BF16)
        cq_ref[:, b0 + LANES:b0 + 2 * LANES] = _rope64(
            _norm_low64(q[:, b0 + LANES:b0 + 2 * LANES], g_cqr), ca, saa, sab).astype(BF16)
        ck_ref[:, b0:b0 + LANES] = _norm_seg128(kv[:, b0:b0 + LANES], g_ckn).astype(BF16)
        ck_ref[:, b0 + LANES:b0 + 2 * LANES] = krope
        cv_ref[:, hh * LANES:(hh + 1) * LANES] = kv[:, b0 + LANES:b0 + 2 * LANES].astype(BF16)


def _split_rows(n_a_tiles):
    first = lambda i: (jnp.minimum(i, n_a_tiles - 1), 0)
    second = lambda i: (jnp.maximum(i - n_a_tiles, 0), 0)
    return first, second


def _prep(layer, xa, xb, n_tiles, n_a_tiles, mod48, mod_row, g1, w_in, tables, rope_blk, gains, gcq, gckv,
          wuq_bf, wukv_bf):
    d = xa.shape[1]
    tm = TM_PREP
    m = n_tiles * tm
    row = lambda i: (i, 0)
    const = lambda i: (0, 0)
    first, second = _split_rows(n_a_tiles)
    tab_spec = pl.BlockSpec((tm, LANES), lambda i: (rope_blk(i), 0))
    widths = (512, 512, 512, 1024, 256, 256, C_HEADS * C_HEAD_PAD, C_HEADS * C_HEAD_PAD, 512)
    return pl.pallas_call(
        functools.partial(_prep_kernel, n_a_tiles=n_a_tiles, layer=layer),
        out_shape=[jax.ShapeDtypeStruct((m, w), BF16) for w in widths],
        grid=(n_tiles,),
        in_specs=[
            pl.BlockSpec((tm, d), first),
            pl.BlockSpec((tm, d), second),
            pl.BlockSpec((1, 1, d), lambda i: (mod_row(i) * 6 + 0, 0, 0)),
            pl.BlockSpec((1, 1, d), lambda i: (mod_row(i) * 6 + 1, 0, 0)),
            pl.BlockSpec((1, d), const),
            pl.BlockSpec(memory_space=pl.ANY),
            tab_spec, tab_spec, tab_spec, tab_spec, tab_spec,
            pl.BlockSpec((8, LANES), const),
            pl.BlockSpec((1, C_Q_RANK), const),
            pl.BlockSpec((1, C_KV_RANK), const),
            pl.BlockSpec((C_Q_RANK, C_HEADS * C_HEAD_PAD), const, pipeline_mode=pl.Buffered(1)),
            pl.BlockSpec((C_KV_RANK, C_HEADS * C_HEAD_PAD), const, pipeline_mode=pl.Buffered(1)),
        ],
        out_specs=[pl.BlockSpec((tm, w), row) for w in widths],
        scratch_shapes=[pltpu.VMEM((d, D_IN_PAD), BF16),
                        pltpu.VMEM((2, W_STAGE_ROWS, w_in.shape[2]), F32),
                        pltpu.SemaphoreType.DMA((2,))],
        compiler_params=_cparams(("arbitrary",)),
        name="prep",
    )(xa, xb, mod48, mod48, g1, w_in, *tables, gains, gcq, gckv, wuq_bf, wukv_bf)


def _dot_nt(a, b):
    return lax.dot_general(a, b, (((1,), (1,)), ((), ())), preferred_element_type=F32)


def _dot_tn(a, b):
    return lax.dot_general(a, b, (((0,), (0,)), ((), ())), preferred_element_type=F32)


def _softmax_pv_t(q, k_refs, v_refs):
    s = [_dot_nt(k[...], q) for k in k_refs]
    m = functools.reduce(jnp.maximum, [jnp.max(si, axis=0, keepdims=True) for si in s])
    e = [jnp.exp2(si - m) for si in s]
    l = functools.reduce(jnp.add, [jnp.sum(ei, axis=0, keepdims=True) for ei in e])
    o = functools.reduce(jnp.add, [_dot_tn(v[...], ei.astype(BF16)) for ei, v in zip(e, v_refs)])
    return o * (1.0 / l)


def _attn_a_kernel(*refs, n_src, lam_init):
    lv_ref, gsub_ref, q_ref = refs[0], refs[1], refs[2]
    k_refs = refs[3:3 + n_src]
    v_refs = refs[3 + n_src:3 + 2 * n_src]
    o_ref = refs[3 + 2 * n_src]
    tq = q_ref.shape[0]
    lv = lv_ref[...]
    lam = (jnp.exp(jnp.sum(lv[0:1] * lv[1:2], axis=-1, keepdims=True))
           - jnp.exp(jnp.sum(lv[2:3] * lv[3:4], axis=-1, keepdims=True)) + lam_init)
    q = q_ref[...]
    lo = lax.broadcasted_iota(jnp.int32, (1, LANES), 1) < 64
    zero = jnp.zeros_like(q)
    qq = jnp.concatenate([jnp.where(lo, q, zero), jnp.where(lo, zero, q)], axis=0)
    o2 = _softmax_pv_t(qq, k_refs, v_refs)
    o = o2[:, :tq] - lam * o2[:, tq:]
    ms = jnp.mean(o * o, axis=0, keepdims=True)
    o = (o * lax.rsqrt(ms + EPS)).T
    o_ref[...] = (o * gsub_ref[...] * (1.0 - lam_init)).astype(BF16)


def _attn_a(lv, gsub, aq, ak, av, q_row0, q_rows, srcs, n_batch, layer_idx):
    tq = min(TQ_A, q_rows)
    nq = q_rows // tq
    qb0 = q_row0 // tq
    lam_init = 0.8 - 0.6 * math.exp(-0.3 * layer_idx)
    in_specs = [
        pl.BlockSpec((4, A_DK), lambda b, h, i: (0, 0)),
        pl.BlockSpec((1, A_DV), lambda b, h, i: (0, 0)),
        pl.BlockSpec((tq, LANES), lambda b, h, i: (qb0 + b * nq + i, h)),
    ]
    kv_specs = [pl.BlockSpec((rows, LANES), lambda b, h, i, blk0=row0 // rows: (blk0 + b, h))
                for row0, rows in srcs]
    return pl.pallas_call(
        functools.partial(_attn_a_kernel, n_src=len(srcs), lam_init=lam_init),
        out_shape=jax.ShapeDtypeStruct((n_batch * q_rows, A_HEADS * A_DV), BF16),
        grid=(n_batch, A_HEADS, nq),
        in_specs=in_specs + kv_specs + kv_specs,
        out_specs=pl.BlockSpec((tq, LANES), lambda b, h, i: (b * nq + i, h)),
        compiler_params=_cparams(("arbitrary", "arbitrary", "arbitrary")),
        name="attn_a",
    )(lv, gsub, aq, *([ak] * len(srcs)), *([av] * len(srcs)))


def _attn_c_kernel(*refs, n_src):
    q_ref = refs[0]
    k_refs = refs[1:1 + n_src]
    v_refs = refs[1 + n_src:1 + 2 * n_src]
    o_ref = refs[1 + 2 * n_src]
    for hh in range(C_HEADS_PER_STEP):
        qk = slice(hh * C_HEAD_PAD, (hh + 1) * C_HEAD_PAD)
        dv = slice(hh * C_DV, (hh + 1) * C_DV)
        o = _softmax_pv_t(q_ref[:, qk], [k.at[:, qk] for k in k_refs], [v.at[:, dv] for v in v_refs])
        o_ref[:, dv] = o.T.astype(BF16)


def _attn_c(cq, ck, cv, q_row0, q_rows, srcs, n_batch):
    tq = min(TQ_C, q_rows)
    nq = q_rows // tq
    qb0 = q_row0 // tq
    hp = C_HEADS_PER_STEP
    in_specs = [pl.BlockSpec((tq, hp * C_HEAD_PAD), lambda b, h, i: (qb0 + b * nq + i, h))]
    k_specs = [pl.BlockSpec((rows, hp * C_HEAD_PAD), lambda b, h, i, blk0=row0 // rows: (blk0 + b, h))
               for row0, rows in srcs]
    v_specs = [pl.BlockSpec((rows, hp * C_DV), lambda b, h, i, blk0=row0 // rows: (blk0 + b, h))
               for row0, rows in srcs]
    return pl.pallas_call(
        functools.partial(_attn_c_kernel, n_src=len(srcs)),
        out_shape=jax.ShapeDtypeStruct((n_batch * q_rows, C_HEADS * C_DV), BF16),
        grid=(n_batch, C_HEADS // hp, nq),
        in_specs=in_specs + k_specs + v_specs,
        out_specs=pl.BlockSpec((tq, hp * C_DV), lambda b, h, i: (b * nq + i, h)),
        compiler_params=_cparams(("arbitrary", "arbitrary", "arbitrary")),
        name="attn_c",
    )(cq, *([ck] * len(srcs)), *([cv] * len(srcs)))


def _stack_heads(q):
    return jnp.concatenate([q[:, g * LANES:(g + 1) * LANES] for g in range(B_GROUP)], axis=0)


def _sink_row(sink_ref, kvh, cols):
    return jnp.concatenate(
        [jnp.full((1, cols), sink_ref[kvh * B_GROUP + g] * LOG2E, F32) for g in range(B_GROUP)], axis=1)


def _attn_b_kernel(sink_ref, q_ref, kp_ref, km_ref, kn_ref, vp_ref, vm_ref, vn_ref, kc_ref, vc_ref, o_ref):
    kvh = pl.program_id(1)
    qb = pl.program_id(2)
    nqb = pl.num_programs(2)
    n_blk = QB_B // BLOCK
    kband = jnp.concatenate([kp_ref[...], km_ref[...], kn_ref[...]], axis=0)
    vband = jnp.concatenate([vp_ref[...], vm_ref[...], vn_ref[...]], axis=0)
    kc, vc = kc_ref[...], vc_ref[...]
    sink = _sink_row(sink_ref, kvh, BLOCK)
    c = lax.broadcasted_iota(jnp.int32, (3 * BLOCK, B_GROUP * BLOCK), 0)
    r = lax.broadcasted_iota(jnp.int32, (3 * BLOCK, B_GROUP * BLOCK), 1) % BLOCK
    cr = c - r
    band_ok = (cr >= 0) & (cr <= BLOCK + WINDOW)
    for j in range(n_blk):
        q4 = _stack_heads(q_ref[j * BLOCK:(j + 1) * BLOCK, :])
        s_loc = _dot_nt(kband[j * BLOCK:(j + 3) * BLOCK], q4)
        valid = band_ok
        if j == 0:
            valid = valid & (c >= jnp.where(qb > 0, 0, BLOCK))
        if j == n_blk - 1:
            valid = valid & (c < jnp.where(qb < nqb - 1, 3 * BLOCK, 2 * BLOCK))
        s_loc = jnp.where(valid, s_loc, NEG_INF)
        s_ctx = _dot_nt(kc, q4)
        m = jnp.maximum(jnp.maximum(jnp.max(s_loc, axis=0, keepdims=True),
                                    jnp.max(s_ctx, axis=0, keepdims=True)), sink)
        e_loc, e_ctx = jnp.exp2(s_loc - m), jnp.exp2(s_ctx - m)
        l = (jnp.sum(e_loc, axis=0, keepdims=True) + jnp.sum(e_ctx, axis=0, keepdims=True)
             + jnp.exp2(sink - m))
        o = (_dot_tn(vband[j * BLOCK:(j + 3) * BLOCK], e_loc.astype(BF16))
             + _dot_tn(vc, e_ctx.astype(BF16))) * (1.0 / l)
        o = o.T
        for g in range(B_GROUP):
            o_ref[j * BLOCK:(j + 1) * BLOCK, g * LANES:(g + 1) * LANES] = (
                o[g * BLOCK:(g + 1) * BLOCK].astype(BF16))


def _attn_b(sink, bq, bk, bv, n_batch, n_lat, n_ctx):
    nqb = n_lat // QB_B
    per = QB_B // BLOCK
    blocks_per_batch = n_lat // BLOCK
    ctx_blk0 = n_batch * n_lat // n_ctx
    gw = B_GROUP * B_DH
    prev = lambda b, h, i: (b * blocks_per_batch + jnp.maximum(i * per - 1, 0), h)
    main = lambda b, h, i: (b * nqb + i, h)
    nxt = lambda b, h, i: (b * blocks_per_batch + jnp.minimum(i * per + per, blocks_per_batch - 1), h)
    ctx = lambda b, h, i: (ctx_blk0 + b, h)
    return pl.pallas_call(
        _attn_b_kernel,
        out_shape=jax.ShapeDtypeStruct((n_batch * n_lat, B_HEADS * B_DH), BF16),
        grid=(n_batch, B_KV_HEADS, nqb),
        in_specs=[
            pl.BlockSpec(memory_space=pltpu.SMEM),
            pl.BlockSpec((QB_B, gw), main),
            pl.BlockSpec((BLOCK, B_DH), prev), pl.BlockSpec((QB_B, B_DH), main), pl.BlockSpec((BLOCK, B_DH), nxt),
            pl.BlockSpec((BLOCK, B_DH), prev), pl.BlockSpec((QB_B, B_DH), main), pl.BlockSpec((BLOCK, B_DH), nxt),
            pl.BlockSpec((n_ctx, B_DH), ctx), pl.BlockSpec((n_ctx, B_DH), ctx),
        ],
        out_specs=pl.BlockSpec((QB_B, gw), main),
        compiler_params=_cparams(("arbitrary", "arbitrary", "arbitrary")),
        name="attn_b",
    )(sink, bq, bk, bk, bk, bv, bv, bv, bk, bv)


def _attn_b_ctx_kernel(sink_ref, q_ref, k_ref, v_ref, o_ref):
    kvh = pl.program_id(1)
    rows = q_ref.shape[0]
    q4 = _stack_heads(q_ref[...])
    sink = _sink_row(sink_ref, kvh, rows)
    s = _dot_nt(k_ref[...], q4)
    m = jnp.maximum(jnp.max(s, axis=0, keepdims=True), sink)
    e = jnp.exp2(s - m)
    l = jnp.sum(e, axis=0, keepdims=True) + jnp.exp2(sink - m)
    o = (_dot_tn(v_ref[...], e.astype(BF16)) * (1.0 / l)).T
    for g in range(B_GROUP):
        o_ref[:, g * LANES:(g + 1) * LANES] = o[g * rows:(g + 1) * rows].astype(BF16)


def _attn_b_ctx(sink, bq, bk, bv, n_batch, n_lat, n_ctx):
    gw = B_GROUP * B_DH
    ctx_blk0 = n_batch * n_lat // n_ctx
    return pl.pallas_call(
        _attn_b_ctx_kernel,
        out_shape=jax.ShapeDtypeStruct((n_batch * n_ctx, B_HEADS * B_DH), BF16),
        grid=(n_batch, B_KV_HEADS),
        in_specs=[
            pl.BlockSpec(memory_space=pltpu.SMEM),
            pl.BlockSpec((n_ctx, gw), lambda b, h: (ctx_blk0 + b, h)),
            pl.BlockSpec((n_ctx, B_DH), lambda b, h: (ctx_blk0 + b, h)),
            pl.BlockSpec((n_ctx, B_DH), lambda b, h: (ctx_blk0 + b, h)),
        ],
        out_specs=pl.BlockSpec((n_ctx, gw), lambda b, h: (b, h)),
        compiler_params=_cparams(("arbitrary", "arbitrary")),
        name="attn_b_ctx",
    )(sink, bq, bk, bv)


def _route(h2, rw_ref, rb_ref):
    tm = h2.shape[0]
    h_hi = h2.astype(BF16)
    h_lo = (h2 - h_hi.astype(F32)).astype(BF16)
    p = jnp.dot(jnp.concatenate([h_hi, h_lo], axis=0), rw_ref[...], preferred_element_type=F32)
    p = p[:tm] + p[tm:]
    logits = p[:, :N_EXPERTS] + p[:, N_EXPERTS:]
    scores = 1.0 / (1.0 + jnp.exp(-logits))
    sel = scores + rb_ref[...]
    lane_i = lax.broadcasted_iota(jnp.int32, sel.shape, 1)
    lane = lane_i.astype(F32)
    big = float(N_EXPERTS)

    def top2(mask):
        v = jnp.where(mask, sel, -jnp.inf)
        m1 = jnp.max(v, axis=-1, keepdims=True)
        i1 = jnp.min(jnp.where(v == m1, lane, big), axis=-1, keepdims=True)
        v2 = jnp.where(lane == i1, -jnp.inf, v)
        m2 = jnp.max(v2, axis=-1, keepdims=True)
        i2 = jnp.min(jnp.where(v2 == m2, lane, big), axis=-1, keepdims=True)
        return m1, i1, m2, i2

    best = None
    for g in range(N_GROUPS):
        m1, i1, m2, i2 = top2((lane_i >= g * EXPERTS_PER_GROUP) & (lane_i < (g + 1) * EXPERTS_PER_GROUP))
        gs = m1 + m2
        if best is None:
            best = (gs, i1, i2)
        else:
            take = gs > best[0]
            best = (jnp.where(take, gs, best[0]), jnp.where(take, i1, best[1]), jnp.where(take, i2, best[2]))
    _, e1, e2 = best
    w1 = jnp.sum(jnp.where(lane == e1, scores, 0.0), axis=-1, keepdims=True)
    w2 = jnp.sum(jnp.where(lane == e2, scores, 0.0), axis=-1, keepdims=True)
    tot = w1 + w2
    return e1, e2, w1 / tot, w2 / tot


def _outproj_kernel(oa1_ref, ob1_ref, oc1_ref, x1in_ref, oa2_ref, ob2_ref, oc2_ref, x2in_ref,
                    g1_ref, sh2_ref, sc2_ref, n2_ref, w_hbm, rw_ref, rb_ref,
                    x1_ref, h2_ref, idx_ref, wt_ref, cnt_ref, w_ref, w_stage, w_sem, cnt_acc,
                    *, n_a_tiles, layer):
    @pl.when(pl.program_id(0) == 0)
    def _():
        _load_weight_bf16(w_hbm, layer, w_stage, w_sem, w_ref)

    first = pl.program_id(0) < n_a_tiles
    oa = jnp.where(first, oa1_ref[...], oa2_ref[...])
    ob = jnp.where(first, ob1_ref[...], ob2_ref[...])
    oc = jnp.where(first, oc1_ref[...], oc2_ref[...])
    x = jnp.where(first, x1in_ref[...], x2in_ref[...])
    y = jnp.dot(jnp.concatenate([oa, ob, oc], axis=1), w_ref[...], preferred_element_type=F32)
    x1 = x + g1_ref[0] * y
    x1_ref[...] = x1
    ms = jnp.mean(x1 * x1, axis=-1, keepdims=True)
    h2 = x1 * lax.rsqrt(ms + EPS) * n2_ref[...]
    h2 = h2 * (1.0 + sc2_ref[0]) + sh2_ref[0]
    tm = h2.shape[0]
    for cc in range(ROW_CHUNKS):
        h2_ref[pl.ds(cc, tm, stride=ROW_CHUNKS), :] = h2[:, cc * LANES:(cc + 1) * LANES]
    e1, e2, w1, w2 = _route(h2, rw_ref, rb_ref)
    lane = lax.broadcasted_iota(jnp.int32, idx_ref.shape, 1)
    wt_ref[...] = jnp.where(lane == 0, w1, jnp.where(lane == 1, w2, 0.0))

    @pl.when(pl.program_id(0) == 0)
    def _():
        cnt_acc[...] = jnp.zeros_like(cnt_acc)

    lane_f = lane.astype(F32)
    sel1, sel2 = lane_f == e1, lane_f == e2
    picks = jnp.where(sel1, 1.0, 0.0) + jnp.where(sel2, 1.0, 0.0)
    t_row = lax.broadcasted_iota(jnp.int32, (tm, tm), 0)
    t_col = lax.broadcasted_iota(jnp.int32, (tm, tm), 1)
    earlier = jnp.where(t_col < t_row, 1.0, 0.0).astype(BF16)
    before = jnp.dot(earlier, picks.astype(BF16), preferred_element_type=F32) + cnt_acc[0:1, :]
    r1 = jnp.sum(jnp.where(sel1, before, 0.0), axis=-1, keepdims=True)
    r2 = jnp.sum(jnp.where(sel2, before, 0.0), axis=-1, keepdims=True)
    idx_ref[...] = jnp.where(lane == 0, e1, jnp.where(lane == 1, e2, jnp.where(
        lane == 2, r1, jnp.where(lane == 3, r2, 0.0)))).astype(jnp.int32)
    cnt_acc[...] = cnt_acc[...] + jnp.sum(picks, axis=0, keepdims=True)
    cnt_ref[...] = cnt_acc[...]


def _outproj(layer, set_a, set_b, n_tiles, n_a_tiles, mod48, mod_row, n2, w_out, router_w, router_b):
    d = set_a[3].shape[1]
    tm = TM_PREP
    m = n_tiles * tm
    row = lambda i: (i, 0)
    const = lambda i: (0, 0)
    first, second = _split_rows(n_a_tiles)
    mod_spec = lambda j: pl.BlockSpec((1, 1, d), lambda i: (mod_row(i) * 6 + j, 0, 0))
    in_specs = (
        [pl.BlockSpec((tm, a.shape[1]), first) for a in set_a]
        + [pl.BlockSpec((tm, a.shape[1]), second) for a in set_b]
        + [mod_spec(2), mod_spec(3), mod_spec(4),
           pl.BlockSpec((1, d), const),
           pl.BlockSpec(memory_space=pl.ANY),
           pl.BlockSpec((d, 2 * N_EXPERTS), const),
           pl.BlockSpec((1, N_EXPERTS), const)])
    return pl.pallas_call(
        functools.partial(_outproj_kernel, n_a_tiles=n_a_tiles, layer=layer),
        out_shape=[jax.ShapeDtypeStruct((m, d), F32), jax.ShapeDtypeStruct((m * ROW_CHUNKS, LANES), F32),
                   jax.ShapeDtypeStruct((m, LANES), jnp.int32), jax.ShapeDtypeStruct((m, LANES), F32),
                   jax.ShapeDtypeStruct((8, LANES), F32)],
        grid=(n_tiles,),
        in_specs=in_specs,
        out_specs=[pl.BlockSpec((tm, d), row), pl.BlockSpec((tm * ROW_CHUNKS, LANES), row),
                   pl.BlockSpec((tm, LANES), row), pl.BlockSpec((tm, LANES), row),
                   pl.BlockSpec((8, LANES), const)],
        scratch_shapes=[pltpu.VMEM((d, d), BF16),
                        pltpu.VMEM((2, W_STAGE_ROWS, d), F32),
                        pltpu.SemaphoreType.DMA((2,)),
                        pltpu.VMEM((8, LANES), F32)],
        compiler_params=_cparams(("arbitrary",)),
        name="outproj",
    )(*set_a, *set_b, mod48, mod48, mod48, n2, w_out, router_w, router_b)


def _experts_kernel(te_ref, nxt_ref, ws_ref, nv_ref, src_ref, h2_hbm, wg_hbm, wu_hbm, wd_hbm, o_ref,
                    xbuf, xsem, wg_f, wu_f, wd_f, wsem, wg_bf, wu_bf, wd_bf, *, layer):
    i = pl.program_id(0)
    n_valid = nv_ref[0]
    slot = i % X_SLOTS

    def issue_rows(tile):
        s = tile % X_SLOTS
        base = tile * TM_E
        for r in range(TM_E):
            row0 = pl.multiple_of(src_ref[base + r] * ROW_CHUNKS, ROW_CHUNKS)
            pltpu.make_async_copy(h2_hbm.at[pl.ds(row0, ROW_CHUNKS), :],
                                  xbuf.at[s, pl.ds(r * X_PITCH, ROW_CHUNKS), :], xsem.at[s]).start()

    def wait_rows(s):
        pltpu.make_async_copy(h2_hbm.at[pl.ds(0, TM_E * ROW_CHUNKS), :],
                              xbuf.at[s, pl.ds(0, TM_E * ROW_CHUNKS), :], xsem.at[s]).wait()

    def weight_copies(e, s):
        return (pltpu.make_async_copy(wg_hbm.at[layer, e], wg_f.at[s], wsem.at[s]),
                pltpu.make_async_copy(wu_hbm.at[layer, e], wu_f.at[s], wsem.at[s]),
                pltpu.make_async_copy(wd_hbm.at[layer, e], wd_f.at[s], wsem.at[s]))

    @pl.when(i == 0)
    def _():
        for cp in weight_copies(te_ref[0], ws_ref[0]):
            cp.start()
        for t in range(X_SLOTS - 1):
            issue_rows(t)

    first_of_expert = jnp.logical_or(i == 0, te_ref[i] != te_ref[jnp.maximum(i - 1, 0)])

    @pl.when(jnp.logical_and(first_of_expert, i < n_valid))
    def _():
        ws = ws_ref[i]
        for cp in weight_copies(te_ref[i], ws):
            cp.wait()

        @pl.when(nxt_ref[i] >= 0)
        def _():
            for cp in weight_copies(nxt_ref[i], 1 - ws):
                cp.start(priority=WEIGHT_DMA_PRIORITY)

        wg_bf[...] = wg_f[ws].astype(BF16)
        wu_bf[...] = wu_f[ws].astype(BF16)
        wd_bf[...] = wd_f[ws].astype(BF16)

    @pl.when(i < n_valid)
    def _():
        wait_rows(slot)
        x = jnp.concatenate([xbuf[slot, pl.ds(cc, TM_E, stride=X_PITCH), :] for cc in range(ROW_CHUNKS)],
                            axis=1).astype(BF16)
        issue_rows(i + X_SLOTS - 1)
        g = jnp.dot(x, wg_bf[...], preferred_element_type=F32)
        u = jnp.dot(x, wu_bf[...], preferred_element_type=F32)
        a = (_silu(g) * u).astype(BF16)
        y = jnp.dot(a, wd_bf[...], preferred_element_type=F32)
        for cc in range(ROW_CHUNKS):
            o_ref[pl.ds(cc, TM_E, stride=ROW_CHUNKS), :] = y[:, cc * LANES:(cc + 1) * LANES]

    @pl.when(i == n_valid - 1)
    def _():
        for t in range(1, X_SLOTS):
            wait_rows((i + t) % X_SLOTS)

    @pl.when(i >= n_valid)
    def _():
        o_ref[...] = jnp.zeros_like(o_ref)


def _experts(layer, tile_expert, next_expert, w_slot, n_valid, src, h2_all, w_gate, w_up, w_down):
    n_rows = src.shape[0] - (X_SLOTS - 1) * TM_E
    nt = n_rows // TM_E
    d, de = w_gate.shape[2], w_gate.shape[3]
    grid_spec = pltpu.PrefetchScalarGridSpec(
        num_scalar_prefetch=5,
        grid=(nt,),
        in_specs=[pl.BlockSpec(memory_space=pl.ANY)] * 4,
        out_specs=pl.BlockSpec((TM_E * ROW_CHUNKS, LANES), lambda i, *_: (i, 0)),
        scratch_shapes=[
            pltpu.VMEM((X_SLOTS, TM_E * X_PITCH, LANES), F32),
            pltpu.SemaphoreType.DMA((X_SLOTS,)),
            pltpu.VMEM((2, d, de), F32), pltpu.VMEM((2, d, de), F32), pltpu.VMEM((2, de, d), F32),
            pltpu.SemaphoreType.DMA((2,)),
            pltpu.VMEM((d, de), BF16), pltpu.VMEM((d, de), BF16), pltpu.VMEM((de, d), BF16),
        ],
    )
    return pl.pallas_call(
        functools.partial(_experts_kernel, layer=layer),
        out_shape=jax.ShapeDtypeStruct((n_rows * ROW_CHUNKS, LANES), F32),
        grid_spec=grid_spec,
        compiler_params=_cparams(("arbitrary",)),
        name="experts",
    )(tile_expert, next_expert, w_slot, n_valid, src, h2_all, w_gate, w_up, w_down)


def _combine_kernel(pos_ref, o_hbm, x1_ref, wt_ref, g2_ref, x2_ref, buf, sem):
    i = pl.program_id(0)
    nt = pl.num_programs(0)
    slot = i % 2

    def issue(tile, s):
        base = tile * (TM_C * TOP_K)
        for r in range(TM_C):
            for k in range(TOP_K):
                row0 = pl.multiple_of(pos_ref[base + TOP_K * r + k] * ROW_CHUNKS, ROW_CHUNKS)
                pltpu.make_async_copy(o_hbm.at[pl.ds(row0, ROW_CHUNKS), :],
                                      buf.at[s, k, pl.ds(r * X_PITCH, ROW_CHUNKS), :], sem.at[s]
                                      ).start(priority=k % DMA_QUEUES)

    @pl.when(i == 0)
    def _():
        issue(0, 0)

    @pl.when(i + 1 < nt)
    def _():
        issue(i + 1, 1 - slot)

    def expert_rows(k):
        return jnp.concatenate([buf[slot, k, pl.ds(cc, TM_C, stride=X_PITCH), :] for cc in range(ROW_CHUNKS)],
                               axis=1)

    for k in range(TOP_K):
        pltpu.make_async_copy(o_hbm.at[pl.ds(0, TM_C * ROW_CHUNKS), :],
                              buf.at[slot, k, pl.ds(0, TM_C * ROW_CHUNKS), :], sem.at[slot]).wait()
    wt = wt_ref[...]
    y = wt[:, 0:1] * expert_rows(0) + wt[:, 1:2] * expert_rows(1)
    x2_ref[...] = x1_ref[...] + g2_ref[0] * y


def _combine(pos, o_sorted, x1, wts, mod48, mod_row):
    m, d = x1.shape
    grid_spec = pltpu.PrefetchScalarGridSpec(
        num_scalar_prefetch=1,
        grid=(m // TM_C,),
        in_specs=[
            pl.BlockSpec(memory_space=pl.ANY),
            pl.BlockSpec((TM_C, d), lambda i, p: (i, 0)),
            pl.BlockSpec((TM_C, LANES), lambda i, p: (i, 0)),
            pl.BlockSpec((1, 1, d), lambda i, p: (mod_row(i) * 6 + 5, 0, 0)),
        ],
        out_specs=pl.BlockSpec((TM_C, d), lambda i, p: (i, 0)),
        scratch_shapes=[pltpu.VMEM((2, TOP_K, TM_C * X_PITCH, LANES), F32), pltpu.SemaphoreType.DMA((2,))],
    )
    return pl.pallas_call(
        _combine_kernel,
        out_shape=jax.ShapeDtypeStruct((m, d), F32),
        grid_spec=grid_spec,
        compiler_params=_cparams(("arbitrary",)),
        name="combine",
    )(pos, o_sorted, x1, wts, mod48)


def _rope_tables(n_lat, tm):
    f32 = np.float32
    t = np.arange(n_lat)
    r = (t // GRID_W).astype(f32)
    col = (t % GRID_W).astype(f32)

    def cos_sin(dim):
        nf = dim // 4
        inv = (f32(ROPE_THETA) ** (-np.arange(nf, dtype=f32) / f32(nf))).astype(f32)
        ang = np.concatenate([r[:, None] * inv, col[:, None] * inv], axis=-1).astype(f32)
        return np.cos(ang).astype(f32), np.sin(ang).astype(f32)

    c64, s64 = cos_sin(A_DK)
    c128, s128 = cos_sin(B_DH)
    z32 = np.zeros_like(s64)
    tabs = [
        np.concatenate([c64, c64, c64, c64], axis=-1),
        np.concatenate([-s64, z32, -s64, z32], axis=-1),
        np.concatenate([z32, s64, z32, s64], axis=-1),
        np.concatenate([c128, c128], axis=-1),
        np.concatenate([-s128, s128], axis=-1),
    ]
    ident = [np.ones((tm, LANES), f32), np.zeros((tm, LANES), f32), np.zeros((tm, LANES), f32),
             np.ones((tm, LANES), f32), np.zeros((tm, LANES), f32)]
    return [jnp.asarray(np.concatenate([a, b], axis=0)) for a, b in zip(tabs, ident)]


def _layer_params(l, a_qn, a_kn, b_qn, b_kn, c_qa_norm, c_kva_norm, c_wuq, c_wukv, c_qn, c_kn):
    z64 = jnp.zeros((C_ROPE,), F32)
    gains = jnp.stack([
        jnp.tile(a_qn[l], 2) * (A_DK ** -0.5 * LOG2E),
        jnp.tile(a_kn[l], 2),
        b_qn[l] * (B_DH ** -0.5 * LOG2E),
        b_kn[l],
        c_qn[l][:C_NOPE] * (C_DQK ** -0.5 * LOG2E),
        jnp.concatenate([c_qn[l][C_NOPE:] * (C_DQK ** -0.5 * LOG2E), z64]),
        c_kn[l][:C_NOPE],
        jnp.concatenate([c_kn[l][C_NOPE:], z64]),
    ])
    wq = c_wuq[l].reshape(C_Q_RANK, C_HEADS, C_DQK)
    wq = jnp.pad(wq, ((0, 0), (0, 0), (0, C_HEAD_PAD - C_DQK))).reshape(C_Q_RANK, C_HEADS * C_HEAD_PAD)
    return dict(gains=gains, gcq=c_qa_norm[l][None], gckv=c_kva_norm[l][None],
                wuq=wq.astype(BF16), wukv=c_wukv[l].astype(BF16))


def _sorted_rows(idx, rank, counts):
    t = idx.shape[0]
    n_pairs = t * TOP_K
    n_rows = ((n_pairs + N_EXPERTS * (TM_E - 1)) // TM_E) * TM_E
    nt = n_rows // TM_E
    flat_e = idx.reshape(-1)
    rank = rank.reshape(-1)
    padded = ((counts + TM_E - 1) // TM_E) * TM_E
    ends = jnp.cumsum(padded)
    pos = (ends - padded)[flat_e] + rank
    src = jnp.zeros((n_rows + (X_SLOTS - 1) * TM_E,), jnp.int32).at[pos].set(
        jnp.arange(n_pairs, dtype=jnp.int32) // TOP_K, unique_indices=True)
    tile_start = jnp.arange(nt, dtype=jnp.int32) * TM_E
    tile_expert = jnp.sum((ends[None, :] <= tile_start[:, None]).astype(jnp.int32), axis=1)
    last_used = jnp.sum((ends <= ends[-1] - 1).astype(jnp.int32))
    tile_expert = jnp.minimum(tile_expert, last_used)
    n_valid = (ends[-1:] // TM_E).astype(jnp.int32)
    used = counts > 0
    e_ids = jnp.arange(N_EXPERTS, dtype=jnp.int32)
    later = jnp.where(used[None, :] & (e_ids[None, :] > e_ids[:, None]), e_ids[None, :], N_EXPERTS)
    next_used = jnp.min(later, axis=1)
    next_used = jnp.where(next_used == N_EXPERTS, -1, next_used).astype(jnp.int32)
    parity = ((jnp.cumsum(used.astype(jnp.int32)) - 1) % 2).astype(jnp.int32)
    return (pos.astype(jnp.int32), src, tile_expert.astype(jnp.int32), next_used[tile_expert],
            parity[tile_expert], n_valid)


def kernel(x, c, ctx, c_ctx, ada_w, ada_b, norm1_g, norm2_g, w_in, w_out, a_qn, a_kn, a_lambda, a_subln,
           b_qn, b_kn, b_sink, c_qa_norm, c_kva_norm, c_wuq, c_wukv, c_qn, c_kn,
           router_w, router_bias, moe_w_gate, moe_w_up, moe_w_down):
    bsz, n_lat, d = x.shape
    n_ctx = ctx.shape[1]
    depth = ada_w.shape[0]
    t_lat, t_ctx = bsz * n_lat, bsz * n_ctx
    tm = TM_PREP
    lat_tiles = n_lat // tm
    n_lat_tiles, n_ctx_tiles = t_lat // tm, t_ctx // tm
    n_all_tiles = n_lat_tiles + n_ctx_tiles

    cond8 = jnp.concatenate([c, c_ctx[None], jnp.zeros((8 - bsz - 1, d), F32)], axis=0)
    mod = _ada_modulation(cond8, ada_w, ada_b)
    tables = _rope_tables(n_lat, tm)
    rb = router_bias[None]
    rw_hi = router_w.astype(BF16)
    rw_lo = (router_w - rw_hi.astype(F32)).astype(BF16)
    rw2 = jnp.concatenate([rw_hi, rw_lo], axis=1)

    mod_row = lambda i: jnp.minimum(i // lat_tiles, bsz)
    mod_row_c = lambda i: jnp.minimum(i // (n_lat // TM_C), bsz)
    rope_blk = lambda i: jnp.where(i < n_lat_tiles, i % lat_tiles, lat_tiles)

    xa, xb, n_a_tiles = x.reshape(t_lat, d), ctx.reshape(t_ctx, d), n_lat_tiles
    lat_src, ctx_src = (0, n_lat), (t_lat, n_ctx)
    for l in range(depth):
        last = l == depth - 1
        p = _layer_params(l, a_qn, a_kn, b_qn, b_kn, c_qa_norm, c_kva_norm, c_wuq, c_wukv, c_qn, c_kn)
        mod48 = mod[l].reshape(8 * 6, 1, d)
        aq, ak, av, bq, bk, bv, cq, ck, cv = _prep(
            l, xa, xb, n_all_tiles, n_a_tiles, mod48, mod_row, norm1_g[l][None], w_in, tables, rope_blk,
            p["gains"], p["gcq"], p["gckv"], p["wuq"], p["wukv"])

        lv, gsub, sink = a_lambda[l], a_subln[l][None], b_sink[l]
        o_a = _attn_a(lv, gsub, aq, ak, av, 0, n_lat, [lat_src, ctx_src], bsz, l)
        o_b = _attn_b(sink, bq, bk, bv, bsz, n_lat, n_ctx)
        o_c = _attn_c(cq, ck, cv, 0, n_lat, [lat_src, ctx_src], bsz)
        set_a = (o_a, o_b, o_c, xa)
        if not last:
            oc_a = _attn_a(lv, gsub, aq, ak, av, t_lat, n_ctx, [ctx_src], bsz, l)
            oc_b = _attn_b_ctx(sink, bq, bk, bv, bsz, n_lat, n_ctx)
            oc_c = _attn_c(cq, ck, cv, t_lat, n_ctx, [ctx_src], bsz)
            set_b, n_tok_tiles = (oc_a, oc_b, oc_c, xb), n_all_tiles
        else:
            set_b, n_tok_tiles = set_a, n_lat_tiles
        x1, h2, idx, wts, cnt = _outproj(l, set_a, set_b, n_tok_tiles, min(n_a_tiles, n_tok_tiles), mod48,
                                         mod_row, norm2_g[l][None], w_out, rw2, rb)

        pos, src, tile_expert, next_expert, w_slot, n_valid = _sorted_rows(
            idx[:, :TOP_K], idx[:, TOP_K:2 * TOP_K], cnt[0, :N_EXPERTS].astype(jnp.int32))
        o_sorted = _experts(l, tile_expert, next_expert, w_slot, n_valid, src, h2,
                            moe_w_gate, moe_w_up, moe_w_down)
        xa = _combine(pos, o_sorted, x1, wts, mod48, mod_row_c)
        xb, n_a_tiles = xa, n_all_tiles
    return xa.reshape(bsz, n_lat, d)
```
